```python
import math
import jax, jax.numpy as jnp
from jax import lax
import numpy as np

D_MODEL = 2048
BATCH = 2
SEQ = 4096
DEPTH = 1
DEC_BATCH = 32
DEC_SEQ = 4
PAST_LEN = 16384
PAGE_SIZE = 128

DA_HEADS = 8
DA_VD = (D_MODEL // 2) // DA_HEADS
DA_HD = DA_VD // 2
DA_ROT = DA_HD // 4
ROPE_THETA = 500000.0
Q_BLOCK = 128
RET_HEADS = 8
RET_VD = (D_MODEL // 2) // RET_HEADS
RET_KD = RET_VD // 2
RET_THETA = 10000.0
RET_CHUNK = 128
N_GROUPS = 4
EXPERTS_PER_GROUP = 8
N_EXPERTS = N_GROUPS * EXPERTS_PER_GROUP
TOP_K = 2
D_FF_EXPERT = D_MODEL // 2
MOE_BLOCK = 128
PLE_DIM = 256
EPS = 1e-6
NEG = -1e30

DA_QK_W = DA_HEADS * 2 * DA_HD
DA_V_W = DA_HEADS * DA_VD
RET_QK_W = RET_HEADS * RET_KD
RET_V_W = RET_HEADS * RET_VD
IN_COLS = 2 * DA_QK_W + DA_V_W + 2 * RET_QK_W + 2 * RET_V_W + 2 * D_MODEL

kernel_name = 'hybrid_diffattn_retention_hmoe_step'

F32 = jnp.float32


def rms_norm(x, g):
    x32 = x.astype(F32)
    y = x32 * lax.rsqrt(jnp.mean(x32 * x32, axis=-1, keepdims=True) + EPS)
    return (y * g.astype(F32)).astype(x.dtype)


def head_rms(x, g):
    h, dv = x.shape[-2:]
    y = x * lax.rsqrt(jnp.mean(x * x, axis=-1, keepdims=True) + EPS)
    return y * g.astype(F32).reshape(h, dv)


def head_group_norm(x, g):
    h, dv = x.shape[-2:]
    xc = x - jnp.mean(x, axis=-1, keepdims=True)
    y = xc * lax.rsqrt(jnp.mean(xc * xc, axis=-1, keepdims=True) + EPS)
    return y * g.astype(F32).reshape(h, dv)


def rope(x, pos, rot_dim, theta):
    half = rot_dim // 2
    inv = 1.0 / jnp.power(jnp.asarray(theta, F32), jnp.arange(half, dtype=F32) * (2.0 / rot_dim))
    ang = pos.astype(F32)[:, None] * inv[None, :]
    cos = jnp.cos(ang)[None, :, None, :]
    sin = jnp.sin(ang)[None, :, None, :]
    xr = x[..., :rot_dim].astype(F32)
    x1, x2 = xr[..., :half], xr[..., half:]
    rot = jnp.concatenate([x1 * cos - x2 * sin, x2 * cos + x1 * sin], axis=-1)
    return jnp.concatenate([rot.astype(x.dtype), x[..., rot_dim:]], axis=-1)


def project(hn, pos, w_in):
    b, s = hn.shape[:2]
    z = hn @ w_in
    sizes = [DA_QK_W, DA_QK_W, DA_V_W, RET_QK_W, RET_QK_W, RET_V_W, RET_V_W, D_MODEL, D_MODEL]
    idx = np.cumsum(sizes)[:-1].tolist()
    q, k, v, rq, rk, rv, rg, ga, gr = jnp.split(z, idx, axis=-1)
    q = rope(q.reshape(b, s, DA_HEADS * 2, DA_HD), pos, DA_ROT, ROPE_THETA).reshape(b, s, DA_HEADS, 2, DA_HD)
    k = rope(k.reshape(b, s, DA_HEADS * 2, DA_HD), pos, DA_ROT, ROPE_THETA).reshape(b, s, DA_HEADS, 2, DA_HD)
    v = v.reshape(b, s, DA_HEADS, DA_VD)
    rq = rope(rq.reshape(b, s, RET_HEADS, RET_KD), pos, RET_KD, RET_THETA)
    rk = rope(rk.reshape(b, s, RET_HEADS, RET_KD), pos, RET_KD, RET_THETA) * (RET_KD ** -0.5)
    rv = rv.reshape(b, s, RET_HEADS, RET_VD)
    return q, k, v, rq, rk, rv, rg, ga, gr


def diff_attn_prompt(q, k, v, lam):
    b, s = q.shape[:2]
    nqb = s // Q_BLOCK
    scale = DA_HD ** -0.5
    qb = q.reshape(b, nqb, Q_BLOCK, DA_HEADS, 2, DA_HD).transpose(1, 0, 2, 3, 4, 5)
    starts = jnp.arange(nqb) * Q_BLOCK
    kpos = jnp.arange(s)
    v32 = v.astype(F32)

    def block(args):
        qblk, start = args
        sc = jnp.einsum('bqhmd,bkhmd->bhmqk', qblk, k, preferred_element_type=F32) * scale
        qpos = start + jnp.arange(Q_BLOCK)
        sc = jnp.where(kpos[None, :] <= qpos[:, None], sc, NEG)
        p = jax.nn.softmax(sc, axis=-1)
        pd = p[:, :, 0] - lam * p[:, :, 1]
        return jnp.einsum('bhqk,bkhv->bqhv', pd, v32)

    o = lax.map(block, (qb, starts))
    return o.transpose(1, 0, 2, 3, 4).reshape(b, s, DA_HEADS, DA_VD)


def online_update(carry, sc, vals):
    m, l, acc = carry
    m_new = jnp.maximum(m, jnp.max(sc, axis=-1))
    alpha = jnp.exp(m - m_new)
    p = jnp.exp(sc - m_new[..., None])
    l = l * alpha + jnp.sum(p, axis=-1)
    acc = acc * alpha[..., None] + jnp.einsum('bhmtk,bkhv->bhmtv', p, vals.astype(F32))
    return (m_new, l, acc)


def diff_attn_paged(q, k_new, v_new, cache_k, cache_v, page_table, li, lam):
    db, t = q.shape[:2]
    scale = DA_HD ** -0.5
    init = (jnp.full((db, DA_HEADS, 2, t), NEG, F32),
            jnp.zeros((db, DA_HEADS, 2, t), F32),
            jnp.zeros((db, DA_HEADS, 2, t, DA_VD), F32))

    def step(carry, pids):
        kp = cache_k[li, pids]
        vp = cache_v[li, pids]
        sc = jnp.einsum('bthmd,bkhmd->bhmtk', q, kp, preferred_element_type=F32) * scale
        return online_update(carry, sc, vp), None

    carry, _ = lax.scan(step, init, page_table.T)
    sc = jnp.einsum('bthmd,bkhmd->bhmtk', q, k_new, preferred_element_type=F32) * scale
    causal = jnp.arange(t)[None, :] <= jnp.arange(t)[:, None]
    sc = jnp.where(causal, sc, NEG)
    m, l, acc = online_update(carry, sc, v_new)
    o = acc / l[..., None]
    out = o[:, :, 0] - lam * o[:, :, 1]
    return out.transpose(0, 2, 1, 3)


def retention_chunk(state, q, k, v, log_g):
    c = q.shape[1]
    idx = jnp.arange(c, dtype=F32)
    diff = idx[:, None] - idx[None, :]
    dmat = jnp.where(diff[None] >= 0, jnp.exp(jnp.maximum(diff, 0.0)[None] * log_g[:, None, None]), 0.0)
    att = jnp.einsum('bihd,bjhd->bhij', q, k) * dmat
    inner = jnp.einsum('bhij,bjhv->bihv', att, v)
    cross = jnp.einsum('bihd,bhdv->bihv', q, state) * jnp.exp((idx + 1.0)[:, None] * log_g[None, :])[None, :, :, None]
    decay_k = jnp.exp((c - 1.0 - idx)[:, None] * log_g[None, :])
    new_state = jnp.exp(c * log_g)[None, :, None, None] * state + jnp.einsum('bjhd,jh,bjhv->bhdv', k, decay_k, v)
    return new_state, inner + cross


def retention_prompt(q, k, v, log_g):
    b, s = q.shape[:2]
    nc = s // RET_CHUNK

    def to_chunks(a):
        return a.astype(F32).reshape(b, nc, RET_CHUNK, *a.shape[2:]).swapaxes(0, 1)

    s0 = jnp.zeros((b, RET_HEADS, RET_KD, RET_VD), F32)

    def step(st, xs):
        qc, kc, vc = xs
        return retention_chunk(st, qc, kc, vc, log_g)

    s_fin, o = lax.scan(step, s0, (to_chunks(q), to_chunks(k), to_chunks(v)))
    return o.swapaxes(0, 1).reshape(b, s, RET_HEADS, RET_VD), s_fin


def moe(xn, w_router_group, b_router_group, w_router_expert, b_router_expert, w_exp_gate, w_exp_up, w_exp_down):
    n = xn.shape[0]
    glog = (xn @ w_router_group).astype(F32) + b_router_group.astype(F32)
    gprob = jax.nn.softmax(glog, axis=-1)
    gidx = jnp.argmax(glog, axis=-1)
    gp = jnp.take_along_axis(gprob, gidx[:, None], axis=-1)[:, 0]
    elog = ((xn @ w_router_expert).astype(F32) + b_router_expert.astype(F32)).reshape(n, N_GROUPS, EXPERTS_PER_GROUP)
    elog_g = jnp.take_along_axis(elog, gidx[:, None, None], axis=1)[:, 0]
    top_v, top_i = lax.top_k(elog_g, TOP_K)
    wts = jax.nn.softmax(top_v, axis=-1) * gp[:, None]
    eid = gidx[:, None] * EXPERTS_PER_GROUP + top_i

    m = n * TOP_K
    e_flat = eid.reshape(m).astype(jnp.int32)
    tok_flat = jnp.repeat(jnp.arange(n, dtype=jnp.int32), TOP_K)
    w_flat = wts.reshape(m)
    order = jnp.argsort(e_flat)
    e_s, t_s, w_s = e_flat[order], tok_flat[order], w_flat[order]
    counts = jnp.bincount(e_flat, length=N_EXPERTS)
    starts = jnp.cumsum(counts) - counts
    pcounts = (counts + MOE_BLOCK - 1) // MOE_BLOCK * MOE_BLOCK
    pends = jnp.cumsum(pcounts)
    pstarts = pends - pcounts
    dest = pstarts[e_s] + (jnp.arange(m) - starts[e_s])
    nb = -(-m // MOE_BLOCK) + N_EXPERTS
    rows = nb * MOE_BLOCK
    row_tok = jnp.zeros((rows,), jnp.int32).at[dest].set(t_s)
    row_w = jnp.zeros((rows,), F32).at[dest].set(w_s)
    blk_e = jnp.minimum(jnp.searchsorted(pends, jnp.arange(nb) * MOE_BLOCK, side='right'), N_EXPERTS - 1)
    xb = xn[row_tok].reshape(nb, MOE_BLOCK, xn.shape[-1])

    def expert_block(args):
        xblk, e = args
        hg = xblk @ w_exp_gate[e]
        hu = xblk @ w_exp_up[e]
        return (jax.nn.silu(hg) * hu) @ w_exp_down[e]

    yb = lax.map(expert_block, (xb, blk_e)).reshape(rows, -1)
    return jax.ops.segment_sum(yb * row_w[:, None], row_tok, num_segments=n)


def finish(h, a, r, rg, ga, gr, pe, lam_init, da_norm_g, ret_norm_g, w_br_attn, w_br_ret, w_out,
           norm_ffn_g, w_router_group, b_router_group, w_router_expert, b_router_expert,
           w_exp_gate, w_exp_up, w_exp_down, norm_ple_g, w_ple_gate, w_ple_proj):
    b, s = h.shape[:2]
    a = (head_rms(a, da_norm_g) * (1.0 - lam_init)).reshape(b, s, DA_V_W)
    r = head_group_norm(r, ret_norm_g).reshape(b, s, RET_V_W)
    r = jax.nn.silu(rg.astype(F32)) * r
    mix = jax.nn.sigmoid(ga.astype(F32)) * (a @ w_br_attn) + jax.nn.sigmoid(gr.astype(F32)) * (r @ w_br_ret)
    h = h + (mix @ w_out).astype(h.dtype)
    hn = rms_norm(h, norm_ffn_g)
    y = moe(hn.reshape(b * s, D_MODEL), w_router_group, b_router_group, w_router_expert, b_router_expert,
            w_exp_gate, w_exp_up, w_exp_down)
    h = h + y.reshape(b, s, D_MODEL).astype(h.dtype)
    hn = rms_norm(h, norm_ple_g)
    h = h + (jax.nn.sigmoid((hn @ w_ple_gate).astype(F32)) * (pe @ w_ple_proj)).astype(h.dtype)
    return h


def setup_inputs(seed: int = 0) -> dict:
    key = jax.random.key(seed)
    ks = jax.random.split(key, 40)
    n_pages = PAST_LEN // PAGE_SIZE
    n_pool = (5 * DEC_BATCH * n_pages + 3) // 4

    def nrm(k, shape, scale=1.0):
        return jax.random.normal(k, shape, F32) * scale

    def gain(k, shape):
        return 1.0 + 0.02 * jax.random.normal(k, shape, F32)

    page_table = jax.random.permutation(ks[5], n_pool)[: DEC_BATCH * n_pages].reshape(DEC_BATCH, n_pages).astype(jnp.int32)
    return {
        'x_prompt': nrm(ks[0], (BATCH, SEQ, D_MODEL)),
        'x_sample': nrm(ks[1], (DEC_BATCH, DEC_SEQ, D_MODEL)),
        'cache_k': nrm(ks[2], (DEPTH, n_pool, PAGE_SIZE, DA_HEADS, 2, DA_HD)),
        'cache_v': nrm(ks[3], (DEPTH, n_pool, PAGE_SIZE, DA_HEADS, DA_VD)),
        'state_ret': nrm(ks[4], (DEPTH, DEC_BATCH, RET_HEADS, RET_KD, RET_VD)),
        'page_table': page_table,
        'p_prompt': nrm(ks[6], (DEPTH, BATCH, SEQ, PLE_DIM)),
        'p_sample': nrm(ks[7], (DEPTH, DEC_BATCH, DEC_SEQ, PLE_DIM)),
        'norm_mix_g': gain(ks[8], (DEPTH, D_MODEL)),
        'w_in': nrm(ks[9], (DEPTH, D_MODEL, IN_COLS), D_MODEL ** -0.5),
        'lam_q1': nrm(ks[10], (DEPTH, DA_HD), 0.1),
        'lam_k1': nrm(ks[11], (DEPTH, DA_HD), 0.1),
        'lam_q2': nrm(ks[12], (DEPTH, DA_HD), 0.1),
        'lam_k2': nrm(ks[13], (DEPTH, DA_HD), 0.1),
        'da_norm_g': gain(ks[14], (DEPTH, DA_V_W)),
        'ret_norm_g': gain(ks[15], (DEPTH, RET_V_W)),
        'w_br_attn': nrm(ks[16], (DEPTH, DA_V_W, D_MODEL), DA_V_W ** -0.5),
        'w_br_ret': nrm(ks[17], (DEPTH, RET_V_W, D_MODEL), RET_V_W ** -0.5),
        'w_out': nrm(ks[18], (DEPTH, D_MODEL, D_MODEL), D_MODEL ** -0.5),
        'norm_ffn_g': gain(ks[19], (DEPTH, D_MODEL)),
        'w_router_group': nrm(ks[20], (DEPTH, D_MODEL, N_GROUPS), D_MODEL ** -0.5),
        'b_router_group': nrm(ks[21], (DEPTH, N_GROUPS), 0.01),
        'w_router_expert': nrm(ks[22], (DEPTH, D_MODEL, N_EXPERTS), D_MODEL ** -0.5),
        'b_router_expert': nrm(ks[23], (DEPTH, N_EXPERTS), 0.01),
        'w_exp_gate': nrm(ks[24], (DEPTH, N_EXPERTS, D_MODEL, D_FF_EXPERT), D_MODEL ** -0.5),
        'w_exp_up': nrm(ks[25], (DEPTH, N_EXPERTS, D_MODEL, D_FF_EXPERT), D_MODEL ** -0.5),
        'w_exp_down': nrm(ks[26], (DEPTH, N_EXPERTS, D_FF_EXPERT, D_MODEL), D_FF_EXPERT ** -0.5),
        'norm_ple_g': gain(ks[27], (DEPTH, D_MODEL)),
        'w_ple_gate': nrm(ks[28], (DEPTH, D_MODEL, D_MODEL), D_MODEL ** -0.5),
        'w_ple_proj': nrm(ks[29], (DEPTH, PLE_DIM, D_MODEL), PLE_DIM ** -0.5),
        'final_norm_g': gain(ks[30], (D_MODEL,)),
    }


def reference(x_prompt, x_sample, cache_k, cache_v, state_ret, page_table, p_prompt, p_sample,
              norm_mix_g, w_in, lam_q1, lam_k1, lam_q2, lam_k2, da_norm_g, ret_norm_g,
              w_br_attn, w_br_ret, w_out, norm_ffn_g, w_router_group, b_router_group,
              w_router_expert, b_router_expert, w_exp_gate, w_exp_up, w_exp_down,
              norm_ple_g, w_ple_gate, w_ple_proj, final_norm_g):
    s_p = x_prompt.shape[1]
    t_s = x_sample.shape[1]
    pos_p = jnp.arange(s_p)
    pos_s = PAST_LEN + jnp.arange(t_s)
    log_g = jnp.log1p(-jnp.exp2(-5.0 - jnp.arange(RET_HEADS, dtype=F32)))
    hp, hs = x_prompt, x_sample
    kp_l, vp_l, rp_l, ks_l, vs_l, rs_l = [], [], [], [], [], []
    for li in range(DEPTH):
        lam_init = 0.8 - 0.6 * math.exp(-0.3 * li)
        lam = (jnp.exp(jnp.sum(lam_q1[li].astype(F32) * lam_k1[li].astype(F32)))
               - jnp.exp(jnp.sum(lam_q2[li].astype(F32) * lam_k2[li].astype(F32))) + lam_init)
        fin_w = (da_norm_g[li], ret_norm_g[li], w_br_attn[li], w_br_ret[li], w_out[li], norm_ffn_g[li],
                 w_router_group[li], b_router_group[li], w_router_expert[li], b_router_expert[li],
                 w_exp_gate[li], w_exp_up[li], w_exp_down[li], norm_ple_g[li], w_ple_gate[li], w_ple_proj[li])
        hn = rms_norm(hp, norm_mix_g[li])
        q, k, v, rq, rk, rv, rg, ga, gr = project(hn, pos_p, w_in[li])
        a = diff_attn_prompt(q, k, v, lam)
        r, st_p = retention_prompt(rq, rk, rv, log_g)
        hp = finish(hp, a, r, rg, ga, gr, p_prompt[li], lam_init, *fin_w)
        kp_l.append(k.astype(cache_k.dtype))
        vp_l.append(v.astype(cache_v.dtype))
        rp_l.append(st_p.astype(state_ret.dtype))
        hn = rms_norm(hs, norm_mix_g[li])
        q, k, v, rq, rk, rv, rg, ga, gr = project(hn, pos_s, w_in[li])
        a = diff_attn_paged(q, k, v, cache_k, cache_v, page_table, li, lam)
        st_s, r = retention_chunk(state_ret[li].astype(F32), rq.astype(F32), rk.astype(F32), rv.astype(F32), log_g)
        hs = finish(hs, a, r, rg, ga, gr, p_sample[li], lam_init, *fin_w)
        ks_l.append(k.astype(cache_k.dtype))
        vs_l.append(v.astype(cache_v.dtype))
        rs_l.append(st_s.astype(state_ret.dtype))
    y_prompt = rms_norm(hp, final_norm_g).astype(x_prompt.dtype)
    y_sample = rms_norm(hs, final_norm_g).astype(x_sample.dtype)
    return (y_prompt, y_sample, jnp.stack(kp_l), jnp.stack(vp_l), jnp.stack(rp_l),
            jnp.stack(ks_l), jnp.stack(vs_l), jnp.stack(rs_l))
```

```python
import functools
import math

import jax
import jax.numpy as jnp
from jax import lax
from jax.experimental import pallas as pl
from jax.experimental.pallas import tpu as pltpu

F32 = jnp.float32
BF16 = jnp.bfloat16
I32 = jnp.int32

PAST_LEN = 16384
DA_HEADS = 8
DA_VD = 128
DA_HD = 64
DA_ROT = 16
ROPE_THETA = 500000.0
RET_HEADS = 8
RET_VD = 128
RET_KD = 64
RET_THETA = 10000.0
N_GROUPS = 4
EXPERTS_PER_GROUP = 8
N_EXPERTS = N_GROUPS * EXPERTS_PER_GROUP
EPS = 1e-6
NEG = -1e30
LANES = 128

VMEM_LIMIT = 56 * 1024 * 1024

COL_Q, COL_K, COL_V, COL_RQK, COL_RV, COL_GATES = 0, 1024, 2048, 3072, 4096, 5120

MOE_ROWS = 1024
MOE_SUB = 256
MOE_FSPLIT = 4


def _params(sem, vmem=VMEM_LIMIT):
    return pltpu.CompilerParams(dimension_semantics=sem, vmem_limit_bytes=vmem)


def _rmsnorm_kernel(x_ref, g_ref, o_ref):
    x = x_ref[...]
    ms = jnp.mean(x * x, axis=-1, keepdims=True)
    o_ref[...] = (x * lax.rsqrt(ms + EPS) * g_ref[...]).astype(o_ref.dtype)


def _rmsnorm(x, g, tm):
    m, d = x.shape
    return pl.pallas_call(
        _rmsnorm_kernel,
        out_shape=jax.ShapeDtypeStruct((m, d), BF16),
        grid=(m // tm,),
        in_specs=[pl.BlockSpec((tm, d), lambda i: (i, 0)),
                  pl.BlockSpec((1, d), lambda i: (0, 0))],
        out_specs=pl.BlockSpec((tm, d), lambda i: (i, 0)),
        compiler_params=_params(("arbitrary",)),
        name="rmsnorm",
    )(x, g.reshape(1, d))


def _proj_kernel(*refs, shift, tn):
    if shift:
        xn_ref, w_ref, c_ref, s1_ref, s2_ref, o_ref, wbf_ref = refs
    else:
        xn_ref, w_ref, o_ref, wbf_ref = refs

    @pl.when(pl.program_id(1) == 0)
    def _():
        wbf_ref[...] = w_ref[...].astype(BF16)

    acc = jnp.dot(xn_ref[...], wbf_ref[...], preferred_element_type=F32)
    if shift:
        rep = tn // LANES
        c = jnp.tile(c_ref[0], (1, rep))
        s1 = jnp.tile(s1_ref[0], (1, rep))
        s2 = jnp.tile(s2_ref[0], (1, rep))
        acc = (acc * c + pltpu.roll(acc, tn - shift, 1) * s1
               + pltpu.roll(acc, shift, 1) * s2)
    o_ref[...] = acc.astype(o_ref.dtype)


def _proj(xn, w_in, col0, ncols, tm, out_dtype, rope=None):
    m, d = xn.shape
    tn = 512
    nj = ncols // tn
    j0 = col0 // tn
    in_specs = [pl.BlockSpec((tm, d), lambda j, i: (i, 0)),
                pl.BlockSpec((d, tn), lambda j, i: (0, j + j0))]
    args = [xn, w_in]
    shift = 0
    if rope is not None:
        tabs, shift = rope
        nt, npos = tabs.shape[1], tabs.shape[2]
        npb = npos // tm
        per_tile = nt > 1
        for t in range(3):
            in_specs.append(pl.BlockSpec(
                (1, tm, LANES),
                (lambda j, i: (j, i % npb, 0)) if per_tile else (lambda j, i: (0, i % npb, 0))))
            args.append(tabs[t])
    return pl.pallas_call(
        functools.partial(_proj_kernel, shift=shift, tn=tn),
        out_shape=jax.ShapeDtypeStruct((m, ncols), out_dtype),
        grid=(nj, m // tm),
        in_specs=in_specs,
        out_specs=pl.BlockSpec((tm, tn), lambda j, i: (i, j)),
        scratch_shapes=[pltpu.VMEM((d, tn), BF16)],
        compiler_params=_params(("arbitrary", "arbitrary")),
        name="in_proj",
    )(*args)


def _rope_tables(pos, rot_dim, period, theta, scale):
    half = rot_dim // 2
    inv = 1.0 / jnp.power(jnp.asarray(theta, F32), jnp.arange(half, dtype=F32) * (2.0 / rot_dim))
    ang = pos.astype(F32)[:, None] * inv[None, :]
    cos, sin = jnp.cos(ang), jnp.sin(ang)
    npos = pos.shape[0]
    pad = period - rot_dim
    c = jnp.concatenate([cos, cos, jnp.ones((npos, pad), F32)], axis=1)
    s1 = jnp.concatenate([-sin, jnp.zeros((npos, half + pad), F32)], axis=1)
    s2 = jnp.concatenate([jnp.zeros((npos, half), F32), sin, jnp.zeros((npos, pad), F32)], axis=1)
    tabs = jnp.stack([c, s1, s2]) * scale
    return jnp.tile(tabs, (1, 1, LANES // period))


def _attn_prompt_kernel(lam_ref, q_ref, k_ref, v_ref, g_ref, o_ref, m_ref, l_ref, acc_ref,
                        *, tq, out_scale):
    qi = pl.program_id(2)
    q = q_ref[...]
    lane = lax.broadcasted_iota(I32, (tq, LANES), 1)
    zero = jnp.zeros_like(q)
    qq = jnp.concatenate([jnp.where(lane < DA_HD, q, zero),
                          jnp.where(lane >= DA_HD, q, zero)], axis=0)
    m_ref[...] = jnp.full(m_ref.shape, NEG, F32)
    l_ref[...] = jnp.zeros(l_ref.shape, F32)
    acc_ref[...] = jnp.zeros(acc_ref.shape, F32)

    def step(j, masked):
        off = pl.multiple_of(j * tq, tq)
        kb = k_ref[pl.ds(off, tq), :].astype(BF16)
        vb = v_ref[pl.ds(off, tq), :].astype(BF16)
        s = lax.dot_general(qq, kb, (((1,), (1,)), ((), ())), preferred_element_type=F32)
        if masked:
            row = lax.broadcasted_iota(I32, (2 * tq, tq), 0) & (tq - 1)
            col = lax.broadcasted_iota(I32, (2 * tq, tq), 1)
            s = jnp.where(col <= row, s, NEG)
        m_old = m_ref[...]
        m_new = jnp.maximum(m_old, jnp.max(s, axis=-1, keepdims=True))
        alpha = jnp.exp(m_old - m_new)
        p = jnp.exp(s - m_new)
        l_ref[...] = alpha * l_ref[...] + jnp.sum(p, axis=-1, keepdims=True)
        acc_ref[...] = alpha * acc_ref[...] + jnp.dot(p.astype(BF16), vb,
                                                      preferred_element_type=F32)
        m_ref[...] = m_new

    def body(j, carry):
        step(j, False)
        return carry

    lax.fori_loop(0, qi, body, 0)
    step(qi, True)

    o = acc_ref[...] / l_ref[...]
    out = o[:tq] - lam_ref[0] * o[tq:]
    ms = jnp.mean(out * out, axis=-1, keepdims=True)
    out = out * lax.rsqrt(ms + EPS) * g_ref[...] * out_scale
    o_ref[...] = out.astype(o_ref.dtype)


def _attn_prompt(lam, q, k, v, g, b, s, out_scale, tq):
    nq = s // tq
    return pl.pallas_call(
        functools.partial(_attn_prompt_kernel, tq=tq, out_scale=out_scale),
        out_shape=jax.ShapeDtypeStruct((b * s, DA_HEADS * DA_VD), BF16),
        grid=(b, DA_HEADS, nq),
        in_specs=[pl.BlockSpec(memory_space=pltpu.SMEM),
                  pl.BlockSpec((tq, LANES), lambda bi, h, qi: (bi * nq + qi, h)),
                  pl.BlockSpec((s, LANES), lambda bi, h, qi: (bi, h)),
                  pl.BlockSpec((s, LANES), lambda bi, h, qi: (bi, h)),
                  pl.BlockSpec((1, LANES), lambda bi, h, qi: (0, h))],
        out_specs=pl.BlockSpec((tq, LANES), lambda bi, h, qi: (bi * nq + qi, h)),
        scratch_shapes=[pltpu.VMEM((2 * tq, 1), F32),
                        pltpu.VMEM((2 * tq, 1), F32),
                        pltpu.VMEM((2 * tq, LANES), F32)],
        compiler_params=_params(("arbitrary", "arbitrary", "arbitrary")),
        name="attn_prompt",
    )(lam, q, k, v, g)


def _attn_paged_kernel(pt_ref, lam_ref, q_ref, kn_ref, vn_ref, g_ref, *rest,
                       npg, t, out_scale):
    k_refs = rest[:npg]
    v_refs = rest[npg:2 * npg]
    o_ref, m_ref, l_ref, acc_ref = rest[2 * npg:]
    c = pl.program_id(1)
    nc = pl.num_programs(1)
    rows = 2 * t

    @pl.when(c == 0)
    def _():
        m_ref[...] = jnp.full(m_ref.shape, NEG, F32)
        l_ref[...] = jnp.zeros(l_ref.shape, F32)
        acc_ref[...] = jnp.zeros(acc_ref.shape, F32)

    def update(scores, values):
        s_all = jnp.concatenate(scores, axis=0)
        m_old = m_ref[...]
        m_new = jnp.maximum(m_old, jnp.max(s_all, axis=-1, keepdims=True))
        alpha = jnp.exp(m_old - m_new)
        p = jnp.exp(s_all - m_new)
        l_ref[...] = alpha * l_ref[...] + jnp.sum(p, axis=-1, keepdims=True)
        m_ref[...] = m_new
        pb = p.astype(BF16)
        for h in range(DA_HEADS):
            r0 = h * rows
            pv = None
            for lo, hi, vb in values[h]:
                d = jnp.dot(pb[r0:r0 + rows, lo:hi], vb, preferred_element_type=F32)
                pv = d if pv is None else pv + d
            acc_ref[r0:r0 + rows, :] = alpha[r0:r0 + rows] * acc_ref[r0:r0 + rows, :] + pv

    q = q_ref[0].astype(BF16)
    scores, values = [], []
    for h in range(DA_HEADS):
        qh = q[h * rows:(h + 1) * rows]
        sh, vh = [], []
        for i in range(npg):
            kb = k_refs[i][0, :, h * LANES:(h + 1) * LANES].astype(BF16)
            vb = v_refs[i][0, :, h * LANES:(h + 1) * LANES].astype(BF16)
            sh.append(lax.dot_general(qh, kb, (((1,), (1,)), ((), ())),
                                      preferred_element_type=F32))
            vh.append((i * LANES, (i + 1) * LANES, vb))
        scores.append(jnp.concatenate(sh, axis=1))
        values.append(vh)
    update(scores, values)

    @pl.when(c == nc - 1)
    def _():
        tp = kn_ref.shape[1]
        scores, values = [], []
        row = lax.broadcasted_iota(I32, (rows, tp), 0) & (t - 1)
        col = lax.broadcasted_iota(I32, (rows, tp), 1)
        for h in range(DA_HEADS):
            qh = q[h * rows:(h + 1) * rows]
            kb = kn_ref[0, :, h * LANES:(h + 1) * LANES].astype(BF16)
            vb = vn_ref[0, :, h * LANES:(h + 1) * LANES].astype(BF16)
            s = lax.dot_general(qh, kb, (((1,), (1,)), ((), ())), preferred_element_type=F32)
            scores.append(jnp.where(col <= row, s, NEG))
            values.append([(0, tp, vb)])
        update(scores, values)
        o = acc_ref[...] / l_ref[...]
        lam = lam_ref[0]
        for h in range(DA_HEADS):
            r0 = h * rows
            out = o[r0:r0 + t] - lam * o[r0 + t:r0 + rows]
            ms = jnp.mean(out * out, axis=-1, keepdims=True)
            gh = g_ref[:, h * LANES:(h + 1) * LANES]
            o_ref[0, :, h * LANES:(h + 1) * LANES] = (
                out * lax.rsqrt(ms + EPS) * gh * out_scale).astype(o_ref.dtype)


def _attn_paged(page_table, lam, qm, k_new, v_new, g, cache_k, cache_v, page0, t, out_scale, npg):
    db, n_pages = page_table.shape
    page = cache_k.shape[1]
    w = cache_k.shape[2]
    tp = k_new.shape[1]
    nc = n_pages // npg
    pt_flat = page_table.reshape(-1) + page0

    def page_spec(i):
        return pl.BlockSpec((1, page, w),
                            lambda b, c, pt: (pt[b * n_pages + c * npg + i], 0, 0))

    in_specs = [pl.BlockSpec(memory_space=pltpu.SMEM),
                pl.BlockSpec((1, qm.shape[1], LANES), lambda b, c, pt: (b, 0, 0)),
                pl.BlockSpec((1, tp, w), lambda b, c, pt: (b, 0, 0)),
                pl.BlockSpec((1, tp, w), lambda b, c, pt: (b, 0, 0)),
                pl.BlockSpec((1, w), lambda b, c, pt: (0, 0))]
    in_specs += [page_spec(i) for i in range(npg)] * 2
    rows = DA_HEADS * 2 * t
    return pl.pallas_call(
        functools.partial(_attn_paged_kernel, npg=npg, t=t, out_scale=out_scale),
        out_shape=jax.ShapeDtypeStruct((db, t, w), BF16),
        grid_spec=pltpu.PrefetchScalarGridSpec(
            num_scalar_prefetch=1,
            grid=(db, nc),
            in_specs=in_specs,
            out_specs=pl.BlockSpec((1, t, w), lambda b, c, pt: (b, 0, 0)),
            scratch_shapes=[pltpu.VMEM((rows, 1), F32),
                            pltpu.VMEM((rows, 1), F32),
                            pltpu.VMEM((rows, LANES), F32)]),
        compiler_params=_params(("arbitrary", "arbitrary")),
        name="attn_paged",
    )(pt_flat, lam, qm, k_new, v_new, g, *([cache_k] * npg), *([cache_v] * npg))


def _ret_kernel(rq_ref, rk_ref, rv_ref, rg_ref, s0_ref, dmat_ref, dq_ref, dk_ref, gc_ref, g_ref,
                o_ref, sout_ref, st_ref):
    c = pl.program_id(1)

    @pl.when(c == 0)
    def _():
        st_ref[...] = s0_ref[0]

    for h in range(RET_HEADS):
        kq = slice(h * RET_KD, (h + 1) * RET_KD)
        vs = slice(h * RET_VD, (h + 1) * RET_VD)
        qb = rq_ref[0, :, kq].astype(BF16)
        k = rk_ref[0, :, kq]
        vb = rv_ref[0, :, vs].astype(BF16)
        st = st_ref[h]
        att = lax.dot_general(qb, k.astype(BF16), (((1,), (1,)), ((), ())),
                              preferred_element_type=F32) * dmat_ref[h]
        inner = jnp.dot(att.astype(BF16), vb, preferred_element_type=F32)
        cross = jnp.dot(qb, st.astype(BF16), preferred_element_type=F32) * dq_ref[h]
        r = inner + cross
        kd = (k * dk_ref[h]).astype(BF16)
        st_ref[h] = gc_ref[h] * st + lax.dot_general(
            kd, vb, (((0,), (0,)), ((), ())), preferred_element_type=F32)
        mu = jnp.mean(r, axis=-1, keepdims=True)
        xc = r - mu
        var = jnp.mean(xc * xc, axis=-1, keepdims=True)
        y = xc * lax.rsqrt(var + EPS) * g_ref[:, vs]
        rg = rg_ref[0, :, vs]
        o_ref[0, :, vs] = (rg * jax.nn.sigmoid(rg) * y).astype(o_ref.dtype)

    @pl.when(c == pl.num_programs(1) - 1)
    def _():
        sout_ref[0] = st_ref[...]


def _ret_tables(log_g, chunk, valid):
    idx = jnp.arange(chunk, dtype=F32)
    diff = idx[:, None] - idx[None, :]
    dmat = jnp.where(diff[None] >= 0,
                     jnp.exp(jnp.maximum(diff, 0.0)[None] * log_g[:, None, None]), 0.0)
    dq = jnp.exp((idx + 1.0)[None, :] * log_g[:, None])
    dk = jnp.where(idx[None, :] < valid,
                   jnp.exp((valid - 1.0 - idx)[None, :] * log_g[:, None]), 0.0)
    gc = jnp.exp(valid * log_g)
    h = log_g.shape[0]
    return (dmat.astype(F32),
            jnp.broadcast_to(dq[:, :, None], (h, chunk, RET_VD)).astype(F32),
            jnp.broadcast_to(dk[:, :, None], (h, chunk, RET_KD)).astype(F32),
            jnp.broadcast_to(gc[:, None, None], (h, 1, RET_VD)).astype(F32))


def _retention(rqk, rv, gates, state0, g, log_g, chunk, valid):
    b, s, _ = rqk.shape
    nc = s // chunk
    dmat, dq, dk, gc = _ret_tables(log_g, chunk, valid)
    qw = RET_HEADS * RET_KD
    vw = RET_HEADS * RET_VD
    full = lambda shp: pl.BlockSpec(shp, lambda bi, c: (0,) * len(shp))
    return pl.pallas_call(
        _ret_kernel,
        out_shape=(jax.ShapeDtypeStruct((b, s, vw), BF16),
                   jax.ShapeDtypeStruct((b, RET_HEADS, RET_KD, RET_VD), F32)),
        grid=(b, nc),
        in_specs=[pl.BlockSpec((1, chunk, qw), lambda bi, c: (bi, c, 0)),
                  pl.BlockSpec((1, chunk, qw), lambda bi, c: (bi, c, 1)),
                  pl.BlockSpec((1, chunk, vw), lambda bi, c: (bi, c, 0)),
                  pl.BlockSpec((1, chunk, vw), lambda bi, c: (bi, c, 0)),
                  pl.BlockSpec((1, RET_HEADS, RET_KD, RET_VD), lambda bi, c: (bi, 0, 0, 0)),
                  full(dmat.shape), full(dq.shape), full(dk.shape), full(gc.shape),
                  full((1, vw))],
        out_specs=(pl.BlockSpec((1, chunk, vw), lambda bi, c: (bi, c, 0)),
                   pl.BlockSpec((1, RET_HEADS, RET_KD, RET_VD), lambda bi, c: (bi, 0, 0, 0))),
        scratch_shapes=[pltpu.VMEM((RET_HEADS, RET_KD, RET_VD), F32)],
        compiler_params=_params(("arbitrary", "arbitrary")),
        name="retention",
    )(rqk, rqk, rv, gates, state0, dmat, dq, dk, gc, g.reshape(1, vw))


def _finish_kernel(a_ref, r_ref, ga_ref, gr_ref, h_ref, wa_ref, wr_ref, wo_ref, gffn_ref,
                   wrt_ref, brt_ref, carry0_ref,
                   h1_ref, hn_ref, route_ref, cnt_ref, carry_ref, *, tm):
    i = pl.program_id(0)

    @pl.when(i == 0)
    def _():
        carry_ref[...] = carry0_ref[...]

    am = jnp.dot(a_ref[...], wa_ref[...], preferred_element_type=F32)
    rm = jnp.dot(r_ref[...], wr_ref[...], preferred_element_type=F32)
    mix = jax.nn.sigmoid(ga_ref[...]) * am + jax.nn.sigmoid(gr_ref[...]) * rm
    h1 = h_ref[...] + jnp.dot(mix.astype(BF16), wo_ref[...], preferred_element_type=F32)
    h1_ref[...] = h1
    ms = jnp.mean(h1 * h1, axis=-1, keepdims=True)
    hn = h1 * lax.rsqrt(ms + EPS) * gffn_ref[...]
    hn_ref[...] = hn
    logits = jnp.dot(hn.astype(BF16), wrt_ref[...], preferred_element_type=F32) + brt_ref[...]
    lane = lax.broadcasted_iota(I32, (tm, LANES), 1)
    is_g = (lane >= N_EXPERTS) & (lane < N_EXPERTS + N_GROUPS)
    glog = jnp.where(is_g, logits, -jnp.inf)
    gmax = jnp.max(glog, axis=-1, keepdims=True)
    gidx = jnp.min(jnp.where(glog == gmax, lane - N_EXPERTS, LANES), axis=-1, keepdims=True)
    gden = jnp.sum(jnp.where(is_g, jnp.exp(glog - gmax), 0.0), axis=-1, keepdims=True)
    gp = 1.0 / gden
    in_g = (lane < N_EXPERTS) & ((lane // EXPERTS_PER_GROUP) == gidx)
    e1 = jnp.where(in_g, logits, -jnp.inf)
    v1 = jnp.max(e1, axis=-1, keepdims=True)
    i1 = jnp.min(jnp.where(e1 == v1, lane, LANES), axis=-1, keepdims=True)
    e2 = jnp.where(lane == i1, -jnp.inf, e1)
    v2 = jnp.max(e2, axis=-1, keepdims=True)
    i2 = jnp.min(jnp.where(e2 == v2, lane, LANES), axis=-1, keepdims=True)
    tt = jnp.exp(v2 - v1)
    w1 = gp / (1.0 + tt)
    w2 = gp * tt / (1.0 + tt)
    oh = jnp.where((lane == i1) | (lane == i2), 1.0, 0.0)
    rr = lax.broadcasted_iota(I32, (tm, tm), 0)
    cc = lax.broadcasted_iota(I32, (tm, tm), 1)
    tri = jnp.where(cc < rr, 1.0, 0.0).astype(BF16)
    before = jnp.dot(tri, oh.astype(BF16), preferred_element_type=F32) + carry_ref[...]
    rank1 = jnp.sum(jnp.where(lane == i1, before, 0.0), axis=-1, keepdims=True)
    rank2 = jnp.sum(jnp.where(lane == i2, before, 0.0), axis=-1, keepdims=True)
    carry = carry_ref[...] + jnp.sum(oh, axis=0, keepdims=True)
    carry_ref[...] = carry
    cnt_ref[...] = carry
    cols = [i1.astype(F32), i2.astype(F32), w1, w2, rank1, rank2]
    route = jnp.zeros((tm, LANES), F32)
    for ci, val in enumerate(cols):
        route = jnp.where(lane == ci, val, route)
    route_ref[...] = route


def _finish(a, r, gagr, h, wa, wr, wo, gffn, wrt, brt, carry0, tm):
    m, d = h.shape
    aw = a.shape[1]
    assert gagr.shape[1] == 2 * d
    const = lambda shp: pl.BlockSpec(shp, lambda i: (0,) * len(shp))
    return pl.pallas_call(
        functools.partial(_finish_kernel, tm=tm),
        out_shape=(jax.ShapeDtypeStruct((m, d), F32),
                   jax.ShapeDtypeStruct((m, d), F32),
                   jax.ShapeDtypeStruct((m, LANES), F32),
                   jax.ShapeDtypeStruct((1, LANES), F32)),
        grid=(m // tm,),
        in_specs=[pl.BlockSpec((tm, aw), lambda i: (i, 0)),
                  pl.BlockSpec((tm, aw), lambda i: (i, 0)),
                  pl.BlockSpec((tm, d), lambda i: (i, 0)),
                  pl.BlockSpec((tm, d), lambda i: (i, 1)),
                  pl.BlockSpec((tm, d), lambda i: (i, 0)),
                  const(wa.shape), const(wr.shape), const(wo.shape), const((1, d)),
                  const(wrt.shape), const((1, LANES)), const((1, LANES))],
        out_specs=(pl.BlockSpec((tm, d), lambda i: (i, 0)),
                   pl.BlockSpec((tm, d), lambda i: (i, 0)),
                   pl.BlockSpec((tm, LANES), lambda i: (i, 0)),
                   pl.BlockSpec((1, LANES), lambda i: (0, 0))),
        scratch_shapes=[pltpu.VMEM((1, LANES), F32)],
        compiler_params=_params(("arbitrary",)),
        name="finish",
    )(a, r, gagr, gagr, h, wa, wr, wo, gffn.reshape(1, d), wrt, brt, carry0)


def _moe_kernel(item_e, item_start, item_n, row_src,
                hn_hbm, wg_ref, wu_ref, wd_ref, yk_hbm,
                xg, yacc, wgb, wub, wdb, gsem, ssem, *, sub):
    i = pl.program_id(0)
    f = pl.program_id(1)
    nf = pl.num_programs(1)
    n = item_n[i]
    start = item_start[i]

    @pl.when((i == 0) & (f == 0))
    def _():
        xg[...] = jnp.zeros(xg.shape, xg.dtype)

    def gather_copy(r, tok):
        return pltpu.make_async_copy(hn_hbm.at[pl.ds(tok, 1)], xg.at[pl.ds(r, 1)], gsem)

    def scatter_copy(r, slot, tok):
        return pltpu.make_async_copy(yacc.at[pl.ds(r, 1)], yk_hbm.at[slot, pl.ds(tok, 1)], ssem)

    @pl.when((f == 0) & (n > 0))
    def _():
        def issue(r, c):
            gather_copy(r, row_src[start + r] >> 1).start()
            return c
        lax.fori_loop(0, n, issue, 0)

        def wait(r, c):
            gather_copy(0, 0).wait()
            return c
        lax.fori_loop(0, n, wait, 0)

    @pl.when(n > 0)
    def _():
        wgb[...] = wg_ref[0].astype(BF16)
        wub[...] = wu_ref[0].astype(BF16)
        wdb[...] = wd_ref[0].astype(BF16)

        def sub_block(s, c):
            off = pl.multiple_of(s * sub, sub)
            x = xg[pl.ds(off, sub), :].astype(BF16)
            hg = jnp.dot(x, wgb[...], preferred_element_type=F32)
            hu = jnp.dot(x, wub[...], preferred_element_type=F32)
            hm = (hg * jax.nn.sigmoid(hg) * hu).astype(BF16)
            part = jnp.dot(hm, wdb[...], preferred_element_type=F32)

            @pl.when(f == 0)
            def _():
                yacc[pl.ds(off, sub), :] = part

            @pl.when(f > 0)
            def _():
                yacc[pl.ds(off, sub), :] += part
            return c
        lax.fori_loop(0, (n + sub - 1) // sub, sub_block, 0)

    @pl.when((f == nf - 1) & (n > 0))
    def _():
        def issue(r, c):
            src = row_src[start + r]
            scatter_copy(r, src & 1, src >> 1).start()
            return c
        lax.fori_loop(0, n, issue, 0)

        def wait(r, c):
            scatter_copy(0, 0, 0).wait()
            return c
        lax.fori_loop(0, n, wait, 0)


def _moe_plan(route, counts, rows_per_item, max_items):
    ntok = route.shape[0]
    eid = route[:, 0:2].astype(I32)
    rank = route[:, 4:6].astype(I32)
    counts = counts.astype(I32)
    ends = jnp.cumsum(counts)
    starts = ends - counts
    dest = (starts[eid] + rank).reshape(-1)
    src = jnp.arange(2 * ntok, dtype=I32)
    row_src = jnp.zeros((2 * ntok,), I32).at[dest].set(src)
    nit = (counts + rows_per_item - 1) // rows_per_item
    it_end = jnp.cumsum(nit)
    it_first = it_end - nit
    total = it_end[-1]
    t = jnp.arange(max_items, dtype=I32)
    e = jnp.minimum(jnp.searchsorted(it_end, t, side='right'), N_EXPERTS - 1).astype(I32)
    j = t - it_first[e]
    n = jnp.clip(counts[e] - j * rows_per_item, 0, rows_per_item)
    n = jnp.where(t < total, n, 0).astype(I32)
    start = (starts[e] + j * rows_per_item).astype(I32)
    start = jnp.where(n > 0, start, 0)
    last_e = e[jnp.maximum(total - 1, 0)]
    e = jnp.where(t < total, e, last_e)
    return e, start, n, row_src


def _moe(hn, route, counts, wg, wu, wd):
    ntok, d = hn.shape
    ne, _, ff = wg.shape
    rows, sub, nf = MOE_ROWS, MOE_SUB, MOE_FSPLIT
    tf = ff // nf
    max_items = ne + (2 * ntok) // rows
    item_e, item_start, item_n, row_src = _moe_plan(route, counts, rows, max_items)

    def f_eff(i, f, item_n):
        return jnp.where(item_n[i] > 0, f, nf - 1)

    return pl.pallas_call(
        functools.partial(_moe_kernel, sub=sub),
        out_shape=jax.ShapeDtypeStruct((2, ntok, d), F32),
        grid_spec=pltpu.PrefetchScalarGridSpec(
            num_scalar_prefetch=4,
            grid=(max_items, nf),
            in_specs=[pl.BlockSpec(memory_space=pl.ANY),
                      pl.BlockSpec((1, d, tf), lambda i, f, ie, ist, inn, rs: (ie[i], 0, f_eff(i, f, inn))),
                      pl.BlockSpec((1, d, tf), lambda i, f, ie, ist, inn, rs: (ie[i], 0, f_eff(i, f, inn))),
                      pl.BlockSpec((1, tf, d), lambda i, f, ie, ist, inn, rs: (ie[i], f_eff(i, f, inn), 0))],
            out_specs=pl.BlockSpec(memory_space=pl.ANY),
            scratch_shapes=[pltpu.VMEM((rows, d), F32),
                            pltpu.VMEM((rows, d), F32),
                            pltpu.VMEM((d, tf), BF16),
                            pltpu.VMEM((d, tf), BF16),
                            pltpu.VMEM((tf, d), BF16),
                            pltpu.SemaphoreType.DMA,
                            pltpu.SemaphoreType.DMA]),
        compiler_params=_params(("arbitrary", "arbitrary")),
        name="moe_experts",
    )(item_e, item_start, item_n, row_src, hn, wg, wu, wd)


def _ple_kernel(h1_ref, y0_ref, y1_ref, route_ref, pe_ref, gple_ref, wg_ref, wp_ref, gfin_ref,
                o_ref):
    route = route_ref[...]
    y = route[:, 2:3] * y0_ref[0] + route[:, 3:4] * y1_ref[0]
    h2 = h1_ref[...] + y
    ms = jnp.mean(h2 * h2, axis=-1, keepdims=True)
    hn = (h2 * lax.rsqrt(ms + EPS) * gple_ref[...]).astype(BF16)
    gate = jax.nn.sigmoid(jnp.dot(hn, wg_ref[...], preferred_element_type=F32))
    pp = jnp.dot(pe_ref[...].astype(BF16), wp_ref[...], preferred_element_type=F32)
    h3 = h2 + gate * pp
    ms = jnp.mean(h3 * h3, axis=-1, keepdims=True)
    o_ref[...] = h3 * lax.rsqrt(ms + EPS) * gfin_ref[...]


def _ple(h1, yk, row0, route, pe, gple, wg, wp, gfin, tm):
    m, d = h1.shape
    blk0 = row0 // tm
    const = lambda shp: pl.BlockSpec(shp, lambda i: (0,) * len(shp))
    return pl.pallas_call(
        _ple_kernel,
        out_shape=jax.ShapeDtypeStruct((m, d), F32),
        grid=(m // tm,),
        in_specs=[pl.BlockSpec((tm, d), lambda i: (i, 0)),
                  pl.BlockSpec((1, tm, d), lambda i: (0, i + blk0, 0)),
                  pl.BlockSpec((1, tm, d), lambda i: (1, i + blk0, 0)),
                  pl.BlockSpec((tm, LANES), lambda i: (i, 0)),
                  pl.BlockSpec((tm, pe.shape[1]), lambda i: (i, 0)),
                  const((1, d)), const(wg.shape), const(wp.shape), const((1, d))],
        out_specs=pl.BlockSpec((tm, d), lambda i: (i, 0)),
        compiler_params=_params(("arbitrary",)),
        name="ple_final",
    )(h1, yk, yk, route, pe, gple.reshape(1, d), wg, wp, gfin.reshape(1, d))


def _project_group(x, g, w_in, pos, tm):
    xn = _rmsnorm(x, g, tm)
    da_q = _rope_tables(pos, DA_ROT, DA_HD, ROPE_THETA, DA_HD ** -0.5)[:, None]
    da_k = _rope_tables(pos, DA_ROT, DA_HD, ROPE_THETA, 1.0)[:, None]
    rq_t = _rope_tables(pos, RET_KD, RET_KD, RET_THETA, 1.0)
    rk_t = _rope_tables(pos, RET_KD, RET_KD, RET_THETA, RET_KD ** -0.5)
    ret_t = jnp.stack([rq_t, rk_t], axis=1)
    q = _proj(xn, w_in, COL_Q, 1024, tm, BF16, rope=(da_q, DA_ROT // 2))
    k = _proj(xn, w_in, COL_K, 1024, tm, F32, rope=(da_k, DA_ROT // 2))
    v = _proj(xn, w_in, COL_V, 1024, tm, F32)
    rqk = _proj(xn, w_in, COL_RQK, 1024, tm, F32, rope=(ret_t, RET_KD // 2))
    rv = _proj(xn, w_in, COL_RV, 1024, tm, F32)
    rg = _proj(xn, w_in, COL_GATES, 1024, tm, F32)
    gagr = _proj(xn, w_in, COL_GATES + 1024, 4096, tm, F32)
    return q, k, v, rqk, rv, rg, gagr


def kernel(x_prompt, x_sample, cache_k, cache_v, state_ret, page_table, p_prompt, p_sample, norm_mix_g, w_in, lam_q1, lam_k1, lam_q2, lam_k2, da_norm_g, ret_norm_g, w_br_attn, w_br_ret, w_out, norm_ffn_g, w_router_group, b_router_group, w_router_expert, b_router_expert, w_exp_gate, w_exp_up, w_exp_down, norm_ple_g, w_ple_gate, w_ple_proj, final_norm_g):
    bp, sp, d = x_prompt.shape
    bs, ts, _ = x_sample.shape
    depth = w_in.shape[0]
    assert depth == 1, "the final norm is fused into the last stage of a single layer"
    n_pool, page = cache_k.shape[1], cache_k.shape[2]
    mp, msm = bp * sp, bs * ts
    log_g = jnp.log1p(-jnp.exp2(-5.0 - jnp.arange(RET_HEADS, dtype=F32)))
    pos_p = jnp.arange(sp)
    pos_s = jnp.tile(PAST_LEN + jnp.arange(ts), bs)
    cache_k2 = cache_k.reshape(depth * n_pool, page, DA_HEADS * 2 * DA_HD)
    cache_v2 = cache_v.reshape(depth * n_pool, page, DA_HEADS * DA_VD)
    tm_p = next(t for t in (1024, 512, 256, 128) if sp % t == 0)
    tf_p = 256

    hp = x_prompt.reshape(mp, d)
    hs = x_sample.reshape(msm, d)
    kp_l, vp_l, rp_l, ks_l, vs_l, rs_l = [], [], [], [], [], []
    for li in range(depth):
        lam_init = 0.8 - 0.6 * math.exp(-0.3 * li)
        lam = (jnp.exp(jnp.sum(lam_q1[li] * lam_k1[li]))
               - jnp.exp(jnp.sum(lam_q2[li] * lam_k2[li])) + lam_init).reshape(1).astype(F32)
        out_scale = 1.0 - lam_init
        wa = w_br_attn[li].astype(BF16)
        wr = w_br_ret[li].astype(BF16)
        wo = w_out[li].astype(BF16)
        wpg = w_ple_gate[li].astype(BF16)
        wpp = w_ple_proj[li].astype(BF16)
        wrt = jnp.zeros((d, LANES), F32)
        wrt = wrt.at[:, :N_EXPERTS].set(w_router_expert[li])
        wrt = wrt.at[:, N_EXPERTS:N_EXPERTS + N_GROUPS].set(w_router_group[li]).astype(BF16)
        brt = jnp.zeros((1, LANES), F32)
        brt = brt.at[0, :N_EXPERTS].set(b_router_expert[li])
        brt = brt.at[0, N_EXPERTS:N_EXPERTS + N_GROUPS].set(b_router_group[li])
        da_g = da_norm_g[li].reshape(1, -1)

        q, k, v, rqk, rv, rg, gagr = _project_group(hp, norm_mix_g[li], w_in[li], pos_p, tm_p)
        a_p = _attn_prompt(lam, q, k, v, da_g, bp, sp, out_scale, tq=256 if sp % 256 == 0 else 128)
        chunk = 256 if sp % 256 == 0 else 128
        r_p, st_p = _retention(rqk.reshape(bp, sp, -1), rv.reshape(bp, sp, -1),
                               rg.reshape(bp, sp, -1),
                               jnp.zeros((bp, RET_HEADS, RET_KD, RET_VD), F32),
                               ret_norm_g[li], log_g, chunk, chunk)
        kp_l.append(k.reshape(bp, sp, DA_HEADS, 2, DA_HD))
        vp_l.append(v.reshape(bp, sp, DA_HEADS, DA_VD))
        rp_l.append(st_p)
        h1_p, hn_p, route_p, cnt_p = _finish(
            a_p, r_p.reshape(mp, -1), gagr, hp, wa, wr, wo, norm_ffn_g[li], wrt, brt,
            jnp.zeros((1, LANES), F32), tf_p)

        q, k, v, rqk, rv, rg, gagr = _project_group(hs, norm_mix_g[li], w_in[li], pos_s, msm)
        tpad = 8
        pad_t = lambda z: jnp.pad(z.reshape(bs, ts, -1), ((0, 0), (0, tpad - ts), (0, 0)))
        q4 = q.astype(F32).reshape(bs, ts, DA_HEADS, 2, DA_HD)
        eye = jnp.eye(2, dtype=F32)
        qm = jnp.einsum('bthmd,mn->bhmtnd', q4, eye).reshape(bs, DA_HEADS * 2 * ts, 2 * DA_HD)
        a_s = _attn_paged(page_table, lam, qm, pad_t(k), pad_t(v), da_g, cache_k2, cache_v2,
                          li * n_pool, ts, out_scale, npg=4)
        r_s, st_s = _retention(pad_t(rqk), pad_t(rv), pad_t(rg), state_ret[li],
                               ret_norm_g[li], log_g, tpad, ts)
        ks_l.append(k.reshape(bs, ts, DA_HEADS, 2, DA_HD))
        vs_l.append(v.reshape(bs, ts, DA_HEADS, DA_VD))
        rs_l.append(st_s)
        h1_s, hn_s, route_s, cnt = _finish(
            a_s.reshape(msm, -1), r_s[:, :ts].reshape(msm, -1), gagr, hs, wa, wr, wo,
            norm_ffn_g[li], wrt, brt, cnt_p, msm)

        hn_all = jnp.concatenate([hn_p, hn_s], axis=0)
        route_all = jnp.concatenate([route_p, route_s], axis=0)
        yk = _moe(hn_all, route_all, cnt[0, :N_EXPERTS], w_exp_gate[li], w_exp_up[li],
                  w_exp_down[li])

        gfin = final_norm_g
        hp = _ple(h1_p, yk, 0, route_p, p_prompt[li].reshape(mp, -1), norm_ple_g[li], wpg, wpp,
                  gfin, tf_p)
        hs = _ple(h1_s, yk, mp, route_s, p_sample[li].reshape(msm, -1), norm_ple_g[li], wpg, wpp,
                  gfin, msm)
    y_prompt = hp.reshape(bp, sp, d)
    y_sample = hs.reshape(bs, ts, d)
    return (y_prompt, y_sample, jnp.stack(kp_l), jnp.stack(vp_l), jnp.stack(rp_l),
            jnp.stack(ks_l), jnp.stack(vs_l), jnp.stack(rs_l))
```

```python
import functools
import math

import jax
import jax.numpy as jnp
from jax import lax
from jax.experimental import pallas as pl
from jax.experimental.pallas import tpu as pltpu

F32 = jnp.float32
BF16 = jnp.bfloat16
I32 = jnp.int32

PAST_LEN = 16384
DA_HEADS = 8
DA_VD = 128
DA_HD = 64
DA_ROT = 16
ROPE_THETA = 500000.0
RET_HEADS = 8
RET_VD = 128
RET_KD = 64
RET_THETA = 10000.0
N_GROUPS = 4
EXPERTS_PER_GROUP = 8
N_EXPERTS = N_GROUPS * EXPERTS_PER_GROUP
EPS = 1e-6
NEG = -1e30
LANES = 128

VMEM_LIMIT = 56 * 1024 * 1024

COL_Q, COL_K, COL_V, COL_RQK, COL_RV, COL_GATES = 0, 1024, 2048, 3072, 4096, 5120

MOE_ROWS = 1024
MOE_SUB = 128
MOE_FSPLIT = 4


def _params(sem, vmem=VMEM_LIMIT):
    return pltpu.CompilerParams(dimension_semantics=sem, vmem_limit_bytes=vmem)


def _rmsnorm_kernel(x_ref, g_ref, o_ref):
    x = x_ref[...]
    ms = jnp.mean(x * x, axis=-1, keepdims=True)
    o_ref[...] = (x * lax.rsqrt(ms + EPS) * g_ref[...]).astype(o_ref.dtype)


def _rmsnorm(x, g, tm):
    m, d = x.shape
    return pl.pallas_call(
        _rmsnorm_kernel,
        out_shape=jax.ShapeDtypeStruct((m, d), BF16),
        grid=(m // tm,),
        in_specs=[pl.BlockSpec((tm, d), lambda i: (i, 0)),
                  pl.BlockSpec((1, d), lambda i: (0, 0))],
        out_specs=pl.BlockSpec((tm, d), lambda i: (i, 0)),
        compiler_params=_params(("arbitrary",)),
        name="rmsnorm",
    )(x, g.reshape(1, d))


def _proj_kernel(*refs, shift, tn):
    if shift:
        xn_ref, w_ref, c_ref, s1_ref, s2_ref, o_ref, wbf_ref = refs
    else:
        xn_ref, w_ref, o_ref, wbf_ref = refs

    @pl.when(pl.program_id(1) == 0)
    def _():
        wbf_ref[...] = w_ref[...].astype(BF16)

    acc = jnp.dot(xn_ref[...], wbf_ref[...], preferred_element_type=F32)
    if shift:
        rep = tn // LANES
        c = jnp.tile(c_ref[0], (1, rep))
        s1 = jnp.tile(s1_ref[0], (1, rep))
        s2 = jnp.tile(s2_ref[0], (1, rep))
        acc = (acc * c + pltpu.roll(acc, tn - shift, 1) * s1
               + pltpu.roll(acc, shift, 1) * s2)
    o_ref[...] = acc.astype(o_ref.dtype)


def _proj(xn, w_in, col0, ncols, tm, out_dtype, rope=None):
    m, d = xn.shape
    tn = 512
    nj = ncols // tn
    j0 = col0 // tn
    in_specs = [pl.BlockSpec((tm, d), lambda j, i: (i, 0)),
                pl.BlockSpec((d, tn), lambda j, i: (0, j + j0))]
    args = [xn, w_in]
    shift = 0
    if rope is not None:
        tabs, shift = rope
        nt, npos = tabs.shape[1], tabs.shape[2]
        npb = npos // tm
        per_tile = nt > 1
        for t in range(3):
            in_specs.append(pl.BlockSpec(
                (1, tm, LANES),
                (lambda j, i: (j, i % npb, 0)) if per_tile else (lambda j, i: (0, i % npb, 0))))
            args.append(tabs[t])
    return pl.pallas_call(
        functools.partial(_proj_kernel, shift=shift, tn=tn),
        out_shape=jax.ShapeDtypeStruct((m, ncols), out_dtype),
        grid=(nj, m // tm),
        in_specs=in_specs,
        out_specs=pl.BlockSpec((tm, tn), lambda j, i: (i, j)),
        scratch_shapes=[pltpu.VMEM((d, tn), BF16)],
        compiler_params=_params(("arbitrary", "arbitrary")),
        name="in_proj",
    )(*args)


def _rope_tables(pos, rot_dim, period, theta, scale):
    half = rot_dim // 2
    inv = 1.0 / jnp.power(jnp.asarray(theta, F32), jnp.arange(half, dtype=F32) * (2.0 / rot_dim))
    ang = pos.astype(F32)[:, None] * inv[None, :]
    cos, sin = jnp.cos(ang), jnp.sin(ang)
    npos = pos.shape[0]
    pad = period - rot_dim
    c = jnp.concatenate([cos, cos, jnp.ones((npos, pad), F32)], axis=1)
    s1 = jnp.concatenate([-sin, jnp.zeros((npos, half + pad), F32)], axis=1)
    s2 = jnp.concatenate([jnp.zeros((npos, half), F32), sin, jnp.zeros((npos, pad), F32)], axis=1)
    tabs = jnp.stack([c, s1, s2]) * scale
    return jnp.tile(tabs, (1, 1, LANES // period))


def _attn_prompt_kernel(lam_ref, q_ref, k_ref, v_ref, g_ref, o_ref,
                        kb_ref, vt_ref, m_ref, l_ref, acc_ref, *, tq, hp, out_scale):
    qi = pl.program_id(2)
    nblk = kb_ref.shape[1]

    @pl.when(qi == 0)
    def _():
        def stage(c, carry):
            off = pl.multiple_of(c * tq, tq)
            for h in range(hp):
                cols = slice(h * LANES, (h + 1) * LANES)
                kb_ref[h, c] = k_ref[pl.ds(off, tq), cols].astype(BF16)
                vt_ref[h, c] = v_ref[pl.ds(off, tq), cols].T.astype(BF16)
            return carry
        lax.fori_loop(0, nblk, stage, 0)

    row = lax.broadcasted_iota(I32, (LANES, tq), 0)
    qqt = []
    for h in range(hp):
        qt = q_ref[:, h * LANES:(h + 1) * LANES].astype(F32).T
        qqt.append(jnp.concatenate([jnp.where(row < DA_HD, qt, 0.0),
                                    jnp.where(row >= DA_HD, qt, 0.0)], axis=1).astype(BF16))
    m_ref[...] = jnp.full(m_ref.shape, NEG, F32)
    l_ref[...] = jnp.zeros(l_ref.shape, F32)
    acc_ref[...] = jnp.zeros(acc_ref.shape, F32)

    def step(j, masked):
        sts = [jnp.dot(kb_ref[h, j], qqt[h], preferred_element_type=F32)
               for h in range(hp)]
        for h in range(hp):
            st = sts[h]
            if masked:
                key = lax.broadcasted_iota(I32, (tq, 2 * tq), 0)
                qry = lax.broadcasted_iota(I32, (tq, 2 * tq), 1) & (tq - 1)
                st = jnp.where(key <= qry, st, NEG)
            m_old = m_ref[h]
            m_new = jnp.maximum(m_old, jnp.max(st, axis=0, keepdims=True))
            alpha = jnp.exp2(m_old - m_new)
            p = jnp.exp2(st - m_new)
            l_ref[h] = alpha * l_ref[h] + jnp.sum(p, axis=0, keepdims=True)
            acc_ref[h] = alpha * acc_ref[h] + jnp.dot(vt_ref[h, j], p.astype(BF16),
                                                      preferred_element_type=F32)
            m_ref[h] = m_new

    def body(j, carry):
        step(j, False)
        return carry

    lax.fori_loop(0, qi, body, 0)
    step(qi, True)

    for h in range(hp):
        cols = slice(h * LANES, (h + 1) * LANES)
        o = acc_ref[h] / l_ref[h]
        out = (o[:, :tq] - lam_ref[0] * o[:, tq:]).T
        ms = jnp.mean(out * out, axis=-1, keepdims=True)
        out = out * lax.rsqrt(ms + EPS) * g_ref[:, cols] * out_scale
        o_ref[:, cols] = out.astype(o_ref.dtype)


def _attn_prompt(lam, q, k, v, g, b, s, out_scale, tq, hp):
    nq = s // tq
    w = hp * LANES
    return pl.pallas_call(
        functools.partial(_attn_prompt_kernel, tq=tq, hp=hp, out_scale=out_scale),
        out_shape=jax.ShapeDtypeStruct((b * s, DA_HEADS * DA_VD), BF16),
        grid=(b, DA_HEADS // hp, nq),
        in_specs=[pl.BlockSpec(memory_space=pltpu.SMEM),
                  pl.BlockSpec((tq, w), lambda bi, h, qi: (bi * nq + qi, h)),
                  pl.BlockSpec((s, w), lambda bi, h, qi: (bi, h)),
                  pl.BlockSpec((s, w), lambda bi, h, qi: (bi, h)),
                  pl.BlockSpec((1, w), lambda bi, h, qi: (0, h))],
        out_specs=pl.BlockSpec((tq, w), lambda bi, h, qi: (bi * nq + qi, h)),
        scratch_shapes=[pltpu.VMEM((hp, nq, tq, LANES), BF16),
                        pltpu.VMEM((hp, nq, LANES, tq), BF16),
                        pltpu.VMEM((hp, 1, 2 * tq), F32),
                        pltpu.VMEM((hp, 1, 2 * tq), F32),
                        pltpu.VMEM((hp, LANES, 2 * tq), F32)],
        compiler_params=_params(("arbitrary", "arbitrary", "arbitrary")),
        name="attn_prompt",
    )(lam, q, k, v, g)


def _attn_paged_kernel(pt_ref, lam_ref, q_ref, kn_ref, vn_ref, g_ref, *rest,
                       npg, t, out_scale):
    k_refs = rest[:npg]
    v_refs = rest[npg:2 * npg]
    o_ref, m_ref, l_ref, acc_ref = rest[2 * npg:]
    c = pl.program_id(1)
    nc = pl.num_programs(1)
    rows = 2 * t
    page = k_refs[0].shape[2]

    @pl.when(c == 0)
    def _():
        m_ref[...] = jnp.full(m_ref.shape, NEG, F32)
        l_ref[...] = jnp.zeros(l_ref.shape, F32)
        acc_ref[...] = jnp.zeros(acc_ref.shape, F32)

    def update(scores, values):
        s_all = jnp.concatenate(scores, axis=0)
        m_old = m_ref[...]
        m_new = jnp.maximum(m_old, jnp.max(s_all, axis=-1, keepdims=True))
        alpha = jnp.exp(m_old - m_new)
        p = jnp.exp(s_all - m_new)
        l_ref[...] = alpha * l_ref[...] + jnp.sum(p, axis=-1, keepdims=True)
        m_ref[...] = m_new
        pb = p.astype(BF16)
        for h in range(DA_HEADS):
            r0 = h * rows
            pv = None
            for lo, hi, load_v in values[h]:
                d = jnp.dot(pb[r0:r0 + rows, lo:hi], load_v(), preferred_element_type=F32)
                pv = d if pv is None else pv + d
            acc_ref[r0:r0 + rows, :] = alpha[r0:r0 + rows] * acc_ref[r0:r0 + rows, :] + pv

    q = q_ref[0].astype(BF16)
    scores, values = [], []
    for h in range(DA_HEADS):
        qh = q[h * rows:(h + 1) * rows]
        sh, vh = [], []
        for i in range(npg):
            kb = k_refs[i][0, h * LANES:(h + 1) * LANES, :].astype(BF16)
            sh.append(jnp.dot(qh, kb, preferred_element_type=F32))
            vh.append((i * page, (i + 1) * page,
                       lambda i=i, h=h: v_refs[i][0, pl.ds(h, page, stride=DA_HEADS), :].astype(BF16)))
        scores.append(jnp.concatenate(sh, axis=1))
        values.append(vh)
    update(scores, values)

    @pl.when(c == nc - 1)
    def _():
        tp = kn_ref.shape[1]
        scores, values = [], []
        row = lax.broadcasted_iota(I32, (rows, tp), 0) & (t - 1)
        col = lax.broadcasted_iota(I32, (rows, tp), 1)
        for h in range(DA_HEADS):
            qh = q[h * rows:(h + 1) * rows]
            kb = kn_ref[0, :, h * LANES:(h + 1) * LANES].astype(BF16)
            s = lax.dot_general(qh, kb, (((1,), (1,)), ((), ())), preferred_element_type=F32)
            scores.append(jnp.where(col <= row, s, NEG))
            values.append([(0, tp, lambda h=h: vn_ref[0, :, h * LANES:(h + 1) * LANES].astype(BF16))])
        update(scores, values)
        o = acc_ref[...] / l_ref[...]
        lam = lam_ref[0]
        for h in range(DA_HEADS):
            r0 = h * rows
            out = o[r0:r0 + t] - lam * o[r0 + t:r0 + rows]
            ms = jnp.mean(out * out, axis=-1, keepdims=True)
            gh = g_ref[:, h * LANES:(h + 1) * LANES]
            o_ref[0, :, h * LANES:(h + 1) * LANES] = (
                out * lax.rsqrt(ms + EPS) * gh * out_scale).astype(o_ref.dtype)


def _attn_paged(page_table, lam, qm, k_new, v_new, g, cache_k, cache_v, page0, t, out_scale, npg):
    db, n_pages = page_table.shape
    prow, pcol = cache_k.shape[1], cache_k.shape[2]
    assert cache_v.shape[1:] == (prow, pcol)
    w = k_new.shape[2]
    tp = k_new.shape[1]
    nc = n_pages // npg
    pt_flat = page_table.reshape(-1) + page0

    def page_spec(i):
        return pl.BlockSpec((1, prow, pcol),
                            lambda b, c, pt: (pt[b * n_pages + c * npg + i], 0, 0))

    in_specs = [pl.BlockSpec(memory_space=pltpu.SMEM),
                pl.BlockSpec((1, qm.shape[1], LANES), lambda b, c, pt: (b, 0, 0)),
                pl.BlockSpec((1, tp, w), lambda b, c, pt: (b, 0, 0)),
                pl.BlockSpec((1, tp, w), lambda b, c, pt: (b, 0, 0)),
                pl.BlockSpec((1, w), lambda b, c, pt: (0, 0))]
    in_specs += [page_spec(i) for i in range(npg)] * 2
    rows = DA_HEADS * 2 * t
    return pl.pallas_call(
        functools.partial(_attn_paged_kernel, npg=npg, t=t, out_scale=out_scale),
        out_shape=jax.ShapeDtypeStruct((db, t, w), BF16),
        grid_spec=pltpu.PrefetchScalarGridSpec(
            num_scalar_prefetch=1,
            grid=(db, nc),
            in_specs=in_specs,
            out_specs=pl.BlockSpec((1, t, w), lambda b, c, pt: (b, 0, 0)),
            scratch_shapes=[pltpu.VMEM((rows, 1), F32),
                            pltpu.VMEM((rows, 1), F32),
                            pltpu.VMEM((rows, LANES), F32)]),
        compiler_params=_params(("arbitrary", "arbitrary")),
        name="attn_paged",
    )(pt_flat, lam, qm, k_new, v_new, g, *([cache_k] * npg), *([cache_v] * npg))


def _ret_kernel(rq_ref, rk_ref, rv_ref, rg_ref, s0_ref, dmat_ref, dq_ref, dk_ref, gc_ref, g_ref,
                o_ref, sout_ref, st_ref):
    c = pl.program_id(1)

    @pl.when(c == 0)
    def _():
        st_ref[...] = s0_ref[0]

    for h in range(RET_HEADS):
        kq = slice(h * RET_KD, (h + 1) * RET_KD)
        vs = slice(h * RET_VD, (h + 1) * RET_VD)
        qb = rq_ref[0, :, kq].astype(BF16)
        k = rk_ref[0, :, kq]
        vb = rv_ref[0, :, vs].astype(BF16)
        st = st_ref[h]
        att = lax.dot_general(qb, k.astype(BF16), (((1,), (1,)), ((), ())),
                              preferred_element_type=F32) * dmat_ref[h]
        inner = jnp.dot(att.astype(BF16), vb, preferred_element_type=F32)
        cross = jnp.dot(qb, st.astype(BF16), preferred_element_type=F32) * dq_ref[h]
        r = inner + cross
        kd = (k * dk_ref[h]).astype(BF16)
        st_ref[h] = gc_ref[h] * st + lax.dot_general(
            kd, vb, (((0,), (0,)), ((), ())), preferred_element_type=F32)
        mu = jnp.mean(r, axis=-1, keepdims=True)
        xc = r - mu
        var = jnp.mean(xc * xc, axis=-1, keepdims=True)
        y = xc * lax.rsqrt(var + EPS) * g_ref[:, vs]
        rg = rg_ref[0, :, vs]
        o_ref[0, :, vs] = (rg * jax.nn.sigmoid(rg) * y).astype(o_ref.dtype)

    @pl.when(c == pl.num_programs(1) - 1)
    def _():
        sout_ref[0] = st_ref[...]


def _ret_tables(log_g, chunk, valid):
    idx = jnp.arange(chunk, dtype=F32)
    diff = idx[:, None] - idx[None, :]
    dmat = jnp.where(diff[None] >= 0,
                     jnp.exp(jnp.maximum(diff, 0.0)[None] * log_g[:, None, None]), 0.0)
    dq = jnp.exp((idx + 1.0)[None, :] * log_g[:, None])
    dk = jnp.where(idx[None, :] < valid,
                   jnp.exp((valid - 1.0 - idx)[None, :] * log_g[:, None]), 0.0)
    gc = jnp.exp(valid * log_g)
    h = log_g.shape[0]
    return (dmat.astype(F32),
            jnp.broadcast_to(dq[:, :, None], (h, chunk, RET_VD)).astype(F32),
            jnp.broadcast_to(dk[:, :, None], (h, chunk, RET_KD)).astype(F32),
            jnp.broadcast_to(gc[:, None, None], (h, 1, RET_VD)).astype(F32))


def _retention(rqk, rv, gates, state0, g, log_g, chunk, valid):
    b, s, _ = rqk.shape
    nc = s // chunk
    dmat, dq, dk, gc = _ret_tables(log_g, chunk, valid)
    qw = RET_HEADS * RET_KD
    vw = RET_HEADS * RET_VD
    full = lambda shp: pl.BlockSpec(shp, lambda bi, c: (0,) * len(shp))
    return pl.pallas_call(
        _ret_kernel,
        out_shape=(jax.ShapeDtypeStruct((b, s, vw), BF16),
                   jax.ShapeDtypeStruct((b, RET_HEADS, RET_KD, RET_VD), F32)),
        grid=(b, nc),
        in_specs=[pl.BlockSpec((1, chunk, qw), lambda bi, c: (bi, c, 0)),
                  pl.BlockSpec((1, chunk, qw), lambda bi, c: (bi, c, 1)),
                  pl.BlockSpec((1, chunk, vw), lambda bi, c: (bi, c, 0)),
                  pl.BlockSpec((1, chunk, vw), lambda bi, c: (bi, c, 0)),
                  pl.BlockSpec((1, RET_HEADS, RET_KD, RET_VD), lambda bi, c: (bi, 0, 0, 0)),
                  full(dmat.shape), full(dq.shape), full(dk.shape), full(gc.shape),
                  full((1, vw))],
        out_specs=(pl.BlockSpec((1, chunk, vw), lambda bi, c: (bi, c, 0)),
                   pl.BlockSpec((1, RET_HEADS, RET_KD, RET_VD), lambda bi, c: (bi, 0, 0, 0))),
        scratch_shapes=[pltpu.VMEM((RET_HEADS, RET_KD, RET_VD), F32)],
        compiler_params=_params(("arbitrary", "arbitrary")),
        name="retention",
    )(rqk, rqk, rv, gates, state0, dmat, dq, dk, gc, g.reshape(1, vw))


def _finish_kernel(a_ref, r_ref, ga_ref, gr_ref, h_ref, wa_ref, wr_ref, wo_ref, gffn_ref,
                   wrt_ref, brt_ref, carry0_ref,
                   h1_ref, hn_ref, route_ref, cnt_ref, carry_ref, *, tm):
    i = pl.program_id(0)

    @pl.when(i == 0)
    def _():
        carry_ref[...] = carry0_ref[...]

    am = jnp.dot(a_ref[...], wa_ref[...], preferred_element_type=F32)
    rm = jnp.dot(r_ref[...], wr_ref[...], preferred_element_type=F32)
    mix = jax.nn.sigmoid(ga_ref[...]) * am + jax.nn.sigmoid(gr_ref[...]) * rm
    h1 = h_ref[...] + jnp.dot(mix.astype(BF16), wo_ref[...], preferred_element_type=F32)
    h1_ref[...] = h1
    ms = jnp.mean(h1 * h1, axis=-1, keepdims=True)
    hn = h1 * lax.rsqrt(ms + EPS) * gffn_ref[...]
    hn_ref[...] = hn
    logits = jnp.dot(hn.astype(BF16), wrt_ref[...], preferred_element_type=F32) + brt_ref[...]
    lane = lax.broadcasted_iota(I32, (tm, LANES), 1)
    is_g = (lane >= N_EXPERTS) & (lane < N_EXPERTS + N_GROUPS)
    glog = jnp.where(is_g, logits, -jnp.inf)
    gmax = jnp.max(glog, axis=-1, keepdims=True)
    gidx = jnp.min(jnp.where(glog == gmax, lane - N_EXPERTS, LANES), axis=-1, keepdims=True)
    gden = jnp.sum(jnp.where(is_g, jnp.exp(glog - gmax), 0.0), axis=-1, keepdims=True)
    gp = 1.0 / gden
    in_g = (lane < N_EXPERTS) & ((lane // EXPERTS_PER_GROUP) == gidx)
    e1 = jnp.where(in_g, logits, -jnp.inf)
    v1 = jnp.max(e1, axis=-1, keepdims=True)
    i1 = jnp.min(jnp.where(e1 == v1, lane, LANES), axis=-1, keepdims=True)
    e2 = jnp.where(lane == i1, -jnp.inf, e1)
    v2 = jnp.max(e2, axis=-1, keepdims=True)
    i2 = jnp.min(jnp.where(e2 == v2, lane, LANES), axis=-1, keepdims=True)
    tt = jnp.exp(v2 - v1)
    w1 = gp / (1.0 + tt)
    w2 = gp * tt / (1.0 + tt)
    oh = jnp.where((lane == i1) | (lane == i2), 1.0, 0.0)
    rr = lax.broadcasted_iota(I32, (tm, tm), 0)
    cc = lax.broadcasted_iota(I32, (tm, tm), 1)
    tri = jnp.where(cc < rr, 1.0, 0.0).astype(BF16)
    before = jnp.dot(tri, oh.astype(BF16), preferred_element_type=F32) + carry_ref[...]
    rank1 = jnp.sum(jnp.where(lane == i1, before, 0.0), axis=-1, keepdims=True)
    rank2 = jnp.sum(jnp.where(lane == i2, before, 0.0), axis=-1, keepdims=True)
    carry = carry_ref[...] + jnp.sum(oh, axis=0, keepdims=True)
    carry_ref[...] = carry
    cnt_ref[...] = carry
    cols = [i1.astype(F32), i2.astype(F32), w1, w2, rank1, rank2]
    route = jnp.zeros((tm, LANES), F32)
    for ci, val in enumerate(cols):
        route = jnp.where(lane == ci, val, route)
    route_ref[...] = route


def _finish(a, r, gagr, h, wa, wr, wo, gffn, wrt, brt, carry0, tm):
    m, d = h.shape
    aw = a.shape[1]
    assert gagr.shape[1] == 2 * d
    const = lambda shp: pl.BlockSpec(shp, lambda i: (0,) * len(shp))
    return pl.pallas_call(
        functools.partial(_finish_kernel, tm=tm),
        out_shape=(jax.ShapeDtypeStruct((m, d), F32),
                   jax.ShapeDtypeStruct((m, d), F32),
                   jax.ShapeDtypeStruct((m, LANES), F32),
                   jax.ShapeDtypeStruct((1, LANES), F32)),
        grid=(m // tm,),
        in_specs=[pl.BlockSpec((tm, aw), lambda i: (i, 0)),
                  pl.BlockSpec((tm, aw), lambda i: (i, 0)),
                  pl.BlockSpec((tm, d), lambda i: (i, 0)),
                  pl.BlockSpec((tm, d), lambda i: (i, 1)),
                  pl.BlockSpec((tm, d), lambda i: (i, 0)),
                  const(wa.shape), const(wr.shape), const(wo.shape), const((1, d)),
                  const(wrt.shape), const((1, LANES)), const((1, LANES))],
        out_specs=(pl.BlockSpec((tm, d), lambda i: (i, 0)),
                   pl.BlockSpec((tm, d), lambda i: (i, 0)),
                   pl.BlockSpec((tm, LANES), lambda i: (i, 0)),
                   pl.BlockSpec((1, LANES), lambda i: (0, 0))),
        scratch_shapes=[pltpu.VMEM((1, LANES), F32)],
        compiler_params=_params(("arbitrary",)),
        name="finish",
    )(a, r, gagr, gagr, h, wa, wr, wo, gffn.reshape(1, d), wrt, brt, carry0)


def _moe_kernel(item_e, item_start, item_n, row_src,
                hn_hbm, wg_ref, wu_ref, wd_ref, yk_hbm,
                xg, yacc, wgb, wub, wdb, gsem, ssem, *, sub):
    i = pl.program_id(0)
    f = pl.program_id(1)
    nf = pl.num_programs(1)
    n = item_n[i]
    start = item_start[i]

    @pl.when((i == 0) & (f == 0))
    def _():
        xg[...] = jnp.zeros(xg.shape, xg.dtype)

    def gather_copy(r, tok):
        return pltpu.make_async_copy(hn_hbm.at[pl.ds(tok, 1)], xg.at[pl.ds(r, 1)], gsem)

    def scatter_copy(r, slot, tok):
        return pltpu.make_async_copy(yacc.at[pl.ds(r, 1)], yk_hbm.at[slot, pl.ds(tok, 1)], ssem)

    @pl.when((f == 0) & (n > 0))
    def _():
        def issue(r, c):
            gather_copy(r, row_src[start + r] >> 1).start()
            return c
        lax.fori_loop(0, n, issue, 0)

        def wait(r, c):
            gather_copy(0, 0).wait()
            return c
        lax.fori_loop(0, n, wait, 0)

    @pl.when(n > 0)
    def _():
        wgb[...] = wg_ref[0].astype(BF16)
        wub[...] = wu_ref[0].astype(BF16)
        wdb[...] = wd_ref[0].astype(BF16)

        def sub_block(s, c):
            off = pl.multiple_of(s * sub, sub)
            x = xg[pl.ds(off, sub), :].astype(BF16)
            hg = jnp.dot(x, wgb[...], preferred_element_type=F32)
            hu = jnp.dot(x, wub[...], preferred_element_type=F32)
            hm = (hg * jax.nn.sigmoid(hg) * hu).astype(BF16)
            part = jnp.dot(hm, wdb[...], preferred_element_type=F32)

            @pl.when(f == 0)
            def _():
                yacc[pl.ds(off, sub), :] = part

            @pl.when(f > 0)
            def _():
                yacc[pl.ds(off, sub), :] += part
            return c
        lax.fori_loop(0, (n + sub - 1) // sub, sub_block, 0)

    @pl.when((f == nf - 1) & (n > 0))
    def _():
        def issue(r, c):
            src = row_src[start + r]
            scatter_copy(r, src & 1, src >> 1).start()
            return c
        lax.fori_loop(0, n, issue, 0)

        def wait(r, c):
            scatter_copy(0, 0, 0).wait()
            return c
        lax.fori_loop(0, n, wait, 0)


def _moe_plan(route, counts, rows_per_item, max_items):
    ntok = route.shape[0]
    eid = route[:, 0:2].astype(I32)
    rank = route[:, 4:6].astype(I32)
    counts = counts.astype(I32)
    ends = jnp.cumsum(counts)
    starts = ends - counts
    dest = (starts[eid] + rank).reshape(-1)
    src = jnp.arange(2 * ntok, dtype=I32)
    row_src = jnp.zeros((2 * ntok,), I32).at[dest].set(src)
    nit = (counts + rows_per_item - 1) // rows_per_item
    it_end = jnp.cumsum(nit)
    it_first = it_end - nit
    total = it_end[-1]
    t = jnp.arange(max_items, dtype=I32)
    e = jnp.minimum(jnp.searchsorted(it_end, t, side='right'), N_EXPERTS - 1).astype(I32)
    j = t - it_first[e]
    n = jnp.clip(counts[e] - j * rows_per_item, 0, rows_per_item)
    n = jnp.where(t < total, n, 0).astype(I32)
    start = (starts[e] + j * rows_per_item).astype(I32)
    start = jnp.where(n > 0, start, 0)
    last_e = e[jnp.maximum(total - 1, 0)]
    e = jnp.where(t < total, e, last_e)
    return e, start, n, row_src


def _moe(hn, route, counts, wg, wu, wd):
    ntok, d = hn.shape
    ne, _, ff = wg.shape
    rows, sub, nf = MOE_ROWS, MOE_SUB, MOE_FSPLIT
    tf = ff // nf
    max_items = ne + (2 * ntok) // rows
    item_e, item_start, item_n, row_src = _moe_plan(route, counts, rows, max_items)

    def f_eff(i, f, item_n):
        return jnp.where(item_n[i] > 0, f, nf - 1)

    return pl.pallas_call(
        functools.partial(_moe_kernel, sub=sub),
        out_shape=jax.ShapeDtypeStruct((2, ntok, d), F32),
        grid_spec=pltpu.PrefetchScalarGridSpec(
            num_scalar_prefetch=4,
            grid=(max_items, nf),
            in_specs=[pl.BlockSpec(memory_space=pl.ANY),
                      pl.BlockSpec((1, d, tf), lambda i, f, ie, ist, inn, rs: (ie[i], 0, f_eff(i, f, inn))),
                      pl.BlockSpec((1, d, tf), lambda i, f, ie, ist, inn, rs: (ie[i], 0, f_eff(i, f, inn))),
                      pl.BlockSpec((1, tf, d), lambda i, f, ie, ist, inn, rs: (ie[i], f_eff(i, f, inn), 0))],
            out_specs=pl.BlockSpec(memory_space=pl.ANY),
            scratch_shapes=[pltpu.VMEM((rows, d), F32),
                            pltpu.VMEM((rows, d), F32),
                            pltpu.VMEM((d, tf), BF16),
                            pltpu.VMEM((d, tf), BF16),
                            pltpu.VMEM((tf, d), BF16),
                            pltpu.SemaphoreType.DMA,
                            pltpu.SemaphoreType.DMA]),
        compiler_params=_params(("arbitrary", "arbitrary")),
        name="moe_experts",
    )(item_e, item_start, item_n, row_src, hn, wg, wu, wd)


def _ple_kernel(h1_ref, y0_ref, y1_ref, route_ref, pe_ref, gple_ref, wg_ref, wp_ref, gfin_ref,
                o_ref):
    route = route_ref[...]
    y = route[:, 2:3] * y0_ref[0] + route[:, 3:4] * y1_ref[0]
    h2 = h1_ref[...] + y
    ms = jnp.mean(h2 * h2, axis=-1, keepdims=True)
    hn = (h2 * lax.rsqrt(ms + EPS) * gple_ref[...]).astype(BF16)
    gate = jax.nn.sigmoid(jnp.dot(hn, wg_ref[...], preferred_element_type=F32))
    pp = jnp.dot(pe_ref[...].astype(BF16), wp_ref[...], preferred_element_type=F32)
    h3 = h2 + gate * pp
    ms = jnp.mean(h3 * h3, axis=-1, keepdims=True)
    o_ref[...] = h3 * lax.rsqrt(ms + EPS) * gfin_ref[...]


def _ple(h1, yk, row0, route, pe, gple, wg, wp, gfin, tm):
    m, d = h1.shape
    blk0 = row0 // tm
    const = lambda shp: pl.BlockSpec(shp, lambda i: (0,) * len(shp))
    return pl.pallas_call(
        _ple_kernel,
        out_shape=jax.ShapeDtypeStruct((m, d), F32),
        grid=(m // tm,),
        in_specs=[pl.BlockSpec((tm, d), lambda i: (i, 0)),
                  pl.BlockSpec((1, tm, d), lambda i: (0, i + blk0, 0)),
                  pl.BlockSpec((1, tm, d), lambda i: (1, i + blk0, 0)),
                  pl.BlockSpec((tm, LANES), lambda i: (i, 0)),
                  pl.BlockSpec((tm, pe.shape[1]), lambda i: (i, 0)),
                  const((1, d)), const(wg.shape), const(wp.shape), const((1, d))],
        out_specs=pl.BlockSpec((tm, d), lambda i: (i, 0)),
        compiler_params=_params(("arbitrary",)),
        name="ple_final",
    )(h1, yk, yk, route, pe, gple.reshape(1, d), wg, wp, gfin.reshape(1, d))


def _project_group(x, g, w_in, pos, tm, q_scale):
    xn = _rmsnorm(x, g, tm)
    da_q = _rope_tables(pos, DA_ROT, DA_HD, ROPE_THETA, q_scale)[:, None]
    da_k = _rope_tables(pos, DA_ROT, DA_HD, ROPE_THETA, 1.0)[:, None]
    rq_t = _rope_tables(pos, RET_KD, RET_KD, RET_THETA, 1.0)
    rk_t = _rope_tables(pos, RET_KD, RET_KD, RET_THETA, RET_KD ** -0.5)
    ret_t = jnp.stack([rq_t, rk_t], axis=1)
    q = _proj(xn, w_in, COL_Q, 1024, tm, BF16, rope=(da_q, DA_ROT // 2))
    k = _proj(xn, w_in, COL_K, 1024, tm, F32, rope=(da_k, DA_ROT // 2))
    v = _proj(xn, w_in, COL_V, 1024, tm, F32)
    rqk = _proj(xn, w_in, COL_RQK, 1024, tm, F32, rope=(ret_t, RET_KD // 2))
    rv = _proj(xn, w_in, COL_RV, 1024, tm, F32)
    rg = _proj(xn, w_in, COL_GATES, 1024, tm, F32)
    gagr = _proj(xn, w_in, COL_GATES + 1024, 4096, tm, F32)
    return q, k, v, rqk, rv, rg, gagr


def kernel(x_prompt, x_sample, cache_k, cache_v, state_ret, page_table, p_prompt, p_sample, norm_mix_g, w_in, lam_q1, lam_k1, lam_q2, lam_k2, da_norm_g, ret_norm_g, w_br_attn, w_br_ret, w_out, norm_ffn_g, w_router_group, b_router_group, w_router_expert, b_router_expert, w_exp_gate, w_exp_up, w_exp_down, norm_ple_g, w_ple_gate, w_ple_proj, final_norm_g):
    bp, sp, d = x_prompt.shape
    bs, ts, _ = x_sample.shape
    depth = w_in.shape[0]
    assert depth == 1, "the final norm is fused into the last stage of a single layer"
    n_pool, page = cache_k.shape[1], cache_k.shape[2]
    mp, msm = bp * sp, bs * ts
    log_g = jnp.log1p(-jnp.exp2(-5.0 - jnp.arange(RET_HEADS, dtype=F32)))
    pos_p = jnp.arange(sp)
    pos_s = jnp.tile(PAST_LEN + jnp.arange(ts), bs)
    cache_k2 = cache_k.transpose(0, 1, 3, 4, 5, 2).reshape(depth * n_pool, DA_HEADS * 2 * DA_HD, page)
    cache_v2 = cache_v.reshape(depth * n_pool, page * DA_HEADS, DA_VD)
    tm_p = next(t for t in (1024, 512, 256, 128) if sp % t == 0)
    tf_p = 256

    hp = x_prompt.reshape(mp, d)
    hs = x_sample.reshape(msm, d)
    kp_l, vp_l, rp_l, ks_l, vs_l, rs_l = [], [], [], [], [], []
    for li in range(depth):
        lam_init = 0.8 - 0.6 * math.exp(-0.3 * li)
        lam = (jnp.exp(jnp.sum(lam_q1[li] * lam_k1[li]))
               - jnp.exp(jnp.sum(lam_q2[li] * lam_k2[li])) + lam_init).reshape(1).astype(F32)
        out_scale = 1.0 - lam_init
        wa = w_br_attn[li].astype(BF16)
        wr = w_br_ret[li].astype(BF16)
        wo = w_out[li].astype(BF16)
        wpg = w_ple_gate[li].astype(BF16)
        wpp = w_ple_proj[li].astype(BF16)
        wrt = jnp.zeros((d, LANES), F32)
        wrt = wrt.at[:, :N_EXPERTS].set(w_router_expert[li])
        wrt = wrt.at[:, N_EXPERTS:N_EXPERTS + N_GROUPS].set(w_router_group[li]).astype(BF16)
        brt = jnp.zeros((1, LANES), F32)
        brt = brt.at[0, :N_EXPERTS].set(b_router_expert[li])
        brt = brt.at[0, N_EXPERTS:N_EXPERTS + N_GROUPS].set(b_router_group[li])
        da_g = da_norm_g[li].reshape(1, -1)

        q, k, v, rqk, rv, rg, gagr = _project_group(hp, norm_mix_g[li], w_in[li], pos_p, tm_p,
                                                    DA_HD ** -0.5 * math.log2(math.e))
        a_p = _attn_prompt(lam, q, k, v, da_g, bp, sp, out_scale, tq=256, hp=4)
        chunk = 256 if sp % 256 == 0 else 128
        r_p, st_p = _retention(rqk.reshape(bp, sp, -1), rv.reshape(bp, sp, -1),
                               rg.reshape(bp, sp, -1),
                               jnp.zeros((bp, RET_HEADS, RET_KD, RET_VD), F32),
                               ret_norm_g[li], log_g, chunk, chunk)
        kp_l.append(k.reshape(bp, sp, DA_HEADS, 2, DA_HD))
        vp_l.append(v.reshape(bp, sp, DA_HEADS, DA_VD))
        rp_l.append(st_p)
        h1_p, hn_p, route_p, cnt_p = _finish(
            a_p, r_p.reshape(mp, -1), gagr, hp, wa, wr, wo, norm_ffn_g[li], wrt, brt,
            jnp.zeros((1, LANES), F32), tf_p)

        q, k, v, rqk, rv, rg, gagr = _project_group(hs, norm_mix_g[li], w_in[li], pos_s, msm,
                                                    DA_HD ** -0.5)
        tpad = 8
        pad_t = lambda z: jnp.pad(z.reshape(bs, ts, -1), ((0, 0), (0, tpad - ts), (0, 0)))
        q4 = q.astype(F32).reshape(bs, ts, DA_HEADS, 2, DA_HD)
        eye = jnp.eye(2, dtype=F32)
        qm = jnp.einsum('bthmd,mn->bhmtnd', q4, eye).reshape(bs, DA_HEADS * 2 * ts, 2 * DA_HD)
        a_s = _attn_paged(page_table, lam, qm, pad_t(k), pad_t(v), da_g, cache_k2, cache_v2,
                          li * n_pool, ts, out_scale, npg=8)
        r_s, st_s = _retention(pad_t(rqk), pad_t(rv), pad_t(rg), state_ret[li],
                               ret_norm_g[li], log_g, tpad, ts)
        ks_l.append(k.reshape(bs, ts, DA_HEADS, 2, DA_HD))
        vs_l.append(v.reshape(bs, ts, DA_HEADS, DA_VD))
        rs_l.append(st_s)
        h1_s, hn_s, route_s, cnt = _finish(
            a_s.reshape(msm, -1), r_s[:, :ts].reshape(msm, -1), gagr, hs, wa, wr, wo,
            norm_ffn_g[li], wrt, brt, cnt_p, msm)

        hn_all = jnp.concatenate([hn_p, hn_s], axis=0)
        route_all = jnp.concatenate([route_p, route_s], axis=0)
        yk = _moe(hn_all, route_all, cnt[0, :N_EXPERTS], w_exp_gate[li], w_exp_up[li],
                  w_exp_down[li])

        gfin = final_norm_g
        hp = _ple(h1_p, yk, 0, route_p, p_prompt[li].reshape(mp, -1), norm_ple_g[li], wpg, wpp,
                  gfin, tf_p)
        hs = _ple(h1_s, yk, mp, route_s, p_sample[li].reshape(msm, -1), norm_ple_g[li], wpg, wpp,
                  gfin, msm)
    y_prompt = hp.reshape(bp, sp, d)
    y_sample = hs.reshape(bs, ts, d)
    return (y_prompt, y_sample, jnp.stack(kp_l), jnp.stack(vp_l), jnp.stack(rp_l),
            jnp.stack(ks_l), jnp.stack(vs_l), jnp.stack(rs_l))
```

```python
import functools
import math

import jax
import jax.numpy as jnp
import numpy as np
from jax import lax
from jax.experimental import pallas as pl
from jax.experimental.pallas import tpu as pltpu

F32 = jnp.float32
BF16 = jnp.bfloat16
I32 = jnp.int32

PAST_LEN = 16384
DA_HEADS = 8
DA_VD = 128
DA_HD = 64
DA_ROT = 16
ROPE_THETA = 500000.0
RET_HEADS = 8
RET_VD = 128
RET_KD = 64
RET_THETA = 10000.0
N_GROUPS = 4
EXPERTS_PER_GROUP = 8
N_EXPERTS = N_GROUPS * EXPERTS_PER_GROUP
EPS = 1e-6
NEG = -1e30
LANES = 128

VMEM_LIMIT = 56 * 1024 * 1024

COL_Q, COL_K, COL_V, COL_RQK, COL_RV, COL_GATES = 0, 1024, 2048, 3072, 4096, 5120

MOE_ROWS = 1024
MOE_SUB = 128
MOE_FSPLIT = 4


def _params(sem, vmem=VMEM_LIMIT):
    return pltpu.CompilerParams(dimension_semantics=sem, vmem_limit_bytes=vmem)


def _rmsnorm_kernel(x_ref, g_ref, o_ref):
    x = x_ref[...]
    ms = jnp.mean(x * x, axis=-1, keepdims=True)
    o_ref[...] = (x * lax.rsqrt(ms + EPS) * g_ref[...]).astype(o_ref.dtype)


def _rmsnorm(x, g, tm):
    m, d = x.shape
    return pl.pallas_call(
        _rmsnorm_kernel,
        out_shape=jax.ShapeDtypeStruct((m, d), BF16),
        grid=(m // tm,),
        in_specs=[pl.BlockSpec((tm, d), lambda i: (i, 0)),
                  pl.BlockSpec((1, d), lambda i: (0, 0))],
        out_specs=pl.BlockSpec((tm, d), lambda i: (i, 0)),
        compiler_params=_params(("arbitrary",)),
        name="rmsnorm",
    )(x, g.reshape(1, d))


def _proj_kernel(*refs, shift, tn):
    if shift:
        xn_ref, w_ref, c_ref, s1_ref, s2_ref, o_ref, wbf_ref = refs
    else:
        xn_ref, w_ref, o_ref, wbf_ref = refs

    @pl.when(pl.program_id(1) == 0)
    def _():
        wbf_ref[...] = w_ref[...].astype(BF16)

    acc = jnp.dot(xn_ref[...], wbf_ref[...], preferred_element_type=F32)
    if shift:
        groups = c_ref.shape[0]
        rep = tn // LANES // groups
        wide = lambda ref: jnp.concatenate(
            [jnp.tile(ref[g], (1, rep)) for g in range(groups)], axis=1)
        acc = (acc * wide(c_ref) + pltpu.roll(acc, tn - shift, 1) * wide(s1_ref)
               + pltpu.roll(acc, shift, 1) * wide(s2_ref))
    o_ref[...] = acc.astype(o_ref.dtype)


def _proj(xn, w_in, col0, ncols, tm, out_dtype, rope=None):
    m, d = xn.shape
    tn = 1024
    nj = ncols // tn
    j0 = col0 // tn
    in_specs = [pl.BlockSpec((tm, d), lambda j, i: (i, 0)),
                pl.BlockSpec((d, tn), lambda j, i: (0, j + j0))]
    args = [xn, w_in]
    shift = 0
    if rope is not None:
        assert nj == 1
        tabs, shift = rope
        groups, npos = tabs.shape[1], tabs.shape[2]
        npb = npos // tm
        for t in range(3):
            in_specs.append(pl.BlockSpec((groups, tm, LANES), lambda j, i: (0, i % npb, 0)))
            args.append(tabs[t])
    return pl.pallas_call(
        functools.partial(_proj_kernel, shift=shift, tn=tn),
        out_shape=jax.ShapeDtypeStruct((m, ncols), out_dtype),
        grid=(nj, m // tm),
        in_specs=in_specs,
        out_specs=pl.BlockSpec((tm, tn), lambda j, i: (i, j)),
        scratch_shapes=[pltpu.VMEM((d, tn), BF16)],
        compiler_params=_params(("arbitrary", "arbitrary")),
        name="in_proj",
    )(*args)


def _rope_tables(pos, rot_dim, period, theta, scale):
    half = rot_dim // 2
    inv = (1.0 / np.power(np.float32(theta), np.arange(half, dtype=np.float32)
                          * np.float32(2.0 / rot_dim))).astype(np.float32)
    ang = (pos.astype(np.float32)[:, None] * inv[None, :]).astype(np.float64)
    cos, sin = np.cos(ang), np.sin(ang)
    npos = pos.shape[0]
    pad = period - rot_dim
    c = np.concatenate([cos, cos, np.ones((npos, pad))], axis=1)
    s1 = np.concatenate([-sin, np.zeros((npos, half + pad))], axis=1)
    s2 = np.concatenate([np.zeros((npos, half)), sin, np.zeros((npos, pad))], axis=1)
    tabs = np.stack([c, s1, s2]).astype(np.float32) * np.float32(scale)
    return np.tile(tabs, (1, 1, LANES // period)).astype(np.float32)


def _attn_prompt_kernel(lam_ref, q_ref, k_ref, v_ref, g_ref, o_ref,
                        kb_ref, vt_ref, m_ref, l_ref, acc_ref, *, tq, hp, out_scale):
    qi = pl.program_id(2)
    nblk = kb_ref.shape[1]

    @pl.when(qi == 0)
    def _():
        def stage(c, carry):
            off = pl.multiple_of(c * tq, tq)
            for h in range(hp):
                cols = slice(h * LANES, (h + 1) * LANES)
                kb_ref[h, c] = k_ref[pl.ds(off, tq), cols].astype(BF16)
                vt_ref[h, c] = v_ref[pl.ds(off, tq), cols].T.astype(BF16)
            return carry
        lax.fori_loop(0, nblk, stage, 0)

    row = lax.broadcasted_iota(I32, (LANES, tq), 0)
    qqt = []
    for h in range(hp):
        qt = q_ref[:, h * LANES:(h + 1) * LANES].astype(F32).T
        qqt.append(jnp.concatenate([jnp.where(row < DA_HD, qt, 0.0),
                                    jnp.where(row >= DA_HD, qt, 0.0)], axis=1).astype(BF16))
    m_ref[...] = jnp.full(m_ref.shape, NEG, F32)
    l_ref[...] = jnp.zeros(l_ref.shape, F32)
    acc_ref[...] = jnp.zeros(acc_ref.shape, F32)

    def step(j, masked):
        sts = [jnp.dot(kb_ref[h, j], qqt[h], preferred_element_type=F32)
               for h in range(hp)]
        for h in range(hp):
            st = sts[h]
            if masked:
                key = lax.broadcasted_iota(I32, (tq, 2 * tq), 0)
                qry = lax.broadcasted_iota(I32, (tq, 2 * tq), 1) & (tq - 1)
                st = jnp.where(key <= qry, st, NEG)
            m_old = m_ref[h]
            m_new = jnp.maximum(m_old, jnp.max(st, axis=0, keepdims=True))
            alpha = jnp.exp2(m_old - m_new)
            p = jnp.exp2(st - m_new)
            l_ref[h] = alpha * l_ref[h] + jnp.sum(p, axis=0, keepdims=True)
            acc_ref[h] = alpha * acc_ref[h] + jnp.dot(vt_ref[h, j], p.astype(BF16),
                                                      preferred_element_type=F32)
            m_ref[h] = m_new

    def body(j, carry):
        step(j, False)
        return carry

    lax.fori_loop(0, qi, body, 0)
    step(qi, True)

    for h in range(hp):
        cols = slice(h * LANES, (h + 1) * LANES)
        o = acc_ref[h] / l_ref[h]
        out = (o[:, :tq] - lam_ref[0] * o[:, tq:]).T
        ms = jnp.mean(out * out, axis=-1, keepdims=True)
        out = out * lax.rsqrt(ms + EPS) * g_ref[:, cols] * out_scale
        o_ref[:, cols] = out.astype(o_ref.dtype)


def _attn_prompt(lam, q, k, v, g, b, s, out_scale, tq, hp):
    nq = s // tq
    w = hp * LANES
    return pl.pallas_call(
        functools.partial(_attn_prompt_kernel, tq=tq, hp=hp, out_scale=out_scale),
        out_shape=jax.ShapeDtypeStruct((b * s, DA_HEADS * DA_VD), BF16),
        grid=(b, DA_HEADS // hp, nq),
        in_specs=[pl.BlockSpec(memory_space=pltpu.SMEM),
                  pl.BlockSpec((tq, w), lambda bi, h, qi: (bi * nq + qi, h)),
                  pl.BlockSpec((s, w), lambda bi, h, qi: (bi, h)),
                  pl.BlockSpec((s, w), lambda bi, h, qi: (bi, h)),
                  pl.BlockSpec((1, w), lambda bi, h, qi: (0, h))],
        out_specs=pl.BlockSpec((tq, w), lambda bi, h, qi: (bi * nq + qi, h)),
        scratch_shapes=[pltpu.VMEM((hp, nq, tq, LANES), BF16),
                        pltpu.VMEM((hp, nq, LANES, tq), BF16),
                        pltpu.VMEM((hp, 1, 2 * tq), F32),
                        pltpu.VMEM((hp, 1, 2 * tq), F32),
                        pltpu.VMEM((hp, LANES, 2 * tq), F32)],
        compiler_params=_params(("arbitrary", "arbitrary", "arbitrary")),
        name="attn_prompt",
    )(lam, q, k, v, g)


def _attn_paged_kernel(pt_ref, lam_ref, q_ref, kn_ref, vn_ref, g_ref, *rest,
                       npg, t, out_scale):
    k_refs = rest[:npg]
    v_refs = rest[npg:2 * npg]
    o_ref, m_ref, l_ref, acc_ref = rest[2 * npg:]
    c = pl.program_id(1)
    nc = pl.num_programs(1)
    rows = 2 * t
    page = k_refs[0].shape[2]

    @pl.when(c == 0)
    def _():
        m_ref[...] = jnp.full(m_ref.shape, NEG, F32)
        l_ref[...] = jnp.zeros(l_ref.shape, F32)
        acc_ref[...] = jnp.zeros(acc_ref.shape, F32)

    def update(scores, values):
        s_all = jnp.concatenate(scores, axis=0)
        m_old = m_ref[...]
        m_new = jnp.maximum(m_old, jnp.max(s_all, axis=-1, keepdims=True))
        alpha = jnp.exp(m_old - m_new)
        p = jnp.exp(s_all - m_new)
        l_ref[...] = alpha * l_ref[...] + jnp.sum(p, axis=-1, keepdims=True)
        m_ref[...] = m_new
        pb = p.astype(BF16)
        for h in range(DA_HEADS):
            r0 = h * rows
            pv = None
            for lo, hi, load_v in values[h]:
                d = jnp.dot(pb[r0:r0 + rows, lo:hi], load_v(), preferred_element_type=F32)
                pv = d if pv is None else pv + d
            acc_ref[r0:r0 + rows, :] = alpha[r0:r0 + rows] * acc_ref[r0:r0 + rows, :] + pv

    q = q_ref[0].astype(BF16)
    scores, values = [], []
    for h in range(DA_HEADS):
        qh = q[h * rows:(h + 1) * rows]
        sh, vh = [], []
        for i in range(npg):
            kb = k_refs[i][0, h * LANES:(h + 1) * LANES, :].astype(BF16)
            sh.append(jnp.dot(qh, kb, preferred_element_type=F32))
            vh.append((i * page, (i + 1) * page,
                       lambda i=i, h=h: v_refs[i][0, pl.ds(h, page, stride=DA_HEADS), :].astype(BF16)))
        scores.append(jnp.concatenate(sh, axis=1))
        values.append(vh)
    update(scores, values)

    @pl.when(c == nc - 1)
    def _():
        tp = kn_ref.shape[1]
        scores, values = [], []
        row = lax.broadcasted_iota(I32, (rows, tp), 0) & (t - 1)
        col = lax.broadcasted_iota(I32, (rows, tp), 1)
        for h in range(DA_HEADS):
            qh = q[h * rows:(h + 1) * rows]
            kb = kn_ref[0, :, h * LANES:(h + 1) * LANES].astype(BF16)
            s = lax.dot_general(qh, kb, (((1,), (1,)), ((), ())), preferred_element_type=F32)
            scores.append(jnp.where(col <= row, s, NEG))
            values.append([(0, tp, lambda h=h: vn_ref[0, :, h * LANES:(h + 1) * LANES].astype(BF16))])
        update(scores, values)
        o = acc_ref[...] / l_ref[...]
        lam = lam_ref[0]
        for h in range(DA_HEADS):
            r0 = h * rows
            out = o[r0:r0 + t] - lam * o[r0 + t:r0 + rows]
            ms = jnp.mean(out * out, axis=-1, keepdims=True)
            gh = g_ref[:, h * LANES:(h + 1) * LANES]
            o_ref[0, :, h * LANES:(h + 1) * LANES] = (
                out * lax.rsqrt(ms + EPS) * gh * out_scale).astype(o_ref.dtype)


def _attn_paged(page_table, lam, qm, k_new, v_new, g, cache_k, cache_v, page0, t, out_scale, npg):
    db, n_pages = page_table.shape
    prow, pcol = cache_k.shape[1], cache_k.shape[2]
    assert cache_v.shape[1:] == (prow, pcol)
    w = k_new.shape[2]
    tp = k_new.shape[1]
    nc = n_pages // npg
    pt_flat = page_table.reshape(-1) + page0

    def page_spec(i):
        return pl.BlockSpec((1, prow, pcol),
                            lambda b, c, pt: (pt[b * n_pages + c * npg + i], 0, 0))

    in_specs = [pl.BlockSpec(memory_space=pltpu.SMEM),
                pl.BlockSpec((1, qm.shape[1], LANES), lambda b, c, pt: (b, 0, 0)),
                pl.BlockSpec((1, tp, w), lambda b, c, pt: (b, 0, 0)),
                pl.BlockSpec((1, tp, w), lambda b, c, pt: (b, 0, 0)),
                pl.BlockSpec((1, w), lambda b, c, pt: (0, 0))]
    in_specs += [page_spec(i) for i in range(npg)] * 2
    rows = DA_HEADS * 2 * t
    return pl.pallas_call(
        functools.partial(_attn_paged_kernel, npg=npg, t=t, out_scale=out_scale),
        out_shape=jax.ShapeDtypeStruct((db, t, w), BF16),
        grid_spec=pltpu.PrefetchScalarGridSpec(
            num_scalar_prefetch=1,
            grid=(db, nc),
            in_specs=in_specs,
            out_specs=pl.BlockSpec((1, t, w), lambda b, c, pt: (b, 0, 0)),
            scratch_shapes=[pltpu.VMEM((rows, 1), F32),
                            pltpu.VMEM((rows, 1), F32),
                            pltpu.VMEM((rows, LANES), F32)]),
        compiler_params=_params(("arbitrary", "arbitrary")),
        name="attn_paged",
    )(pt_flat, lam, qm, k_new, v_new, g, *([cache_k] * npg), *([cache_v] * npg))


def _ret_kernel(rq_ref, rk_ref, rv_ref, rg_ref, s0_ref, dmat_ref, dq_ref, dk_ref, gc_ref, g_ref,
                o_ref, sout_ref, st_ref):
    c = pl.program_id(1)

    @pl.when(c == 0)
    def _():
        st_ref[...] = s0_ref[0]

    for h in range(RET_HEADS):
        kq = slice(h * RET_KD, (h + 1) * RET_KD)
        vs = slice(h * RET_VD, (h + 1) * RET_VD)
        qb = rq_ref[0, :, kq].astype(BF16)
        k = rk_ref[0, :, kq]
        vb = rv_ref[0, :, vs].astype(BF16)
        st = st_ref[h]
        att = lax.dot_general(qb, k.astype(BF16), (((1,), (1,)), ((), ())),
                              preferred_element_type=F32) * dmat_ref[h]
        inner = jnp.dot(att.astype(BF16), vb, preferred_element_type=F32)
        cross = jnp.dot(qb, st.astype(BF16), preferred_element_type=F32) * dq_ref[h]
        r = inner + cross
        kd = (k * dk_ref[h]).astype(BF16)
        st_ref[h] = gc_ref[h] * st + lax.dot_general(
            kd, vb, (((0,), (0,)), ((), ())), preferred_element_type=F32)
        mu = jnp.mean(r, axis=-1, keepdims=True)
        xc = r - mu
        var = jnp.mean(xc * xc, axis=-1, keepdims=True)
        y = xc * lax.rsqrt(var + EPS) * g_ref[:, vs]
        rg = rg_ref[0, :, vs]
        o_ref[0, :, vs] = (rg * jax.nn.sigmoid(rg) * y).astype(o_ref.dtype)

    @pl.when(c == pl.num_programs(1) - 1)
    def _():
        sout_ref[0] = st_ref[...]


def _ret_tables(log_g, chunk, valid):
    f32 = np.float32
    idx = np.arange(chunk, dtype=f32)
    diff = idx[:, None] - idx[None, :]
    ex = lambda e: np.exp(e.astype(f32).astype(np.float64)).astype(f32)
    dmat = np.where(diff[None] >= 0, ex(np.maximum(diff, f32(0))[None] * log_g[:, None, None]),
                    f32(0))
    dq = ex((idx + f32(1))[None, :] * log_g[:, None])
    dk = np.where(idx[None, :] < valid,
                  ex((f32(valid - 1) - idx)[None, :] * log_g[:, None]), f32(0))
    gc = ex(f32(valid) * log_g)
    h = log_g.shape[0]
    return (dmat.astype(f32),
            np.ascontiguousarray(np.broadcast_to(dq[:, :, None], (h, chunk, RET_VD))).astype(f32),
            np.ascontiguousarray(np.broadcast_to(dk[:, :, None], (h, chunk, RET_KD))).astype(f32),
            np.ascontiguousarray(np.broadcast_to(gc[:, None, None], (h, 1, RET_VD))).astype(f32))


def _retention(rqk, rv, gates, state0, g, log_g, chunk, valid):
    b, s, _ = rqk.shape
    nc = s // chunk
    dmat, dq, dk, gc = _ret_tables(log_g, chunk, valid)
    qw = RET_HEADS * RET_KD
    vw = RET_HEADS * RET_VD
    full = lambda shp: pl.BlockSpec(shp, lambda bi, c: (0,) * len(shp))
    return pl.pallas_call(
        _ret_kernel,
        out_shape=(jax.ShapeDtypeStruct((b, s, vw), BF16),
                   jax.ShapeDtypeStruct((b, RET_HEADS, RET_KD, RET_VD), F32)),
        grid=(b, nc),
        in_specs=[pl.BlockSpec((1, chunk, qw), lambda bi, c: (bi, c, 0)),
                  pl.BlockSpec((1, chunk, qw), lambda bi, c: (bi, c, 1)),
                  pl.BlockSpec((1, chunk, vw), lambda bi, c: (bi, c, 0)),
                  pl.BlockSpec((1, chunk, vw), lambda bi, c: (bi, c, 0)),
                  pl.BlockSpec((1, RET_HEADS, RET_KD, RET_VD), lambda bi, c: (bi, 0, 0, 0)),
                  full(dmat.shape), full(dq.shape), full(dk.shape), full(gc.shape),
                  full((1, vw))],
        out_specs=(pl.BlockSpec((1, chunk, vw), lambda bi, c: (bi, c, 0)),
                   pl.BlockSpec((1, RET_HEADS, RET_KD, RET_VD), lambda bi, c: (bi, 0, 0, 0))),
        scratch_shapes=[pltpu.VMEM((RET_HEADS, RET_KD, RET_VD), F32)],
        compiler_params=_params(("arbitrary", "arbitrary")),
        name="retention",
    )(rqk, rqk, rv, gates, state0, dmat, dq, dk, gc, g.reshape(1, vw))


def _finish_kernel(a_ref, r_ref, ga_ref, gr_ref, h_ref, wa_ref, wr_ref, wo_ref, gffn_ref,
                   wrt_ref, brt_ref, carry0_ref,
                   h1_ref, hn_ref, route_ref, cnt_ref, carry_ref, *, tm):
    i = pl.program_id(0)

    @pl.when(i == 0)
    def _():
        carry_ref[...] = carry0_ref[...]

    am = jnp.dot(a_ref[...], wa_ref[...], preferred_element_type=F32)
    rm = jnp.dot(r_ref[...], wr_ref[...], preferred_element_type=F32)
    mix = jax.nn.sigmoid(ga_ref[...]) * am + jax.nn.sigmoid(gr_ref[...]) * rm
    h1 = h_ref[...] + jnp.dot(mix.astype(BF16), wo_ref[...], preferred_element_type=F32)
    h1_ref[...] = h1
    ms = jnp.mean(h1 * h1, axis=-1, keepdims=True)
    hn = h1 * lax.rsqrt(ms + EPS) * gffn_ref[...]
    hn_ref[...] = hn
    logits = jnp.dot(hn.astype(BF16), wrt_ref[...], preferred_element_type=F32) + brt_ref[...]
    lane = lax.broadcasted_iota(I32, (tm, LANES), 1)
    is_g = (lane >= N_EXPERTS) & (lane < N_EXPERTS + N_GROUPS)
    glog = jnp.where(is_g, logits, -jnp.inf)
    gmax = jnp.max(glog, axis=-1, keepdims=True)
    gidx = jnp.min(jnp.where(glog == gmax, lane - N_EXPERTS, LANES), axis=-1, keepdims=True)
    gden = jnp.sum(jnp.where(is_g, jnp.exp(glog - gmax), 0.0), axis=-1, keepdims=True)
    gp = 1.0 / gden
    in_g = (lane < N_EXPERTS) & ((lane // EXPERTS_PER_GROUP) == gidx)
    e1 = jnp.where(in_g, logits, -jnp.inf)
    v1 = jnp.max(e1, axis=-1, keepdims=True)
    i1 = jnp.min(jnp.where(e1 == v1, lane, LANES), axis=-1, keepdims=True)
    e2 = jnp.where(lane == i1, -jnp.inf, e1)
    v2 = jnp.max(e2, axis=-1, keepdims=True)
    i2 = jnp.min(jnp.where(e2 == v2, lane, LANES), axis=-1, keepdims=True)
    tt = jnp.exp(v2 - v1)
    w1 = gp / (1.0 + tt)
    w2 = gp * tt / (1.0 + tt)
    oh = jnp.where((lane == i1) | (lane == i2), 1.0, 0.0)
    rr = lax.broadcasted_iota(I32, (tm, tm), 0)
    cc = lax.broadcasted_iota(I32, (tm, tm), 1)
    tri = jnp.where(cc < rr, 1.0, 0.0).astype(BF16)
    before = jnp.dot(tri, oh.astype(BF16), preferred_element_type=F32) + carry_ref[...]
    rank1 = jnp.sum(jnp.where(lane == i1, before, 0.0), axis=-1, keepdims=True)
    rank2 = jnp.sum(jnp.where(lane == i2, before, 0.0), axis=-1, keepdims=True)
    carry = carry_ref[...] + jnp.sum(oh, axis=0, keepdims=True)
    carry_ref[...] = carry
    cnt_ref[...] = carry
    cols = [i1.astype(F32), i2.astype(F32), w1, w2, rank1, rank2]
    route = jnp.zeros((tm, LANES), F32)
    for ci, val in enumerate(cols):
        route = jnp.where(lane == ci, val, route)
    route_ref[...] = route


def _finish(a, r, gagr, h, wa, wr, wo, gffn, wrt, brt, carry0, tm):
    m, d = h.shape
    aw = a.shape[1]
    assert gagr.shape[1] == 2 * d
    const = lambda shp: pl.BlockSpec(shp, lambda i: (0,) * len(shp))
    return pl.pallas_call(
        functools.partial(_finish_kernel, tm=tm),
        out_shape=(jax.ShapeDtypeStruct((m, d), F32),
                   jax.ShapeDtypeStruct((m, d), F32),
                   jax.ShapeDtypeStruct((m, LANES), F32),
                   jax.ShapeDtypeStruct((1, LANES), F32)),
        grid=(m // tm,),
        in_specs=[pl.BlockSpec((tm, aw), lambda i: (i, 0)),
                  pl.BlockSpec((tm, aw), lambda i: (i, 0)),
                  pl.BlockSpec((tm, d), lambda i: (i, 0)),
                  pl.BlockSpec((tm, d), lambda i: (i, 1)),
                  pl.BlockSpec((tm, d), lambda i: (i, 0)),
                  const(wa.shape), const(wr.shape), const(wo.shape), const((1, d)),
                  const(wrt.shape), const((1, LANES)), const((1, LANES))],
        out_specs=(pl.BlockSpec((tm, d), lambda i: (i, 0)),
                   pl.BlockSpec((tm, d), lambda i: (i, 0)),
                   pl.BlockSpec((tm, LANES), lambda i: (i, 0)),
                   pl.BlockSpec((1, LANES), lambda i: (0, 0))),
        scratch_shapes=[pltpu.VMEM((1, LANES), F32)],
        compiler_params=_params(("arbitrary",)),
        name="finish",
    )(a, r, gagr, gagr, h, wa, wr, wo, gffn.reshape(1, d), wrt, brt, carry0)


def _moe_kernel(item_e, item_start, item_n, row_src,
                hn_hbm, wg_ref, wu_ref, wd_ref, yk_hbm,
                xg, yacc, wgb, wub, wdb, gsem, ssem, *, sub):
    i = pl.program_id(0)
    f = pl.program_id(1)
    nf = pl.num_programs(1)
    n = item_n[i]
    start = item_start[i]

    @pl.when((i == 0) & (f == 0))
    def _():
        xg[...] = jnp.zeros(xg.shape, xg.dtype)

    def gather_copy(r, tok):
        return pltpu.make_async_copy(hn_hbm.at[pl.ds(tok, 1)], xg.at[pl.ds(r, 1)], gsem)

    def scatter_copy(r, slot, tok):
        return pltpu.make_async_copy(yacc.at[pl.ds(r, 1)], yk_hbm.at[slot, pl.ds(tok, 1)], ssem)

    @pl.when((f == 0) & (n > 0))
    def _():
        def issue(r, c):
            gather_copy(r, row_src[start + r] >> 1).start()
            return c
        lax.fori_loop(0, n, issue, 0)

        def wait(r, c):
            gather_copy(0, 0).wait()
            return c
        lax.fori_loop(0, n, wait, 0)

    @pl.when(n > 0)
    def _():
        wgb[...] = wg_ref[0].astype(BF16)
        wub[...] = wu_ref[0].astype(BF16)
        wdb[...] = wd_ref[0].astype(BF16)

        def run_rows(off, m):
            x = xg[pl.ds(off, m), :].astype(BF16)
            hg = jnp.dot(x, wgb[...], preferred_element_type=F32)
            hu = jnp.dot(x, wub[...], preferred_element_type=F32)
            hm = (hg * jax.nn.sigmoid(hg) * hu).astype(BF16)
            part = jnp.dot(hm, wdb[...], preferred_element_type=F32)

            @pl.when(f == 0)
            def _():
                yacc[pl.ds(off, m), :] = part

            @pl.when(f > 0)
            def _():
                yacc[pl.ds(off, m), :] += part

        nsub = (n + sub - 1) // sub
        nbig = nsub // 4

        def big(s, c):
            run_rows(pl.multiple_of(s * (4 * sub), 4 * sub), 4 * sub)
            return c
        lax.fori_loop(0, nbig, big, 0)
        off2 = pl.multiple_of(nbig * (4 * sub), 2 * sub)

        @pl.when((nsub & 2) != 0)
        def _():
            run_rows(off2, 2 * sub)
        off1 = pl.multiple_of(off2 + (nsub & 2) * sub, sub)

        @pl.when((nsub & 1) != 0)
        def _():
            run_rows(off1, sub)

    @pl.when((f == nf - 1) & (n > 0))
    def _():
        def issue(r, c):
            src = row_src[start + r]
            scatter_copy(r, src & 1, src >> 1).start()
            return c
        lax.fori_loop(0, n, issue, 0)

        def wait(r, c):
            scatter_copy(0, 0, 0).wait()
            return c
        lax.fori_loop(0, n, wait, 0)


def _moe_plan(route, counts, rows_per_item, max_items):
    ntok = route.shape[0]
    eid = route[:, 0:2].astype(I32)
    rank = route[:, 4:6].astype(I32)
    counts = counts.astype(I32)
    ends = jnp.cumsum(counts)
    starts = ends - counts
    dest = (starts[eid] + rank).reshape(-1)
    src = jnp.arange(2 * ntok, dtype=I32)
    row_src = jnp.zeros((2 * ntok,), I32).at[dest].set(src)
    nit = (counts + rows_per_item - 1) // rows_per_item
    it_end = jnp.cumsum(nit)
    it_first = it_end - nit
    total = it_end[-1]
    t = jnp.arange(max_items, dtype=I32)
    e = jnp.minimum(jnp.searchsorted(it_end, t, side='right'), N_EXPERTS - 1).astype(I32)
    j = t - it_first[e]
    n = jnp.clip(counts[e] - j * rows_per_item, 0, rows_per_item)
    n = jnp.where(t < total, n, 0).astype(I32)
    start = (starts[e] + j * rows_per_item).astype(I32)
    start = jnp.where(n > 0, start, 0)
    last_e = e[jnp.maximum(total - 1, 0)]
    e = jnp.where(t < total, e, last_e)
    return e, start, n, row_src


def _moe(hn, route, counts, wg, wu, wd):
    ntok, d = hn.shape
    ne, _, ff = wg.shape
    rows, sub, nf = MOE_ROWS, MOE_SUB, MOE_FSPLIT
    tf = ff // nf
    max_items = ne + (2 * ntok) // rows
    item_e, item_start, item_n, row_src = _moe_plan(route, counts, rows, max_items)

    def f_eff(i, f, item_n):
        return jnp.where(item_n[i] > 0, f, nf - 1)

    return pl.pallas_call(
        functools.partial(_moe_kernel, sub=sub),
        out_shape=jax.ShapeDtypeStruct((2, ntok, d), F32),
        grid_spec=pltpu.PrefetchScalarGridSpec(
            num_scalar_prefetch=4,
            grid=(max_items, nf),
            in_specs=[pl.BlockSpec(memory_space=pl.ANY),
                      pl.BlockSpec((1, d, tf), lambda i, f, ie, ist, inn, rs: (ie[i], 0, f_eff(i, f, inn))),
                      pl.BlockSpec((1, d, tf), lambda i, f, ie, ist, inn, rs: (ie[i], 0, f_eff(i, f, inn))),
                      pl.BlockSpec((1, tf, d), lambda i, f, ie, ist, inn, rs: (ie[i], f_eff(i, f, inn), 0))],
            out_specs=pl.BlockSpec(memory_space=pl.ANY),
            scratch_shapes=[pltpu.VMEM((rows, d), F32),
                            pltpu.VMEM((rows, d), F32),
                            pltpu.VMEM((d, tf), BF16),
                            pltpu.VMEM((d, tf), BF16),
                            pltpu.VMEM((tf, d), BF16),
                            pltpu.SemaphoreType.DMA,
                            pltpu.SemaphoreType.DMA]),
        compiler_params=_params(("arbitrary", "arbitrary")),
        name="moe_experts",
    )(item_e, item_start, item_n, row_src, hn, wg, wu, wd)


def _ple_kernel(h1_ref, y0_ref, y1_ref, route_ref, pe_ref, gple_ref, wg_ref, wp_ref, gfin_ref,
                o_ref):
    route = route_ref[...]
    y = route[:, 2:3] * y0_ref[0] + route[:, 3:4] * y1_ref[0]
    h2 = h1_ref[...] + y
    ms = jnp.mean(h2 * h2, axis=-1, keepdims=True)
    hn = (h2 * lax.rsqrt(ms + EPS) * gple_ref[...]).astype(BF16)
    gate = jax.nn.sigmoid(jnp.dot(hn, wg_ref[...], preferred_element_type=F32))
    pp = jnp.dot(pe_ref[...].astype(BF16), wp_ref[...], preferred_element_type=F32)
    h3 = h2 + gate * pp
    ms = jnp.mean(h3 * h3, axis=-1, keepdims=True)
    o_ref[...] = h3 * lax.rsqrt(ms + EPS) * gfin_ref[...]


def _ple(h1, yk, row0, route, pe, gple, wg, wp, gfin, tm):
    m, d = h1.shape
    blk0 = row0 // tm
    const = lambda shp: pl.BlockSpec(shp, lambda i: (0,) * len(shp))
    return pl.pallas_call(
        _ple_kernel,
        out_shape=jax.ShapeDtypeStruct((m, d), F32),
        grid=(m // tm,),
        in_specs=[pl.BlockSpec((tm, d), lambda i: (i, 0)),
                  pl.BlockSpec((1, tm, d), lambda i: (0, i + blk0, 0)),
                  pl.BlockSpec((1, tm, d), lambda i: (1, i + blk0, 0)),
                  pl.BlockSpec((tm, LANES), lambda i: (i, 0)),
                  pl.BlockSpec((tm, pe.shape[1]), lambda i: (i, 0)),
                  const((1, d)), const(wg.shape), const(wp.shape), const((1, d))],
        out_specs=pl.BlockSpec((tm, d), lambda i: (i, 0)),
        compiler_params=_params(("arbitrary",)),
        name="ple_final",
    )(h1, yk, yk, route, pe, gple.reshape(1, d), wg, wp, gfin.reshape(1, d))


def _project_group(x, g, w_in, pos, tm, q_scale):
    xn = _rmsnorm(x, g, tm)
    da_q = _rope_tables(pos, DA_ROT, DA_HD, ROPE_THETA, q_scale)[:, None]
    da_k = _rope_tables(pos, DA_ROT, DA_HD, ROPE_THETA, 1.0)[:, None]
    rq_t = _rope_tables(pos, RET_KD, RET_KD, RET_THETA, 1.0)
    rk_t = _rope_tables(pos, RET_KD, RET_KD, RET_THETA, RET_KD ** -0.5)
    ret_t = np.stack([rq_t, rk_t], axis=1)
    q = _proj(xn, w_in, COL_Q, 1024, tm, BF16, rope=(da_q, DA_ROT // 2))
    k = _proj(xn, w_in, COL_K, 1024, tm, F32, rope=(da_k, DA_ROT // 2))
    v = _proj(xn, w_in, COL_V, 1024, tm, F32)
    rqk = _proj(xn, w_in, COL_RQK, 1024, tm, F32, rope=(ret_t, RET_KD // 2))
    rv = _proj(xn, w_in, COL_RV, 1024, tm, F32)
    rg = _proj(xn, w_in, COL_GATES, 1024, tm, F32)
    gagr = _proj(xn, w_in, COL_GATES + 1024, 4096, tm, F32)
    return q, k, v, rqk, rv, rg, gagr


def kernel(x_prompt, x_sample, cache_k, cache_v, state_ret, page_table, p_prompt, p_sample, norm_mix_g, w_in, lam_q1, lam_k1, lam_q2, lam_k2, da_norm_g, ret_norm_g, w_br_attn, w_br_ret, w_out, norm_ffn_g, w_router_group, b_router_group, w_router_expert, b_router_expert, w_exp_gate, w_exp_up, w_exp_down, norm_ple_g, w_ple_gate, w_ple_proj, final_norm_g):
    bp, sp, d = x_prompt.shape
    bs, ts, _ = x_sample.shape
    depth = w_in.shape[0]
    assert depth == 1, "the final norm is fused into the last stage of a single layer"
    n_pool, page = cache_k.shape[1], cache_k.shape[2]
    mp, msm = bp * sp, bs * ts
    log_g = np.log1p(-np.exp2(-5.0 - np.arange(RET_HEADS))).astype(np.float32)
    pos_p = np.arange(sp)
    pos_s = np.tile(PAST_LEN + np.arange(ts), bs)
    cache_k2 = cache_k.transpose(0, 1, 3, 4, 5, 2).reshape(depth * n_pool, DA_HEADS * 2 * DA_HD, page)
    cache_v2 = cache_v.reshape(depth * n_pool, page * DA_HEADS, DA_VD)
    tm_p = next(t for t in (1024, 512, 256, 128) if sp % t == 0)
    tf_p = 256

    hp = x_prompt.reshape(mp, d)
    hs = x_sample.reshape(msm, d)
    kp_l, vp_l, rp_l, ks_l, vs_l, rs_l = [], [], [], [], [], []
    for li in range(depth):
        lam_init = 0.8 - 0.6 * math.exp(-0.3 * li)
        lam = (jnp.exp(jnp.sum(lam_q1[li] * lam_k1[li]))
               - jnp.exp(jnp.sum(lam_q2[li] * lam_k2[li])) + lam_init).reshape(1).astype(F32)
        out_scale = 1.0 - lam_init
        wa = w_br_attn[li].astype(BF16)
        wr = w_br_ret[li].astype(BF16)
        wo = w_out[li].astype(BF16)
        wpg = w_ple_gate[li].astype(BF16)
        wpp = w_ple_proj[li].astype(BF16)
        wrt = jnp.zeros((d, LANES), F32)
        wrt = wrt.at[:, :N_EXPERTS].set(w_router_expert[li])
        wrt = wrt.at[:, N_EXPERTS:N_EXPERTS + N_GROUPS].set(w_router_group[li]).astype(BF16)
        brt = jnp.zeros((1, LANES), F32)
        brt = brt.at[0, :N_EXPERTS].set(b_router_expert[li])
        brt = brt.at[0, N_EXPERTS:N_EXPERTS + N_GROUPS].set(b_router_group[li])
        da_g = da_norm_g[li].reshape(1, -1)

        q, k, v, rqk, rv, rg, gagr = _project_group(hp, norm_mix_g[li], w_in[li], pos_p, tm_p,
                                                    DA_HD ** -0.5 * math.log2(math.e))
        a_p = _attn_prompt(lam, q, k, v, da_g, bp, sp, out_scale, tq=256, hp=4)
        chunk = 256 if sp % 256 == 0 else 128
        r_p, st_p = _retention(rqk.reshape(bp, sp, -1), rv.reshape(bp, sp, -1),
                               rg.reshape(bp, sp, -1),
                               jnp.zeros((bp, RET_HEADS, RET_KD, RET_VD), F32),
                               ret_norm_g[li], log_g, chunk, chunk)
        kp_l.append(k.reshape(bp, sp, DA_HEADS, 2, DA_HD))
        vp_l.append(v.reshape(bp, sp, DA_HEADS, DA_VD))
        rp_l.append(st_p)
        h1_p, hn_p, route_p, cnt_p = _finish(
            a_p, r_p.reshape(mp, -1), gagr, hp, wa, wr, wo, norm_ffn_g[li], wrt, brt,
            jnp.zeros((1, LANES), F32), tf_p)

        q, k, v, rqk, rv, rg, gagr = _project_group(hs, norm_mix_g[li], w_in[li], pos_s, msm,
                                                    DA_HD ** -0.5)
        tpad = 8
        pad_t = lambda z: jnp.pad(z.reshape(bs, ts, -1), ((0, 0), (0, tpad - ts), (0, 0)))
        q4 = q.astype(F32).reshape(bs, ts, DA_HEADS, 2, DA_HD)
        eye = jnp.eye(2, dtype=F32)
        qm = jnp.einsum('bthmd,mn->bhmtnd', q4, eye).reshape(bs, DA_HEADS * 2 * ts, 2 * DA_HD)
        a_s = _attn_paged(page_table, lam, qm, pad_t(k), pad_t(v), da_g, cache_k2, cache_v2,
                          li * n_pool, ts, out_scale, npg=8)
        r_s, st_s = _retention(pad_t(rqk), pad_t(rv), pad_t(rg), state_ret[li],
                               ret_norm_g[li], log_g, tpad, ts)
        ks_l.append(k.reshape(bs, ts, DA_HEADS, 2, DA_HD))
        vs_l.append(v.reshape(bs, ts, DA_HEADS, DA_VD))
        rs_l.append(st_s)
        h1_s, hn_s, route_s, cnt = _finish(
            a_s.reshape(msm, -1), r_s[:, :ts].reshape(msm, -1), gagr, hs, wa, wr, wo,
            norm_ffn_g[li], wrt, brt, cnt_p, msm)

        hn_all = jnp.concatenate([hn_p, hn_s], axis=0)
        route_all = jnp.concatenate([route_p, route_s], axis=0)
        yk = _moe(hn_all, route_all, cnt[0, :N_EXPERTS], w_exp_gate[li], w_exp_up[li],
                  w_exp_down[li])

        gfin = final_norm_g
        hp = _ple(h1_p, yk, 0, route_p, p_prompt[li].reshape(mp, -1), norm_ple_g[li], wpg, wpp,
                  gfin, tf_p)
        hs = _ple(h1_s, yk, mp, route_s, p_sample[li].reshape(msm, -1), norm_ple_g[li], wpg, wpp,
                  gfin, msm)
    y_prompt = hp.reshape(bp, sp, d)
    y_sample = hs.reshape(bs, ts, d)
    return (y_prompt, y_sample, jnp.stack(kp_l), jnp.stack(vp_l), jnp.stack(rp_l),
            jnp.stack(ks_l), jnp.stack(vs_l), jnp.stack(rs_l))
```

```python
import functools
import math

import jax
import jax.numpy as jnp
import numpy as np
from jax import lax
from jax.experimental import pallas as pl
from jax.experimental.pallas import tpu as pltpu

F32 = jnp.float32
BF16 = jnp.bfloat16
I32 = jnp.int32

PAST_LEN = 16384
DA_HEADS = 8
DA_VD = 128
DA_HD = 64
DA_ROT = 16
ROPE_THETA = 500000.0
RET_HEADS = 8
RET_VD = 128
RET_KD = 64
RET_THETA = 10000.0
N_GROUPS = 4
EXPERTS_PER_GROUP = 8
N_EXPERTS = N_GROUPS * EXPERTS_PER_GROUP
EPS = 1e-6
NEG = -1e30
LANES = 128

VMEM_LIMIT = 56 * 1024 * 1024

COL_Q, COL_K, COL_V, COL_RQK, COL_RV, COL_GATES = 0, 1024, 2048, 3072, 4096, 5120

MOE_ROWS = 1024
MOE_SUB = 128
MOE_FSPLIT = 4


def _params(sem, vmem=VMEM_LIMIT):
    return pltpu.CompilerParams(dimension_semantics=sem, vmem_limit_bytes=vmem)


def _store_slabs(ref, row0, x):
    m, d = x.shape
    nch = d // LANES
    for c in range(nch):
        ref[pl.ds(row0 * nch + c, m, stride=nch), :] = x[:, c * LANES:(c + 1) * LANES]


def _load_slabs(ref, row0, m, d):
    nch = d // LANES
    return jnp.concatenate(
        [ref[pl.ds(row0 * nch + c, m, stride=nch), :] for c in range(nch)], axis=1)


def _rmsnorm_kernel(x_ref, g_ref, o_ref):
    x = x_ref[...]
    ms = jnp.mean(x * x, axis=-1, keepdims=True)
    o_ref[...] = (x * lax.rsqrt(ms + EPS) * g_ref[...]).astype(o_ref.dtype)


def _rmsnorm(x, g, tm):
    m, d = x.shape
    return pl.pallas_call(
        _rmsnorm_kernel,
        out_shape=jax.ShapeDtypeStruct((m, d), BF16),
        grid=(m // tm,),
        in_specs=[pl.BlockSpec((tm, d), lambda i: (i, 0)),
                  pl.BlockSpec((1, d), lambda i: (0, 0))],
        out_specs=pl.BlockSpec((tm, d), lambda i: (i, 0)),
        compiler_params=_params(("arbitrary",)),
        name="rmsnorm",
    )(x, g.reshape(1, d))


def _proj_kernel(*refs, shift, tn, seq_tiles):
    if shift:
        xn_ref, w_ref, c_ref, s1_ref, s2_ref, o_ref, wbf_ref = refs
    else:
        xn_ref, w_ref, o_ref, wbf_ref = refs

    @pl.when(pl.program_id(1) == 0)
    def _():
        wbf_ref[...] = w_ref[...].astype(BF16)

    acc = jnp.dot(xn_ref[...], wbf_ref[...], preferred_element_type=F32)
    if shift:
        groups = c_ref.shape[0]
        rep = tn // LANES // groups
        wide = lambda ref: jnp.concatenate(
            [jnp.tile(ref[g], (1, rep)) for g in range(groups)], axis=1)
        acc = (acc * wide(c_ref) + pltpu.roll(acc, tn - shift, 1) * wide(s1_ref)
               + pltpu.roll(acc, shift, 1) * wide(s2_ref))
    if seq_tiles:
        o_ref[0] = acc.T.astype(o_ref.dtype)
    else:
        o_ref[...] = acc.astype(o_ref.dtype)


def _proj(xn, w_in, col0, ncols, tm, out_dtype, rope=None, seq_tiles=0):
    m, d = xn.shape
    tn = 1024
    nj = ncols // tn
    j0 = col0 // tn
    in_specs = [pl.BlockSpec((tm, d), lambda j, i: (i, 0)),
                pl.BlockSpec((d, tn), lambda j, i: (0, j + j0))]
    args = [xn, w_in]
    shift = 0
    if rope is not None:
        assert nj == 1
        tabs, shift = rope
        groups, npos = tabs.shape[1], tabs.shape[2]
        npb = npos // tm
        for t in range(3):
            in_specs.append(pl.BlockSpec((groups, tm, LANES), lambda j, i: (0, i % npb, 0)))
            args.append(tabs[t])
    if seq_tiles:
        out_shape = jax.ShapeDtypeStruct((m // (seq_tiles * tm), ncols, seq_tiles * tm), out_dtype)
        out_spec = pl.BlockSpec((1, tn, tm), lambda j, i: (i // seq_tiles, j, i % seq_tiles))
    else:
        out_shape = jax.ShapeDtypeStruct((m, ncols), out_dtype)
        out_spec = pl.BlockSpec((tm, tn), lambda j, i: (i, j))
    return pl.pallas_call(
        functools.partial(_proj_kernel, shift=shift, tn=tn, seq_tiles=seq_tiles),
        out_shape=out_shape,
        grid=(nj, m // tm),
        in_specs=in_specs,
        out_specs=out_spec,
        scratch_shapes=[pltpu.VMEM((d, tn), BF16)],
        compiler_params=_params(("arbitrary", "arbitrary")),
        name="in_proj",
    )(*args)


def _rope_tables(pos, rot_dim, period, theta, scale):
    half = rot_dim // 2
    inv = (1.0 / np.power(np.float32(theta), np.arange(half, dtype=np.float32)
                          * np.float32(2.0 / rot_dim))).astype(np.float32)
    ang = (pos.astype(np.float32)[:, None] * inv[None, :]).astype(np.float64)
    cos, sin = np.cos(ang), np.sin(ang)
    npos = pos.shape[0]
    pad = period - rot_dim
    c = np.concatenate([cos, cos, np.ones((npos, pad))], axis=1)
    s1 = np.concatenate([-sin, np.zeros((npos, half + pad))], axis=1)
    s2 = np.concatenate([np.zeros((npos, half)), sin, np.zeros((npos, pad))], axis=1)
    tabs = np.stack([c, s1, s2]).astype(np.float32) * np.float32(scale)
    return np.tile(tabs, (1, 1, LANES // period)).astype(np.float32)


def _attn_prompt_kernel(lam_ref, q_ref, k_ref, v_ref, g_ref, o_ref,
                        kb_ref, vt_ref, m_ref, l_ref, acc_ref, *, tq, hp, out_scale):
    qi = pl.program_id(2)
    nblk = kb_ref.shape[1]

    @pl.when(qi == 0)
    def _():
        for c in range(nblk):
            for h in range(hp):
                kb_ref[h, c] = k_ref[0, h * LANES:(h + 1) * LANES,
                                     c * tq:(c + 1) * tq].T.astype(BF16)

        def stage(c, carry):
            off = pl.multiple_of(c * tq, tq)
            for h in range(hp):
                cols = slice(h * LANES, (h + 1) * LANES)
                vt_ref[h, c] = v_ref[pl.ds(off, tq), cols].T.astype(BF16)
            return carry
        lax.fori_loop(0, nblk, stage, 0)

    row = lax.broadcasted_iota(I32, (LANES, tq), 0)
    qqt = []
    for h in range(hp):
        qt = q_ref[:, h * LANES:(h + 1) * LANES].astype(F32).T
        qqt.append(jnp.concatenate([jnp.where(row < DA_HD, qt, 0.0),
                                    jnp.where(row >= DA_HD, qt, 0.0)], axis=1).astype(BF16))
    m_ref[...] = jnp.full(m_ref.shape, NEG, F32)
    l_ref[...] = jnp.zeros(l_ref.shape, F32)
    acc_ref[...] = jnp.zeros(acc_ref.shape, F32)

    def step(j, masked):
        sts = [jnp.dot(kb_ref[h, j], qqt[h], preferred_element_type=F32)
               for h in range(hp)]
        for h in range(hp):
            st = sts[h]
            if masked:
                key = lax.broadcasted_iota(I32, (tq, 2 * tq), 0)
                qry = lax.broadcasted_iota(I32, (tq, 2 * tq), 1) & (tq - 1)
                st = jnp.where(key <= qry, st, NEG)
            m_old = m_ref[h]
            m_new = jnp.maximum(m_old, jnp.max(st, axis=0, keepdims=True))
            alpha = jnp.exp2(m_old - m_new)
            p = jnp.exp2(st - m_new)
            l_ref[h] = alpha * l_ref[h] + jnp.sum(p, axis=0, keepdims=True)
            acc_ref[h] = alpha * acc_ref[h] + jnp.dot(vt_ref[h, j], p.astype(BF16),
                                                      preferred_element_type=F32)
            m_ref[h] = m_new

    def body(j, carry):
        step(j, False)
        return carry

    lax.fori_loop(0, qi, body, 0)
    step(qi, True)

    for h in range(hp):
        cols = slice(h * LANES, (h + 1) * LANES)
        o = acc_ref[h] / l_ref[h]
        out = (o[:, :tq] - lam_ref[0] * o[:, tq:]).T
        ms = jnp.mean(out * out, axis=-1, keepdims=True)
        out = out * lax.rsqrt(ms + EPS) * g_ref[:, cols] * out_scale
        o_ref[:, cols] = out.astype(o_ref.dtype)


def _attn_prompt(lam, q, k, v, g, b, s, out_scale, tq, hp):
    nq = s // tq
    w = hp * LANES
    return pl.pallas_call(
        functools.partial(_attn_prompt_kernel, tq=tq, hp=hp, out_scale=out_scale),
        out_shape=jax.ShapeDtypeStruct((b * s, DA_HEADS * DA_VD), BF16),
        grid=(b, DA_HEADS // hp, nq),
        in_specs=[pl.BlockSpec(memory_space=pltpu.SMEM),
                  pl.BlockSpec((tq, w), lambda bi, h, qi: (bi * nq + qi, h)),
                  pl.BlockSpec((1, w, s), lambda bi, h, qi: (bi, h, 0)),
                  pl.BlockSpec((s, w), lambda bi, h, qi: (bi, h)),
                  pl.BlockSpec((1, w), lambda bi, h, qi: (0, h))],
        out_specs=pl.BlockSpec((tq, w), lambda bi, h, qi: (bi * nq + qi, h)),
        scratch_shapes=[pltpu.VMEM((hp, nq, tq, LANES), BF16),
                        pltpu.VMEM((hp, nq, LANES, tq), BF16),
                        pltpu.VMEM((hp, 1, 2 * tq), F32),
                        pltpu.VMEM((hp, 1, 2 * tq), F32),
                        pltpu.VMEM((hp, LANES, 2 * tq), F32)],
        compiler_params=_params(("arbitrary", "arbitrary", "arbitrary")),
        name="attn_prompt",
    )(lam, q, k, v, g)


def _attn_paged_kernel(pt_ref, lam_ref, q_ref, kn_ref, vn_ref, g_ref, *rest,
                       npg, t, out_scale):
    k_refs = rest[:npg]
    v_refs = rest[npg:2 * npg]
    o_ref, m_ref, l_ref, acc_ref = rest[2 * npg:]
    c = pl.program_id(1)
    nc = pl.num_programs(1)
    rows = 2 * t
    page = k_refs[0].shape[2]

    @pl.when(c == 0)
    def _():
        m_ref[...] = jnp.full(m_ref.shape, NEG, F32)
        l_ref[...] = jnp.zeros(l_ref.shape, F32)
        acc_ref[...] = jnp.zeros(acc_ref.shape, F32)

    def update(scores, values):
        s_all = jnp.concatenate(scores, axis=0)
        m_old = m_ref[...]
        m_new = jnp.maximum(m_old, jnp.max(s_all, axis=-1, keepdims=True))
        alpha = jnp.exp(m_old - m_new)
        p = jnp.exp(s_all - m_new)
        l_ref[...] = alpha * l_ref[...] + jnp.sum(p, axis=-1, keepdims=True)
        m_ref[...] = m_new
        pb = p.astype(BF16)
        for h in range(DA_HEADS):
            r0 = h * rows
            pv = None
            for lo, hi, load_v in values[h]:
                d = jnp.dot(pb[r0:r0 + rows, lo:hi], load_v(), preferred_element_type=F32)
                pv = d if pv is None else pv + d
            acc_ref[r0:r0 + rows, :] = alpha[r0:r0 + rows] * acc_ref[r0:r0 + rows, :] + pv

    q = q_ref[0].astype(BF16)
    scores, values = [], []
    for h in range(DA_HEADS):
        qh = q[h * rows:(h + 1) * rows]
        sh, vh = [], []
        for i in range(npg):
            kb = k_refs[i][0, h * LANES:(h + 1) * LANES, :].astype(BF16)
            sh.append(jnp.dot(qh, kb, preferred_element_type=F32))
            vh.append((i * page, (i + 1) * page,
                       lambda i=i, h=h: v_refs[i][0, pl.ds(h, page, stride=DA_HEADS), :].astype(BF16)))
        scores.append(jnp.concatenate(sh, axis=1))
        values.append(vh)
    update(scores, values)

    @pl.when(c == nc - 1)
    def _():
        tp = kn_ref.shape[1]
        scores, values = [], []
        row = lax.broadcasted_iota(I32, (rows, tp), 0) & (t - 1)
        col = lax.broadcasted_iota(I32, (rows, tp), 1)
        for h in range(DA_HEADS):
            qh = q[h * rows:(h + 1) * rows]
            kb = kn_ref[0, :, h * LANES:(h + 1) * LANES].astype(BF16)
            s = lax.dot_general(qh, kb, (((1,), (1,)), ((), ())), preferred_element_type=F32)
            scores.append(jnp.where(col <= row, s, NEG))
            values.append([(0, tp, lambda h=h: vn_ref[0, :, h * LANES:(h + 1) * LANES].astype(BF16))])
        update(scores, values)
        o = acc_ref[...] / l_ref[...]
        lam = lam_ref[0]
        for h in range(DA_HEADS):
            r0 = h * rows
            out = o[r0:r0 + t] - lam * o[r0 + t:r0 + rows]
            ms = jnp.mean(out * out, axis=-1, keepdims=True)
            gh = g_ref[:, h * LANES:(h + 1) * LANES]
            o_ref[0, :, h * LANES:(h + 1) * LANES] = (
                out * lax.rsqrt(ms + EPS) * gh * out_scale).astype(o_ref.dtype)


def _attn_paged(page_table, lam, qm, k_new, v_new, g, cache_k, cache_v, page0, t, out_scale, npg):
    db, n_pages = page_table.shape
    prow, pcol = cache_k.shape[1], cache_k.shape[2]
    assert cache_v.shape[1:] == (prow, pcol)
    w = k_new.shape[2]
    tp = k_new.shape[1]
    nc = n_pages // npg
    pt_flat = page_table.reshape(-1) + page0

    def page_spec(i):
        return pl.BlockSpec((1, prow, pcol),
                            lambda b, c, pt: (pt[b * n_pages + c * npg + i], 0, 0))

    in_specs = [pl.BlockSpec(memory_space=pltpu.SMEM),
                pl.BlockSpec((1, qm.shape[1], LANES), lambda b, c, pt: (b, 0, 0)),
                pl.BlockSpec((1, tp, w), lambda b, c, pt: (b, 0, 0)),
                pl.BlockSpec((1, tp, w), lambda b, c, pt: (b, 0, 0)),
                pl.BlockSpec((1, w), lambda b, c, pt: (0, 0))]
    in_specs += [page_spec(i) for i in range(npg)] * 2
    rows = DA_HEADS * 2 * t
    return pl.pallas_call(
        functools.partial(_attn_paged_kernel, npg=npg, t=t, out_scale=out_scale),
        out_shape=jax.ShapeDtypeStruct((db, t, w), BF16),
        grid_spec=pltpu.PrefetchScalarGridSpec(
            num_scalar_prefetch=1,
            grid=(db, nc),
            in_specs=in_specs,
            out_specs=pl.BlockSpec((1, t, w), lambda b, c, pt: (b, 0, 0)),
            scratch_shapes=[pltpu.VMEM((rows, 1), F32),
                            pltpu.VMEM((rows, 1), F32),
                            pltpu.VMEM((rows, LANES), F32)]),
        compiler_params=_params(("arbitrary", "arbitrary")),
        name="attn_paged",
    )(pt_flat, lam, qm, k_new, v_new, g, *([cache_k] * npg), *([cache_v] * npg))


def _ret_kernel(rq_ref, rk_ref, rv_ref, rg_ref, s0_ref, dmat_ref, dq_ref, dk_ref, gc_ref, g_ref,
                o_ref, sout_ref, st_ref):
    c = pl.program_id(1)

    @pl.when(c == 0)
    def _():
        st_ref[...] = s0_ref[0]

    for h in range(RET_HEADS):
        kq = slice(h * RET_KD, (h + 1) * RET_KD)
        vs = slice(h * RET_VD, (h + 1) * RET_VD)
        qb = rq_ref[0, :, kq].astype(BF16)
        k = rk_ref[0, :, kq]
        vb = rv_ref[0, :, vs].astype(BF16)
        st = st_ref[h]
        att = lax.dot_general(qb, k.astype(BF16), (((1,), (1,)), ((), ())),
                              preferred_element_type=F32) * dmat_ref[h]
        inner = jnp.dot(att.astype(BF16), vb, preferred_element_type=F32)
        cross = jnp.dot(qb, st.astype(BF16), preferred_element_type=F32) * dq_ref[h]
        r = inner + cross
        kd = (k * dk_ref[h]).astype(BF16)
        st_ref[h] = gc_ref[h] * st + lax.dot_general(
            kd, vb, (((0,), (0,)), ((), ())), preferred_element_type=F32)
        mu = jnp.mean(r, axis=-1, keepdims=True)
        xc = r - mu
        var = jnp.mean(xc * xc, axis=-1, keepdims=True)
        y = xc * lax.rsqrt(var + EPS) * g_ref[:, vs]
        rg = rg_ref[0, :, vs]
        o_ref[0, :, vs] = (rg * jax.nn.sigmoid(rg) * y).astype(o_ref.dtype)

    @pl.when(c == pl.num_programs(1) - 1)
    def _():
        sout_ref[0] = st_ref[...]


def _ret_tables(log_g, chunk, valid):
    f32 = np.float32
    idx = np.arange(chunk, dtype=f32)
    diff = idx[:, None] - idx[None, :]
    ex = lambda e: np.exp(e.astype(f32).astype(np.float64)).astype(f32)
    dmat = np.where(diff[None] >= 0, ex(np.maximum(diff, f32(0))[None] * log_g[:, None, None]),
                    f32(0))
    dq = ex((idx + f32(1))[None, :] * log_g[:, None])
    dk = np.where(idx[None, :] < valid,
                  ex((f32(valid - 1) - idx)[None, :] * log_g[:, None]), f32(0))
    gc = ex(f32(valid) * log_g)
    h = log_g.shape[0]
    return (dmat.astype(f32),
            np.ascontiguousarray(np.broadcast_to(dq[:, :, None], (h, chunk, RET_VD))).astype(f32),
            np.ascontiguousarray(np.broadcast_to(dk[:, :, None], (h, chunk, RET_KD))).astype(f32),
            np.ascontiguousarray(np.broadcast_to(gc[:, None, None], (h, 1, RET_VD))).astype(f32))


def _retention(rqk, rv, gates, state0, g, log_g, chunk, valid):
    b, s, _ = rqk.shape
    nc = s // chunk
    dmat, dq, dk, gc = _ret_tables(log_g, chunk, valid)
    qw = RET_HEADS * RET_KD
    vw = RET_HEADS * RET_VD
    full = lambda shp: pl.BlockSpec(shp, lambda bi, c: (0,) * len(shp))
    return pl.pallas_call(
        _ret_kernel,
        out_shape=(jax.ShapeDtypeStruct((b, s, vw), BF16),
                   jax.ShapeDtypeStruct((b, RET_HEADS, RET_KD, RET_VD), F32)),
        grid=(b, nc),
        in_specs=[pl.BlockSpec((1, chunk, qw), lambda bi, c: (bi, c, 0)),
                  pl.BlockSpec((1, chunk, qw), lambda bi, c: (bi, c, 1)),
                  pl.BlockSpec((1, chunk, vw), lambda bi, c: (bi, c, 0)),
                  pl.BlockSpec((1, chunk, vw), lambda bi, c: (bi, c, 0)),
                  pl.BlockSpec((1, RET_HEADS, RET_KD, RET_VD), lambda bi, c: (bi, 0, 0, 0)),
                  full(dmat.shape), full(dq.shape), full(dk.shape), full(gc.shape),
                  full((1, vw))],
        out_specs=(pl.BlockSpec((1, chunk, vw), lambda bi, c: (bi, c, 0)),
                   pl.BlockSpec((1, RET_HEADS, RET_KD, RET_VD), lambda bi, c: (bi, 0, 0, 0))),
        scratch_shapes=[pltpu.VMEM((RET_HEADS, RET_KD, RET_VD), F32)],
        compiler_params=_params(("arbitrary", "arbitrary")),
        name="retention",
    )(rqk, rqk, rv, gates, state0, dmat, dq, dk, gc, g.reshape(1, vw))


def _finish_kernel(a_ref, r_ref, ga_ref, gr_ref, h_ref, wa_ref, wr_ref, wo_ref, gffn_ref,
                   wrt_ref, brt_ref, carry0_ref,
                   h1_ref, hn_ref, route_ref, cnt_ref, carry_ref, *, tm):
    i = pl.program_id(0)

    @pl.when(i == 0)
    def _():
        carry_ref[...] = carry0_ref[...]

    am = jnp.dot(a_ref[...], wa_ref[...], preferred_element_type=F32)
    rm = jnp.dot(r_ref[...], wr_ref[...], preferred_element_type=F32)
    mix = jax.nn.sigmoid(ga_ref[...]) * am + jax.nn.sigmoid(gr_ref[...]) * rm
    h1 = h_ref[...] + jnp.dot(mix.astype(BF16), wo_ref[...], preferred_element_type=F32)
    h1_ref[...] = h1
    ms = jnp.mean(h1 * h1, axis=-1, keepdims=True)
    hn = h1 * lax.rsqrt(ms + EPS) * gffn_ref[...]
    _store_slabs(hn_ref, 0, hn)
    logits = jnp.dot(hn.astype(BF16), wrt_ref[...], preferred_element_type=F32) + brt_ref[...]
    lane = lax.broadcasted_iota(I32, (tm, LANES), 1)
    is_g = (lane >= N_EXPERTS) & (lane < N_EXPERTS + N_GROUPS)
    glog = jnp.where(is_g, logits, -jnp.inf)
    gmax = jnp.max(glog, axis=-1, keepdims=True)
    gidx = jnp.min(jnp.where(glog == gmax, lane - N_EXPERTS, LANES), axis=-1, keepdims=True)
    gden = jnp.sum(jnp.where(is_g, jnp.exp(glog - gmax), 0.0), axis=-1, keepdims=True)
    gp = 1.0 / gden
    in_g = (lane < N_EXPERTS) & ((lane // EXPERTS_PER_GROUP) == gidx)
    e1 = jnp.where(in_g, logits, -jnp.inf)
    v1 = jnp.max(e1, axis=-1, keepdims=True)
    i1 = jnp.min(jnp.where(e1 == v1, lane, LANES), axis=-1, keepdims=True)
    e2 = jnp.where(lane == i1, -jnp.inf, e1)
    v2 = jnp.max(e2, axis=-1, keepdims=True)
    i2 = jnp.min(jnp.where(e2 == v2, lane, LANES), axis=-1, keepdims=True)
    tt = jnp.exp(v2 - v1)
    w1 = gp / (1.0 + tt)
    w2 = gp * tt / (1.0 + tt)
    oh = jnp.where((lane == i1) | (lane == i2), 1.0, 0.0)
    rr = lax.broadcasted_iota(I32, (tm, tm), 0)
    cc = lax.broadcasted_iota(I32, (tm, tm), 1)
    tri = jnp.where(cc < rr, 1.0, 0.0).astype(BF16)
    before = jnp.dot(tri, oh.astype(BF16), preferred_element_type=F32) + carry_ref[...]
    rank1 = jnp.sum(jnp.where(lane == i1, before, 0.0), axis=-1, keepdims=True)
    rank2 = jnp.sum(jnp.where(lane == i2, before, 0.0), axis=-1, keepdims=True)
    carry = carry_ref[...] + jnp.sum(oh, axis=0, keepdims=True)
    carry_ref[...] = carry
    cnt_ref[...] = carry
    cols = [i1.astype(F32), i2.astype(F32), w1, w2, rank1, rank2]
    route = jnp.zeros((tm, LANES), F32)
    for ci, val in enumerate(cols):
        route = jnp.where(lane == ci, val, route)
    route_ref[...] = route


def _finish(a, r, gagr, h, wa, wr, wo, gffn, wrt, brt, carry0, tm):
    m, d = h.shape
    aw = a.shape[1]
    assert gagr.shape[1] == 2 * d
    const = lambda shp: pl.BlockSpec(shp, lambda i: (0,) * len(shp))
    return pl.pallas_call(
        functools.partial(_finish_kernel, tm=tm),
        out_shape=(jax.ShapeDtypeStruct((m, d), F32),
                   jax.ShapeDtypeStruct((m * (d // LANES), LANES), F32),
                   jax.ShapeDtypeStruct((m, LANES), F32),
                   jax.ShapeDtypeStruct((1, LANES), F32)),
        grid=(m // tm,),
        in_specs=[pl.BlockSpec((tm, aw), lambda i: (i, 0)),
                  pl.BlockSpec((tm, aw), lambda i: (i, 0)),
                  pl.BlockSpec((tm, d), lambda i: (i, 0)),
                  pl.BlockSpec((tm, d), lambda i: (i, 1)),
                  pl.BlockSpec((tm, d), lambda i: (i, 0)),
                  const(wa.shape), const(wr.shape), const(wo.shape), const((1, d)),
                  const(wrt.shape), const((1, LANES)), const((1, LANES))],
        out_specs=(pl.BlockSpec((tm, d), lambda i: (i, 0)),
                   pl.BlockSpec((tm * (d // LANES), LANES), lambda i: (i, 0)),
                   pl.BlockSpec((tm, LANES), lambda i: (i, 0)),
                   pl.BlockSpec((1, LANES), lambda i: (0, 0))),
        scratch_shapes=[pltpu.VMEM((1, LANES), F32)],
        compiler_params=_params(("arbitrary",)),
        name="finish",
    )(a, r, gagr, gagr, h, wa, wr, wo, gffn.reshape(1, d), wrt, brt, carry0)


def _moe_kernel(item_e, item_start, item_n, row_src,
                hn_hbm, wg_ref, wu_ref, wd_ref, yk_hbm,
                xg, yacc, wgb, wub, wdb, gsem, ssem, *, sub):
    i = pl.program_id(0)
    f = pl.program_id(1)
    nf = pl.num_programs(1)
    n = item_n[i]
    start = item_start[i]
    d = yacc.shape[1]
    nch = d // LANES

    @pl.when((i == 0) & (f == 0))
    def _():
        xg[...] = jnp.zeros(xg.shape, xg.dtype)

    def gather_copy(r, tok):
        return pltpu.make_async_copy(hn_hbm.at[pl.ds(tok * nch, nch)],
                                     xg.at[pl.ds(r * nch, nch)], gsem)

    def scatter_copy(r, slot, tok):
        return pltpu.make_async_copy(xg.at[pl.ds(r * nch, nch)],
                                     yk_hbm.at[slot, pl.ds(tok * nch, nch)], ssem)

    @pl.when((f == 0) & (n > 0))
    def _():
        def issue(r, c):
            gather_copy(r, row_src[start + r] >> 1).start()
            return c
        lax.fori_loop(0, n, issue, 0)

        def wait(r, c):
            gather_copy(0, 0).wait()
            return c
        lax.fori_loop(0, n, wait, 0)

    @pl.when(n > 0)
    def _():
        wgb[...] = wg_ref[0].astype(BF16)
        wub[...] = wu_ref[0].astype(BF16)
        wdb[...] = wd_ref[0].astype(BF16)

        def run_rows(off, m):
            x = _load_slabs(xg, off, m, d).astype(BF16)
            hg = jnp.dot(x, wgb[...], preferred_element_type=F32)
            hu = jnp.dot(x, wub[...], preferred_element_type=F32)
            hm = (hg * jax.nn.sigmoid(hg) * hu).astype(BF16)
            part = jnp.dot(hm, wdb[...], preferred_element_type=F32)

            @pl.when(f == 0)
            def _():
                yacc[pl.ds(off, m), :] = part

            @pl.when((f > 0) & (f < nf - 1))
            def _():
                yacc[pl.ds(off, m), :] += part

            @pl.when(f == nf - 1)
            def _():
                _store_slabs(xg, off, yacc[pl.ds(off, m), :] + part)

        nsub = (n + sub - 1) // sub
        nbig = nsub // 4

        def big(s, c):
            run_rows(pl.multiple_of(s * (4 * sub), 4 * sub), 4 * sub)
            return c
        lax.fori_loop(0, nbig, big, 0)
        off2 = pl.multiple_of(nbig * (4 * sub), 2 * sub)

        @pl.when((nsub & 2) != 0)
        def _():
            run_rows(off2, 2 * sub)
        off1 = pl.multiple_of(off2 + (nsub & 2) * sub, sub)

        @pl.when((nsub & 1) != 0)
        def _():
            run_rows(off1, sub)

    @pl.when((f == nf - 1) & (n > 0))
    def _():
        def issue(r, c):
            src = row_src[start + r]
            scatter_copy(r, src & 1, src >> 1).start()
            return c
        lax.fori_loop(0, n, issue, 0)

        def wait(r, c):
            scatter_copy(0, 0, 0).wait()
            return c
        lax.fori_loop(0, n, wait, 0)


def _moe_plan(route, counts, rows_per_item, max_items):
    ntok = route.shape[0]
    eid = route[:, 0:2].astype(I32)
    rank = route[:, 4:6].astype(I32)
    counts = counts.astype(I32)
    ends = jnp.cumsum(counts)
    starts = ends - counts
    dest = (starts[eid] + rank).reshape(-1)
    src = jnp.arange(2 * ntok, dtype=I32)
    row_src = jnp.zeros((2 * ntok,), I32).at[dest].set(src)
    nit = (counts + rows_per_item - 1) // rows_per_item
    it_end = jnp.cumsum(nit)
    it_first = it_end - nit
    total = it_end[-1]
    t = jnp.arange(max_items, dtype=I32)
    e = jnp.minimum(jnp.searchsorted(it_end, t, side='right'), N_EXPERTS - 1).astype(I32)
    j = t - it_first[e]
    n = jnp.clip(counts[e] - j * rows_per_item, 0, rows_per_item)
    n = jnp.where(t < total, n, 0).astype(I32)
    start = (starts[e] + j * rows_per_item).astype(I32)
    start = jnp.where(n > 0, start, 0)
    last_e = e[jnp.maximum(total - 1, 0)]
    e = jnp.where(t < total, e, last_e)
    return e, start, n, row_src


def _moe(hn, route, counts, wg, wu, wd):
    ne, d, ff = wg.shape
    nch = d // LANES
    ntok = hn.shape[0] // nch
    rows, sub, nf = MOE_ROWS, MOE_SUB, MOE_FSPLIT
    assert nf > 1
    tf = ff // nf
    max_items = ne + (2 * ntok) // rows
    item_e, item_start, item_n, row_src = _moe_plan(route, counts, rows, max_items)

    def f_eff(i, f, item_n):
        return jnp.where(item_n[i] > 0, f, nf - 1)

    return pl.pallas_call(
        functools.partial(_moe_kernel, sub=sub),
        out_shape=jax.ShapeDtypeStruct((2, ntok * nch, LANES), F32),
        grid_spec=pltpu.PrefetchScalarGridSpec(
            num_scalar_prefetch=4,
            grid=(max_items, nf),
            in_specs=[pl.BlockSpec(memory_space=pl.ANY),
                      pl.BlockSpec((1, d, tf), lambda i, f, ie, ist, inn, rs: (ie[i], 0, f_eff(i, f, inn))),
                      pl.BlockSpec((1, d, tf), lambda i, f, ie, ist, inn, rs: (ie[i], 0, f_eff(i, f, inn))),
                      pl.BlockSpec((1, tf, d), lambda i, f, ie, ist, inn, rs: (ie[i], f_eff(i, f, inn), 0))],
            out_specs=pl.BlockSpec(memory_space=pl.ANY),
            scratch_shapes=[pltpu.VMEM((rows * nch, LANES), F32),
                            pltpu.VMEM((rows, d), F32),
                            pltpu.VMEM((d, tf), BF16),
                            pltpu.VMEM((d, tf), BF16),
                            pltpu.VMEM((tf, d), BF16),
                            pltpu.SemaphoreType.DMA,
                            pltpu.SemaphoreType.DMA]),
        compiler_params=_params(("arbitrary", "arbitrary")),
        name="moe_experts",
    )(item_e, item_start, item_n, row_src, hn, wg, wu, wd)


def _ple_kernel(h1_ref, y0_ref, y1_ref, route_ref, pe_ref, gple_ref, wg_ref, wp_ref, gfin_ref,
                o_ref):
    route = route_ref[...]
    tm, d = h1_ref.shape
    y = (route[:, 2:3] * _load_slabs(y0_ref.at[0], 0, tm, d)
         + route[:, 3:4] * _load_slabs(y1_ref.at[0], 0, tm, d))
    h2 = h1_ref[...] + y
    ms = jnp.mean(h2 * h2, axis=-1, keepdims=True)
    hn = (h2 * lax.rsqrt(ms + EPS) * gple_ref[...]).astype(BF16)
    gate = jax.nn.sigmoid(jnp.dot(hn, wg_ref[...], preferred_element_type=F32))
    pp = jnp.dot(pe_ref[...].astype(BF16), wp_ref[...], preferred_element_type=F32)
    h3 = h2 + gate * pp
    ms = jnp.mean(h3 * h3, axis=-1, keepdims=True)
    o_ref[...] = h3 * lax.rsqrt(ms + EPS) * gfin_ref[...]


def _ple(h1, yk, row0, route, pe, gple, wg, wp, gfin, tm):
    m, d = h1.shape
    blk0 = row0 // tm
    nch = d // LANES
    const = lambda shp: pl.BlockSpec(shp, lambda i: (0,) * len(shp))
    return pl.pallas_call(
        _ple_kernel,
        out_shape=jax.ShapeDtypeStruct((m, d), F32),
        grid=(m // tm,),
        in_specs=[pl.BlockSpec((tm, d), lambda i: (i, 0)),
                  pl.BlockSpec((1, tm * nch, LANES), lambda i: (0, i + blk0, 0)),
                  pl.BlockSpec((1, tm * nch, LANES), lambda i: (1, i + blk0, 0)),
                  pl.BlockSpec((tm, LANES), lambda i: (i, 0)),
                  pl.BlockSpec((tm, pe.shape[1]), lambda i: (i, 0)),
                  const((1, d)), const(wg.shape), const(wp.shape), const((1, d))],
        out_specs=pl.BlockSpec((tm, d), lambda i: (i, 0)),
        compiler_params=_params(("arbitrary",)),
        name="ple_final",
    )(h1, yk, yk, route, pe, gple.reshape(1, d), wg, wp, gfin.reshape(1, d))


def _project_group(x, g, w_in, pos, tm, q_scale, k_seq_tiles=0):
    xn = _rmsnorm(x, g, tm)
    da_q = _rope_tables(pos, DA_ROT, DA_HD, ROPE_THETA, q_scale)[:, None]
    da_k = _rope_tables(pos, DA_ROT, DA_HD, ROPE_THETA, 1.0)[:, None]
    rq_t = _rope_tables(pos, RET_KD, RET_KD, RET_THETA, 1.0)
    rk_t = _rope_tables(pos, RET_KD, RET_KD, RET_THETA, RET_KD ** -0.5)
    ret_t = np.stack([rq_t, rk_t], axis=1)
    q = _proj(xn, w_in, COL_Q, 1024, tm, BF16, rope=(da_q, DA_ROT // 2))
    k = _proj(xn, w_in, COL_K, 1024, tm, F32, rope=(da_k, DA_ROT // 2), seq_tiles=k_seq_tiles)
    v = _proj(xn, w_in, COL_V, 1024, tm, F32)
    rqk = _proj(xn, w_in, COL_RQK, 1024, tm, F32, rope=(ret_t, RET_KD // 2))
    rv = _proj(xn, w_in, COL_RV, 1024, tm, F32)
    rg = _proj(xn, w_in, COL_GATES, 1024, tm, F32)
    gagr = _proj(xn, w_in, COL_GATES + 1024, 4096, tm, F32)
    return q, k, v, rqk, rv, rg, gagr


def kernel(x_prompt, x_sample, cache_k, cache_v, state_ret, page_table, p_prompt, p_sample, norm_mix_g, w_in, lam_q1, lam_k1, lam_q2, lam_k2, da_norm_g, ret_norm_g, w_br_attn, w_br_ret, w_out, norm_ffn_g, w_router_group, b_router_group, w_router_expert, b_router_expert, w_exp_gate, w_exp_up, w_exp_down, norm_ple_g, w_ple_gate, w_ple_proj, final_norm_g):
    bp, sp, d = x_prompt.shape
    bs, ts, _ = x_sample.shape
    depth = w_in.shape[0]
    assert depth == 1, "the final norm is fused into the last stage of a single layer"
    n_pool, page = cache_k.shape[1], cache_k.shape[2]
    mp, msm = bp * sp, bs * ts
    log_g = np.log1p(-np.exp2(-5.0 - np.arange(RET_HEADS))).astype(np.float32)
    pos_p = np.arange(sp)
    pos_s = np.tile(PAST_LEN + np.arange(ts), bs)
    cache_k2 = cache_k.transpose(0, 1, 3, 4, 5, 2).reshape(depth * n_pool, DA_HEADS * 2 * DA_HD, page)
    cache_v2 = cache_v.reshape(depth * n_pool, page * DA_HEADS, DA_VD)
    tm_p = next(t for t in (1024, 512, 256, 128) if sp % t == 0)
    tf_p = 256

    hp = x_prompt.reshape(mp, d)
    hs = x_sample.reshape(msm, d)
    kp_l, vp_l, rp_l, ks_l, vs_l, rs_l = [], [], [], [], [], []
    for li in range(depth):
        lam_init = 0.8 - 0.6 * math.exp(-0.3 * li)
        lam = (jnp.exp(jnp.sum(lam_q1[li] * lam_k1[li]))
               - jnp.exp(jnp.sum(lam_q2[li] * lam_k2[li])) + lam_init).reshape(1).astype(F32)
        out_scale = 1.0 - lam_init
        wa = w_br_attn[li].astype(BF16)
        wr = w_br_ret[li].astype(BF16)
        wo = w_out[li].astype(BF16)
        wpg = w_ple_gate[li].astype(BF16)
        wpp = w_ple_proj[li].astype(BF16)
        wrt = jnp.zeros((d, LANES), F32)
        wrt = wrt.at[:, :N_EXPERTS].set(w_router_expert[li])
        wrt = wrt.at[:, N_EXPERTS:N_EXPERTS + N_GROUPS].set(w_router_group[li]).astype(BF16)
        brt = jnp.zeros((1, LANES), F32)
        brt = brt.at[0, :N_EXPERTS].set(b_router_expert[li])
        brt = brt.at[0, N_EXPERTS:N_EXPERTS + N_GROUPS].set(b_router_group[li])
        da_g = da_norm_g[li].reshape(1, -1)

        q, k, v, rqk, rv, rg, gagr = _project_group(hp, norm_mix_g[li], w_in[li], pos_p, tm_p,
                                                    DA_HD ** -0.5 * math.log2(math.e),
                                                    k_seq_tiles=sp // tm_p)
        a_p = _attn_prompt(lam, q, k, v, da_g, bp, sp, out_scale, tq=256, hp=4)
        chunk = 256 if sp % 256 == 0 else 128
        r_p, st_p = _retention(rqk.reshape(bp, sp, -1), rv.reshape(bp, sp, -1),
                               rg.reshape(bp, sp, -1),
                               jnp.zeros((bp, RET_HEADS, RET_KD, RET_VD), F32),
                               ret_norm_g[li], log_g, chunk, chunk)
        kp_l.append(k.reshape(bp, DA_HEADS, 2, DA_HD, sp).transpose(0, 4, 1, 2, 3))
        vp_l.append(v.reshape(bp, sp, DA_HEADS, DA_VD))
        rp_l.append(st_p)
        h1_p, hn_p, route_p, cnt_p = _finish(
            a_p, r_p.reshape(mp, -1), gagr, hp, wa, wr, wo, norm_ffn_g[li], wrt, brt,
            jnp.zeros((1, LANES), F32), tf_p)

        q, k, v, rqk, rv, rg, gagr = _project_group(hs, norm_mix_g[li], w_in[li], pos_s, msm,
                                                    DA_HD ** -0.5)
        tpad = 8
        pad_t = lambda z: jnp.pad(z.reshape(bs, ts, -1), ((0, 0), (0, tpad - ts), (0, 0)))
        q4 = q.astype(F32).reshape(bs, ts, DA_HEADS, 2, DA_HD)
        eye = jnp.eye(2, dtype=F32)
        qm = jnp.einsum('bthmd,mn->bhmtnd', q4, eye).reshape(bs, DA_HEADS * 2 * ts, 2 * DA_HD)
        a_s = _attn_paged(page_table, lam, qm, pad_t(k), pad_t(v), da_g, cache_k2, cache_v2,
                          li * n_pool, ts, out_scale, npg=8)
        r_s, st_s = _retention(pad_t(rqk), pad_t(rv), pad_t(rg), state_ret[li],
                               ret_norm_g[li], log_g, tpad, ts)
        ks_l.append(k.reshape(bs, ts, DA_HEADS, 2, DA_HD))
        vs_l.append(v.reshape(bs, ts, DA_HEADS, DA_VD))
        rs_l.append(st_s)
        h1_s, hn_s, route_s, cnt = _finish(
            a_s.reshape(msm, -1), r_s[:, :ts].reshape(msm, -1), gagr, hs, wa, wr, wo,
            norm_ffn_g[li], wrt, brt, cnt_p, msm)

        hn_all = jnp.concatenate([hn_p, hn_s], axis=0)
        route_all = jnp.concatenate([route_p, route_s], axis=0)
        yk = _moe(hn_all, route_all, cnt[0, :N_EXPERTS], w_exp_gate[li], w_exp_up[li],
                  w_exp_down[li])

        gfin = final_norm_g
        hp = _ple(h1_p, yk, 0, route_p, p_prompt[li].reshape(mp, -1), norm_ple_g[li], wpg, wpp,
                  gfin, tf_p)
        hs = _ple(h1_s, yk, mp, route_s, p_sample[li].reshape(msm, -1), norm_ple_g[li], wpg, wpp,
                  gfin, msm)
    y_prompt = hp.reshape(bp, sp, d)
    y_sample = hs.reshape(bs, ts, d)
    return (y_prompt, y_sample, jnp.stack(kp_l), jnp.stack(vp_l), jnp.stack(rp_l),
            jnp.stack(ks_l), jnp.stack(vs_l), jnp.stack(rs_l))
```

```python
import functools
import math

import jax
import jax.numpy as jnp
import numpy as np
from jax import lax
from jax.experimental import pallas as pl
from jax.experimental.pallas import tpu as pltpu

F32 = jnp.float32
BF16 = jnp.bfloat16
I32 = jnp.int32

PAST_LEN = 16384
DA_HEADS = 8
DA_VD = 128
DA_HD = 64
DA_ROT = 16
ROPE_THETA = 500000.0
RET_HEADS = 8
RET_VD = 128
RET_KD = 64
RET_THETA = 10000.0
N_GROUPS = 4
EXPERTS_PER_GROUP = 8
N_EXPERTS = N_GROUPS * EXPERTS_PER_GROUP
EPS = 1e-6
NEG = -1e30
LANES = 128

VMEM_LIMIT = 56 * 1024 * 1024

COL_Q, COL_K, COL_V, COL_RQK, COL_RV, COL_GATES = 0, 1024, 2048, 3072, 4096, 5120

MOE_ROWS = 768
MOE_SUB = 128
MOE_FSPLIT = 4


def _params(sem, vmem=VMEM_LIMIT):
    return pltpu.CompilerParams(dimension_semantics=sem, vmem_limit_bytes=vmem)


def _rmsnorm_kernel(x_ref, g_ref, o_ref):
    x = x_ref[...]
    ms = jnp.mean(x * x, axis=-1, keepdims=True)
    o_ref[...] = (x * lax.rsqrt(ms + EPS) * g_ref[...]).astype(o_ref.dtype)


def _rmsnorm(x, g, tm):
    m, d = x.shape
    return pl.pallas_call(
        _rmsnorm_kernel,
        out_shape=jax.ShapeDtypeStruct((m, d), BF16),
        grid=(m // tm,),
        in_specs=[pl.BlockSpec((tm, d), lambda i: (i, 0)),
                  pl.BlockSpec((1, d), lambda i: (0, 0))],
        out_specs=pl.BlockSpec((tm, d), lambda i: (i, 0)),
        compiler_params=_params(("arbitrary",)),
        name="rmsnorm",
    )(x, g.reshape(1, d))


def _proj_kernel(*refs, shift, tn, seq_tiles):
    if shift:
        xn_ref, w_ref, c_ref, s1_ref, s2_ref, o_ref, wbf_ref = refs
    else:
        xn_ref, w_ref, o_ref, wbf_ref = refs

    @pl.when(pl.program_id(1) == 0)
    def _():
        wbf_ref[...] = w_ref[...].astype(BF16)

    acc = jnp.dot(xn_ref[...], wbf_ref[...], preferred_element_type=F32)
    if shift:
        groups = c_ref.shape[0]
        rep = tn // LANES // groups
        wide = lambda ref: jnp.concatenate(
            [jnp.tile(ref[g], (1, rep)) for g in range(groups)], axis=1)
        acc = (acc * wide(c_ref) + pltpu.roll(acc, tn - shift, 1) * wide(s1_ref)
               + pltpu.roll(acc, shift, 1) * wide(s2_ref))
    if seq_tiles:
        o_ref[0] = acc.T.astype(o_ref.dtype)
    else:
        o_ref[...] = acc.astype(o_ref.dtype)


def _proj(xn, w_in, col0, ncols, tm, out_dtype, rope=None, seq_tiles=0):
    m, d = xn.shape
    tn = 1024
    nj = ncols // tn
    j0 = col0 // tn
    in_specs = [pl.BlockSpec((tm, d), lambda j, i: (i, 0)),
                pl.BlockSpec((d, tn), lambda j, i: (0, j + j0))]
    args = [xn, w_in]
    shift = 0
    if rope is not None:
        assert nj == 1
        tabs, shift = rope
        groups, npos = tabs.shape[1], tabs.shape[2]
        npb = npos // tm
        for t in range(3):
            in_specs.append(pl.BlockSpec((groups, tm, LANES), lambda j, i: (0, i % npb, 0)))
            args.append(tabs[t])
    if seq_tiles:
        out_shape = jax.ShapeDtypeStruct((m // (seq_tiles * tm), ncols, seq_tiles * tm), out_dtype)
        out_spec = pl.BlockSpec((1, tn, tm), lambda j, i: (i // seq_tiles, j, i % seq_tiles))
    else:
        out_shape = jax.ShapeDtypeStruct((m, ncols), out_dtype)
        out_spec = pl.BlockSpec((tm, tn), lambda j, i: (i, j))
    return pl.pallas_call(
        functools.partial(_proj_kernel, shift=shift, tn=tn, seq_tiles=seq_tiles),
        out_shape=out_shape,
        grid=(nj, m // tm),
        in_specs=in_specs,
        out_specs=out_spec,
        scratch_shapes=[pltpu.VMEM((d, tn), BF16)],
        compiler_params=_params(("arbitrary", "arbitrary")),
        name="in_proj",
    )(*args)


def _rope_tables(pos, rot_dim, period, theta, scale):
    half = rot_dim // 2
    inv = (1.0 / np.power(np.float32(theta), np.arange(half, dtype=np.float32)
                          * np.float32(2.0 / rot_dim))).astype(np.float32)
    ang = (pos.astype(np.float32)[:, None] * inv[None, :]).astype(np.float64)
    cos, sin = np.cos(ang), np.sin(ang)
    npos = pos.shape[0]
    pad = period - rot_dim
    c = np.concatenate([cos, cos, np.ones((npos, pad))], axis=1)
    s1 = np.concatenate([-sin, np.zeros((npos, half + pad))], axis=1)
    s2 = np.concatenate([np.zeros((npos, half)), sin, np.zeros((npos, pad))], axis=1)
    tabs = np.stack([c, s1, s2]).astype(np.float32) * np.float32(scale)
    return np.tile(tabs, (1, 1, LANES // period)).astype(np.float32)


def _attn_prompt_kernel(lam_ref, q_ref, k_ref, v_ref, g_ref, o_ref,
                        kb_ref, vt_ref, m_ref, l_ref, acc_ref, *, tq, hp, out_scale):
    qi = pl.program_id(2)
    nblk = kb_ref.shape[1]

    @pl.when(qi == 0)
    def _():
        for c in range(nblk):
            for h in range(hp):
                kb_ref[h, c] = k_ref[0, h * LANES:(h + 1) * LANES,
                                     c * tq:(c + 1) * tq].T.astype(BF16)

        def stage(c, carry):
            off = pl.multiple_of(c * tq, tq)
            for h in range(hp):
                cols = slice(h * LANES, (h + 1) * LANES)
                vt_ref[h, c] = v_ref[pl.ds(off, tq), cols].T.astype(BF16)
            return carry
        lax.fori_loop(0, nblk, stage, 0)

    row = lax.broadcasted_iota(I32, (LANES, tq), 0)
    qqt = []
    for h in range(hp):
        qt = q_ref[:, h * LANES:(h + 1) * LANES].astype(F32).T
        qqt.append(jnp.concatenate([jnp.where(row < DA_HD, qt, 0.0),
                                    jnp.where(row >= DA_HD, qt, 0.0)], axis=1).astype(BF16))
    m_ref[...] = jnp.full(m_ref.shape, NEG, F32)
    l_ref[...] = jnp.zeros(l_ref.shape, F32)
    acc_ref[...] = jnp.zeros(acc_ref.shape, F32)

    def step(j, masked):
        sts = [jnp.dot(kb_ref[h, j], qqt[h], preferred_element_type=F32)
               for h in range(hp)]
        for h in range(hp):
            st = sts[h]
            if masked:
                key = lax.broadcasted_iota(I32, (tq, 2 * tq), 0)
                qry = lax.broadcasted_iota(I32, (tq, 2 * tq), 1) & (tq - 1)
                st = jnp.where(key <= qry, st, NEG)
            m_old = m_ref[h]
            m_new = jnp.maximum(m_old, jnp.max(st, axis=0, keepdims=True))
            alpha = jnp.exp2(m_old - m_new)
            p = jnp.exp2(st - m_new)
            l_ref[h] = alpha * l_ref[h] + jnp.sum(p, axis=0, keepdims=True)
            acc_ref[h] = alpha * acc_ref[h] + jnp.dot(vt_ref[h, j], p.astype(BF16),
                                                      preferred_element_type=F32)
            m_ref[h] = m_new

    def body(j, carry):
        step(j, False)
        return carry

    lax.fori_loop(0, qi, body, 0)
    step(qi, True)

    for h in range(hp):
        cols = slice(h * LANES, (h + 1) * LANES)
        o = acc_ref[h] / l_ref[h]
        out = (o[:, :tq] - lam_ref[0] * o[:, tq:]).T
        ms = jnp.mean(out * out, axis=-1, keepdims=True)
        out = out * lax.rsqrt(ms + EPS) * g_ref[:, cols] * out_scale
        o_ref[:, cols] = out.astype(o_ref.dtype)


def _attn_prompt(lam, q, k, v, g, b, s, out_scale, tq, hp):
    nq = s // tq
    w = hp * LANES
    return pl.pallas_call(
        functools.partial(_attn_prompt_kernel, tq=tq, hp=hp, out_scale=out_scale),
        out_shape=jax.ShapeDtypeStruct((b * s, DA_HEADS * DA_VD), BF16),
        grid=(b, DA_HEADS // hp, nq),
        in_specs=[pl.BlockSpec(memory_space=pltpu.SMEM),
                  pl.BlockSpec((tq, w), lambda bi, h, qi: (bi * nq + qi, h)),
                  pl.BlockSpec((1, w, s), lambda bi, h, qi: (bi, h, 0)),
                  pl.BlockSpec((s, w), lambda bi, h, qi: (bi, h)),
                  pl.BlockSpec((1, w), lambda bi, h, qi: (0, h))],
        out_specs=pl.BlockSpec((tq, w), lambda bi, h, qi: (bi * nq + qi, h)),
        scratch_shapes=[pltpu.VMEM((hp, nq, tq, LANES), BF16),
                        pltpu.VMEM((hp, nq, LANES, tq), BF16),
                        pltpu.VMEM((hp, 1, 2 * tq), F32),
                        pltpu.VMEM((hp, 1, 2 * tq), F32),
                        pltpu.VMEM((hp, LANES, 2 * tq), F32)],
        compiler_params=_params(("arbitrary", "arbitrary", "arbitrary")),
        name="attn_prompt",
    )(lam, q, k, v, g)


def _attn_paged_kernel(pt_ref, lam_ref, q_ref, kn_ref, vn_ref, g_ref, *rest,
                       npg, t, out_scale):
    k_refs = rest[:npg]
    v_refs = rest[npg:2 * npg]
    o_ref, m_ref, l_ref, acc_ref = rest[2 * npg:]
    c = pl.program_id(1)
    nc = pl.num_programs(1)
    rows = 2 * t
    page = k_refs[0].shape[2]

    @pl.when(c == 0)
    def _():
        m_ref[...] = jnp.full(m_ref.shape, NEG, F32)
        l_ref[...] = jnp.zeros(l_ref.shape, F32)
        acc_ref[...] = jnp.zeros(acc_ref.shape, F32)

    def update(scores, values):
        s_all = jnp.concatenate(scores, axis=0)
        m_old = m_ref[...]
        m_new = jnp.maximum(m_old, jnp.max(s_all, axis=-1, keepdims=True))
        alpha = jnp.exp(m_old - m_new)
        p = jnp.exp(s_all - m_new)
        l_ref[...] = alpha * l_ref[...] + jnp.sum(p, axis=-1, keepdims=True)
        m_ref[...] = m_new
        pb = p.astype(BF16)
        for h in range(DA_HEADS):
            r0 = h * rows
            pv = None
            for lo, hi, load_v in values[h]:
                d = jnp.dot(pb[r0:r0 + rows, lo:hi], load_v(), preferred_element_type=F32)
                pv = d if pv is None else pv + d
            acc_ref[r0:r0 + rows, :] = alpha[r0:r0 + rows] * acc_ref[r0:r0 + rows, :] + pv

    q = q_ref[0].astype(BF16)
    scores, values = [], []
    for h in range(DA_HEADS):
        qh = q[h * rows:(h + 1) * rows]
        sh, vh = [], []
        for i in range(npg):
            kb = k_refs[i][0, h * LANES:(h + 1) * LANES, :].astype(BF16)
            sh.append(jnp.dot(qh, kb, preferred_element_type=F32))
            vh.append((i * page, (i + 1) * page,
                       lambda i=i, h=h: v_refs[i][0, pl.ds(h, page, stride=DA_HEADS), :].astype(BF16)))
        scores.append(jnp.concatenate(sh, axis=1))
        values.append(vh)
    update(scores, values)

    @pl.when(c == nc - 1)
    def _():
        tp = kn_ref.shape[1]
        scores, values = [], []
        row = lax.broadcasted_iota(I32, (rows, tp), 0) & (t - 1)
        col = lax.broadcasted_iota(I32, (rows, tp), 1)
        for h in range(DA_HEADS):
            qh = q[h * rows:(h + 1) * rows]
            kb = kn_ref[0, :, h * LANES:(h + 1) * LANES].astype(BF16)
            s = lax.dot_general(qh, kb, (((1,), (1,)), ((), ())), preferred_element_type=F32)
            scores.append(jnp.where(col <= row, s, NEG))
            values.append([(0, tp, lambda h=h: vn_ref[0, :, h * LANES:(h + 1) * LANES].astype(BF16))])
        update(scores, values)
        o = acc_ref[...] / l_ref[...]
        lam = lam_ref[0]
        for h in range(DA_HEADS):
            r0 = h * rows
            out = o[r0:r0 + t] - lam * o[r0 + t:r0 + rows]
            ms = jnp.mean(out * out, axis=-1, keepdims=True)
            gh = g_ref[:, h * LANES:(h + 1) * LANES]
            o_ref[0, :, h * LANES:(h + 1) * LANES] = (
                out * lax.rsqrt(ms + EPS) * gh * out_scale).astype(o_ref.dtype)


def _attn_paged(page_table, lam, qm, k_new, v_new, g, cache_k, cache_v, page0, t, out_scale, npg):
    db, n_pages = page_table.shape
    prow, pcol = cache_k.shape[1], cache_k.shape[2]
    assert cache_v.shape[1:] == (prow, pcol)
    w = k_new.shape[2]
    tp = k_new.shape[1]
    nc = n_pages // npg
    pt_flat = page_table.reshape(-1) + page0

    def page_spec(i):
        return pl.BlockSpec((1, prow, pcol),
                            lambda b, c, pt: (pt[b * n_pages + c * npg + i], 0, 0))

    in_specs = [pl.BlockSpec(memory_space=pltpu.SMEM),
                pl.BlockSpec((1, qm.shape[1], LANES), lambda b, c, pt: (b, 0, 0)),
                pl.BlockSpec((1, tp, w), lambda b, c, pt: (b, 0, 0)),
                pl.BlockSpec((1, tp, w), lambda b, c, pt: (b, 0, 0)),
                pl.BlockSpec((1, w), lambda b, c, pt: (0, 0))]
    in_specs += [page_spec(i) for i in range(npg)] * 2
    rows = DA_HEADS * 2 * t
    return pl.pallas_call(
        functools.partial(_attn_paged_kernel, npg=npg, t=t, out_scale=out_scale),
        out_shape=jax.ShapeDtypeStruct((db, t, w), BF16),
        grid_spec=pltpu.PrefetchScalarGridSpec(
            num_scalar_prefetch=1,
            grid=(db, nc),
            in_specs=in_specs,
            out_specs=pl.BlockSpec((1, t, w), lambda b, c, pt: (b, 0, 0)),
            scratch_shapes=[pltpu.VMEM((rows, 1), F32),
                            pltpu.VMEM((rows, 1), F32),
                            pltpu.VMEM((rows, LANES), F32)]),
        compiler_params=_params(("arbitrary", "arbitrary")),
        name="attn_paged",
    )(pt_flat, lam, qm, k_new, v_new, g, *([cache_k] * npg), *([cache_v] * npg))


def _ret_kernel(rq_ref, rk_ref, rv_ref, rg_ref, s0_ref, dmat_ref, dq_ref, dk_ref, gc_ref, g_ref,
                o_ref, sout_ref, st_ref):
    c = pl.program_id(1)

    @pl.when(c == 0)
    def _():
        st_ref[...] = s0_ref[0]

    for h in range(RET_HEADS):
        kq = slice(h * RET_KD, (h + 1) * RET_KD)
        vs = slice(h * RET_VD, (h + 1) * RET_VD)
        qb = rq_ref[0, :, kq].astype(BF16)
        k = rk_ref[0, :, kq]
        vb = rv_ref[0, :, vs].astype(BF16)
        st = st_ref[h]
        att = lax.dot_general(qb, k.astype(BF16), (((1,), (1,)), ((), ())),
                              preferred_element_type=F32) * dmat_ref[h]
        inner = jnp.dot(att.astype(BF16), vb, preferred_element_type=F32)
        cross = jnp.dot(qb, st.astype(BF16), preferred_element_type=F32) * dq_ref[h]
        r = inner + cross
        kd = (k * dk_ref[h]).astype(BF16)
        st_ref[h] = gc_ref[h] * st + lax.dot_general(
            kd, vb, (((0,), (0,)), ((), ())), preferred_element_type=F32)
        mu = jnp.mean(r, axis=-1, keepdims=True)
        xc = r - mu
        var = jnp.mean(xc * xc, axis=-1, keepdims=True)
        y = xc * lax.rsqrt(var + EPS) * g_ref[:, vs]
        rg = rg_ref[0, :, vs]
        o_ref[0, :, vs] = (rg * jax.nn.sigmoid(rg) * y).astype(o_ref.dtype)

    @pl.when(c == pl.num_programs(1) - 1)
    def _():
        sout_ref[0] = st_ref[...]


def _ret_tables(log_g, chunk, valid):
    f32 = np.float32
    idx = np.arange(chunk, dtype=f32)
    diff = idx[:, None] - idx[None, :]
    ex = lambda e: np.exp(e.astype(f32).astype(np.float64)).astype(f32)
    dmat = np.where(diff[None] >= 0, ex(np.maximum(diff, f32(0))[None] * log_g[:, None, None]),
                    f32(0))
    dq = ex((idx + f32(1))[None, :] * log_g[:, None])
    dk = np.where(idx[None, :] < valid,
                  ex((f32(valid - 1) - idx)[None, :] * log_g[:, None]), f32(0))
    gc = ex(f32(valid) * log_g)
    h = log_g.shape[0]
    return (dmat.astype(f32),
            np.ascontiguousarray(np.broadcast_to(dq[:, :, None], (h, chunk, RET_VD))).astype(f32),
            np.ascontiguousarray(np.broadcast_to(dk[:, :, None], (h, chunk, RET_KD))).astype(f32),
            np.ascontiguousarray(np.broadcast_to(gc[:, None, None], (h, 1, RET_VD))).astype(f32))


def _retention(rqk, rv, gates, state0, g, log_g, chunk, valid):
    b, s, _ = rqk.shape
    nc = s // chunk
    dmat, dq, dk, gc = _ret_tables(log_g, chunk, valid)
    qw = RET_HEADS * RET_KD
    vw = RET_HEADS * RET_VD
    full = lambda shp: pl.BlockSpec(shp, lambda bi, c: (0,) * len(shp))
    return pl.pallas_call(
        _ret_kernel,
        out_shape=(jax.ShapeDtypeStruct((b, s, vw), BF16),
                   jax.ShapeDtypeStruct((b, RET_HEADS, RET_KD, RET_VD), F32)),
        grid=(b, nc),
        in_specs=[pl.BlockSpec((1, chunk, qw), lambda bi, c: (bi, c, 0)),
                  pl.BlockSpec((1, chunk, qw), lambda bi, c: (bi, c, 1)),
                  pl.BlockSpec((1, chunk, vw), lambda bi, c: (bi, c, 0)),
                  pl.BlockSpec((1, chunk, vw), lambda bi, c: (bi, c, 0)),
                  pl.BlockSpec((1, RET_HEADS, RET_KD, RET_VD), lambda bi, c: (bi, 0, 0, 0)),
                  full(dmat.shape), full(dq.shape), full(dk.shape), full(gc.shape),
                  full((1, vw))],
        out_specs=(pl.BlockSpec((1, chunk, vw), lambda bi, c: (bi, c, 0)),
                   pl.BlockSpec((1, RET_HEADS, RET_KD, RET_VD), lambda bi, c: (bi, 0, 0, 0))),
        scratch_shapes=[pltpu.VMEM((RET_HEADS, RET_KD, RET_VD), F32)],
        compiler_params=_params(("arbitrary", "arbitrary")),
        name="retention",
    )(rqk, rqk, rv, gates, state0, dmat, dq, dk, gc, g.reshape(1, vw))


def _finish_kernel(a_ref, r_ref, ga_ref, gr_ref, h_ref, wa_ref, wr_ref, wo_ref, gffn_ref,
                   wrt_ref, brt_ref, carry0_ref,
                   h1_ref, hn_ref, route_ref, cnt_ref, carry_ref, *, tm):
    i = pl.program_id(0)

    @pl.when(i == 0)
    def _():
        carry_ref[...] = carry0_ref[...]

    am = jnp.dot(a_ref[...], wa_ref[...], preferred_element_type=F32)
    rm = jnp.dot(r_ref[...], wr_ref[...], preferred_element_type=F32)
    mix = jax.nn.sigmoid(ga_ref[...]) * am + jax.nn.sigmoid(gr_ref[...]) * rm
    h1 = h_ref[...] + jnp.dot(mix.astype(BF16), wo_ref[...], preferred_element_type=F32)
    h1_ref[...] = h1
    ms = jnp.mean(h1 * h1, axis=-1, keepdims=True)
    hn = h1 * lax.rsqrt(ms + EPS) * gffn_ref[...]
    hn_ref[...] = hn
    logits = jnp.dot(hn.astype(BF16), wrt_ref[...], preferred_element_type=F32) + brt_ref[...]
    lane = lax.broadcasted_iota(I32, (tm, LANES), 1)
    is_g = (lane >= N_EXPERTS) & (lane < N_EXPERTS + N_GROUPS)
    glog = jnp.where(is_g, logits, -jnp.inf)
    gmax = jnp.max(glog, axis=-1, keepdims=True)
    gidx = jnp.min(jnp.where(glog == gmax, lane - N_EXPERTS, LANES), axis=-1, keepdims=True)
    gden = jnp.sum(jnp.where(is_g, jnp.exp(glog - gmax), 0.0), axis=-1, keepdims=True)
    gp = 1.0 / gden
    in_g = (lane < N_EXPERTS) & ((lane // EXPERTS_PER_GROUP) == gidx)
    e1 = jnp.where(in_g, logits, -jnp.inf)
    v1 = jnp.max(e1, axis=-1, keepdims=True)
    i1 = jnp.min(jnp.where(e1 == v1, lane, LANES), axis=-1, keepdims=True)
    e2 = jnp.where(lane == i1, -jnp.inf, e1)
    v2 = jnp.max(e2, axis=-1, keepdims=True)
    i2 = jnp.min(jnp.where(e2 == v2, lane, LANES), axis=-1, keepdims=True)
    tt = jnp.exp(v2 - v1)
    w1 = gp / (1.0 + tt)
    w2 = gp * tt / (1.0 + tt)
    oh = jnp.where((lane == i1) | (lane == i2), 1.0, 0.0)
    rr = lax.broadcasted_iota(I32, (tm, tm), 0)
    cc = lax.broadcasted_iota(I32, (tm, tm), 1)
    tri = jnp.where(cc < rr, 1.0, 0.0).astype(BF16)
    before = jnp.dot(tri, oh.astype(BF16), preferred_element_type=F32) + carry_ref[...]
    rank1 = jnp.sum(jnp.where(lane == i1, before, 0.0), axis=-1, keepdims=True)
    rank2 = jnp.sum(jnp.where(lane == i2, before, 0.0), axis=-1, keepdims=True)
    carry = carry_ref[...] + jnp.sum(oh, axis=0, keepdims=True)
    carry_ref[...] = carry
    cnt_ref[...] = carry
    cols = [i1.astype(F32), i2.astype(F32), w1, w2, rank1, rank2]
    route = jnp.zeros((tm, LANES), F32)
    for ci, val in enumerate(cols):
        route = jnp.where(lane == ci, val, route)
    route_ref[...] = route


def _finish(a, r, gagr, h, wa, wr, wo, gffn, wrt, brt, carry0, tm):
    m, d = h.shape
    aw = a.shape[1]
    assert gagr.shape[1] == 2 * d
    const = lambda shp: pl.BlockSpec(shp, lambda i: (0,) * len(shp))
    return pl.pallas_call(
        functools.partial(_finish_kernel, tm=tm),
        out_shape=(jax.ShapeDtypeStruct((m, d), F32),
                   jax.ShapeDtypeStruct((m, d), F32),
                   jax.ShapeDtypeStruct((m, LANES), F32),
                   jax.ShapeDtypeStruct((1, LANES), F32)),
        grid=(m // tm,),
        in_specs=[pl.BlockSpec((tm, aw), lambda i: (i, 0)),
                  pl.BlockSpec((tm, aw), lambda i: (i, 0)),
                  pl.BlockSpec((tm, d), lambda i: (i, 0)),
                  pl.BlockSpec((tm, d), lambda i: (i, 1)),
                  pl.BlockSpec((tm, d), lambda i: (i, 0)),
                  const(wa.shape), const(wr.shape), const(wo.shape), const((1, d)),
                  const(wrt.shape), const((1, LANES)), const((1, LANES))],
        out_specs=(pl.BlockSpec((tm, d), lambda i: (i, 0)),
                   pl.BlockSpec((tm, d), lambda i: (i, 0)),
                   pl.BlockSpec((tm, LANES), lambda i: (i, 0)),
                   pl.BlockSpec((1, LANES), lambda i: (0, 0))),
        scratch_shapes=[pltpu.VMEM((1, LANES), F32)],
        compiler_params=_params(("arbitrary",)),
        name="finish",
    )(a, r, gagr, gagr, h, wa, wr, wo, gffn.reshape(1, d), wrt, brt, carry0)


def _moe_kernel(item_e, item_start, item_n, row_src,
                hn_hbm, wg_ref, wu_ref, wd_ref, yk_hbm,
                xg, yacc, gsem, ssem, *, sub):
    i = pl.program_id(0)
    f = pl.program_id(1)
    ni = pl.num_programs(0)
    nf = pl.num_programs(1)
    n = item_n[i]
    slot = lax.rem(i, 2)

    def gather_copy(s, r, tok):
        return pltpu.make_async_copy(hn_hbm.at[pl.ds(tok, 1)], xg.at[s, pl.ds(r, 1)], gsem)

    def scatter_copy(r, k, tok):
        return pltpu.make_async_copy(yacc.at[pl.ds(r, 1)], yk_hbm.at[k, pl.ds(tok, 1)], ssem)

    def start_gather(item, s):
        first = item_start[item]

        def issue(r, c):
            gather_copy(s, r, row_src[first + r] >> 1).start()
            return c
        lax.fori_loop(0, item_n[item], issue, 0)

    def wait_rows(copy, count):
        def wait(r, c):
            copy.wait()
            return c
        lax.fori_loop(0, count, wait, 0)

    @pl.when(f == 0)
    def _():
        @pl.when(i == 0)
        def _():
            xg[...] = jnp.zeros(xg.shape, xg.dtype)
            start_gather(0, 0)

        @pl.when(i > 0)
        def _():
            wait_rows(scatter_copy(0, 0, 0), item_n[i - 1])
        wait_rows(gather_copy(0, 0, 0), n)

    @pl.when((f == 1) & (i + 1 < ni))
    def _():
        start_gather(i + 1, 1 - slot)

    @pl.when(n > 0)
    def _():
        def run_rows(off, m):
            x = xg[slot, pl.ds(off, m), :]
            hg = jnp.dot(x, wg_ref[0], preferred_element_type=F32)
            hu = jnp.dot(x, wu_ref[0], preferred_element_type=F32)
            hm = hg * jax.nn.sigmoid(hg) * hu
            part = jnp.dot(hm, wd_ref[0], preferred_element_type=F32)

            @pl.when(f == 0)
            def _():
                yacc[pl.ds(off, m), :] = part

            @pl.when(f > 0)
            def _():
                yacc[pl.ds(off, m), :] += part

        nsub = (n + sub - 1) // sub
        nbig = nsub // 4

        def big(s, c):
            run_rows(pl.multiple_of(s * (4 * sub), 4 * sub), 4 * sub)
            return c
        lax.fori_loop(0, nbig, big, 0)
        off2 = pl.multiple_of(nbig * (4 * sub), 2 * sub)

        @pl.when((nsub & 2) != 0)
        def _():
            run_rows(off2, 2 * sub)
        off1 = pl.multiple_of(off2 + (nsub & 2) * sub, sub)

        @pl.when((nsub & 1) != 0)
        def _():
            run_rows(off1, sub)

    @pl.when(f == nf - 1)
    def _():
        first = item_start[i]

        def issue(r, c):
            src = row_src[first + r]
            scatter_copy(r, src & 1, src >> 1).start()
            return c
        lax.fori_loop(0, n, issue, 0)

        @pl.when(i == ni - 1)
        def _():
            wait_rows(scatter_copy(0, 0, 0), n)


def _moe_plan(route, counts, rows_per_item, max_items):
    ntok = route.shape[0]
    eid = route[:, 0:2].astype(I32)
    rank = route[:, 4:6].astype(I32)
    counts = counts.astype(I32)
    ends = jnp.cumsum(counts)
    starts = ends - counts
    dest = (starts[eid] + rank).reshape(-1)
    src = jnp.arange(2 * ntok, dtype=I32)
    row_src = jnp.zeros((2 * ntok,), I32).at[dest].set(src)
    nit = (counts + rows_per_item - 1) // rows_per_item
    it_end = jnp.cumsum(nit)
    it_first = it_end - nit
    total = it_end[-1]
    t = jnp.arange(max_items, dtype=I32)
    e = jnp.minimum(jnp.searchsorted(it_end, t, side='right'), N_EXPERTS - 1).astype(I32)
    j = t - it_first[e]
    n = jnp.clip(counts[e] - j * rows_per_item, 0, rows_per_item)
    n = jnp.where(t < total, n, 0).astype(I32)
    start = (starts[e] + j * rows_per_item).astype(I32)
    start = jnp.where(n > 0, start, 0)
    last_e = e[jnp.maximum(total - 1, 0)]
    e = jnp.where(t < total, e, last_e)
    return e, start, n, row_src


def _moe(hn, route, counts, wg, wu, wd):
    ntok, d = hn.shape
    ne, _, ff = wg.shape
    rows, sub, nf = MOE_ROWS, MOE_SUB, MOE_FSPLIT
    assert nf > 1 and rows % sub == 0
    tf = ff // nf
    max_items = ne + (2 * ntok) // rows
    item_e, item_start, item_n, row_src = _moe_plan(route, counts, rows, max_items)

    def f_eff(i, f, item_n):
        return jnp.where(item_n[i] > 0, f, nf - 1)

    return pl.pallas_call(
        functools.partial(_moe_kernel, sub=sub),
        out_shape=jax.ShapeDtypeStruct((2, ntok, d), F32),
        grid_spec=pltpu.PrefetchScalarGridSpec(
            num_scalar_prefetch=4,
            grid=(max_items, nf),
            in_specs=[pl.BlockSpec(memory_space=pl.ANY),
                      pl.BlockSpec((1, d, tf), lambda i, f, ie, ist, inn, rs: (ie[i], 0, f_eff(i, f, inn))),
                      pl.BlockSpec((1, d, tf), lambda i, f, ie, ist, inn, rs: (ie[i], 0, f_eff(i, f, inn))),
                      pl.BlockSpec((1, tf, d), lambda i, f, ie, ist, inn, rs: (ie[i], f_eff(i, f, inn), 0))],
            out_specs=pl.BlockSpec(memory_space=pl.ANY),
            scratch_shapes=[pltpu.VMEM((2, rows, d), F32),
                            pltpu.VMEM((rows, d), F32),
                            pltpu.SemaphoreType.DMA,
                            pltpu.SemaphoreType.DMA]),
        compiler_params=_params(("arbitrary", "arbitrary")),
        name="moe_experts",
    )(item_e, item_start, item_n, row_src, hn, wg, wu, wd)


def _ple_kernel(h1_ref, y0_ref, y1_ref, route_ref, pe_ref, gple_ref, wg_ref, wp_ref, gfin_ref,
                o_ref):
    route = route_ref[...]
    y = route[:, 2:3] * y0_ref[0] + route[:, 3:4] * y1_ref[0]
    h2 = h1_ref[...] + y
    ms = jnp.mean(h2 * h2, axis=-1, keepdims=True)
    hn = (h2 * lax.rsqrt(ms + EPS) * gple_ref[...]).astype(BF16)
    gate = jax.nn.sigmoid(jnp.dot(hn, wg_ref[...], preferred_element_type=F32))
    pp = jnp.dot(pe_ref[...].astype(BF16), wp_ref[...], preferred_element_type=F32)
    h3 = h2 + gate * pp
    ms = jnp.mean(h3 * h3, axis=-1, keepdims=True)
    o_ref[...] = h3 * lax.rsqrt(ms + EPS) * gfin_ref[...]


def _ple(h1, yk, row0, route, pe, gple, wg, wp, gfin, tm):
    m, d = h1.shape
    blk0 = row0 // tm
    const = lambda shp: pl.BlockSpec(shp, lambda i: (0,) * len(shp))
    return pl.pallas_call(
        _ple_kernel,
        out_shape=jax.ShapeDtypeStruct((m, d), F32),
        grid=(m // tm,),
        in_specs=[pl.BlockSpec((tm, d), lambda i: (i, 0)),
                  pl.BlockSpec((1, tm, d), lambda i: (0, i + blk0, 0)),
                  pl.BlockSpec((1, tm, d), lambda i: (1, i + blk0, 0)),
                  pl.BlockSpec((tm, LANES), lambda i: (i, 0)),
                  pl.BlockSpec((tm, pe.shape[1]), lambda i: (i, 0)),
                  const((1, d)), const(wg.shape), const(wp.shape), const((1, d))],
        out_specs=pl.BlockSpec((tm, d), lambda i: (i, 0)),
        compiler_params=_params(("arbitrary",)),
        name="ple_final",
    )(h1, yk, yk, route, pe, gple.reshape(1, d), wg, wp, gfin.reshape(1, d))


def _project_group(x, g, w_in, pos, tm, q_scale, k_seq_tiles=0):
    xn = _rmsnorm(x, g, tm)
    da_q = _rope_tables(pos, DA_ROT, DA_HD, ROPE_THETA, q_scale)[:, None]
    da_k = _rope_tables(pos, DA_ROT, DA_HD, ROPE_THETA, 1.0)[:, None]
    rq_t = _rope_tables(pos, RET_KD, RET_KD, RET_THETA, 1.0)
    rk_t = _rope_tables(pos, RET_KD, RET_KD, RET_THETA, RET_KD ** -0.5)
    ret_t = np.stack([rq_t, rk_t], axis=1)
    q = _proj(xn, w_in, COL_Q, 1024, tm, BF16, rope=(da_q, DA_ROT // 2))
    k = _proj(xn, w_in, COL_K, 1024, tm, F32, rope=(da_k, DA_ROT // 2), seq_tiles=k_seq_tiles)
    v = _proj(xn, w_in, COL_V, 1024, tm, F32)
    rqk = _proj(xn, w_in, COL_RQK, 1024, tm, F32, rope=(ret_t, RET_KD // 2))
    rv = _proj(xn, w_in, COL_RV, 1024, tm, F32)
    rg = _proj(xn, w_in, COL_GATES, 1024, tm, F32)
    gagr = _proj(xn, w_in, COL_GATES + 1024, 4096, tm, F32)
    return q, k, v, rqk, rv, rg, gagr


def kernel(x_prompt, x_sample, cache_k, cache_v, state_ret, page_table, p_prompt, p_sample, norm_mix_g, w_in, lam_q1, lam_k1, lam_q2, lam_k2, da_norm_g, ret_norm_g, w_br_attn, w_br_ret, w_out, norm_ffn_g, w_router_group, b_router_group, w_router_expert, b_router_expert, w_exp_gate, w_exp_up, w_exp_down, norm_ple_g, w_ple_gate, w_ple_proj, final_norm_g):
    bp, sp, d = x_prompt.shape
    bs, ts, _ = x_sample.shape
    depth = w_in.shape[0]
    assert depth == 1, "the final norm is fused into the last stage of a single layer"
    n_pool, page = cache_k.shape[1], cache_k.shape[2]
    mp, msm = bp * sp, bs * ts
    log_g = np.log1p(-np.exp2(-5.0 - np.arange(RET_HEADS))).astype(np.float32)
    pos_p = np.arange(sp)
    pos_s = np.tile(PAST_LEN + np.arange(ts), bs)
    cache_k2 = cache_k.transpose(0, 1, 3, 4, 5, 2).reshape(depth * n_pool, DA_HEADS * 2 * DA_HD, page)
    cache_v2 = cache_v.reshape(depth * n_pool, page * DA_HEADS, DA_VD)
    tm_p = next(t for t in (1024, 512, 256, 128) if sp % t == 0)
    tf_p = 256

    hp = x_prompt.reshape(mp, d)
    hs = x_sample.reshape(msm, d)
    kp_l, vp_l, rp_l, ks_l, vs_l, rs_l = [], [], [], [], [], []
    for li in range(depth):
        lam_init = 0.8 - 0.6 * math.exp(-0.3 * li)
        lam = (jnp.exp(jnp.sum(lam_q1[li] * lam_k1[li]))
               - jnp.exp(jnp.sum(lam_q2[li] * lam_k2[li])) + lam_init).reshape(1).astype(F32)
        out_scale = 1.0 - lam_init
        wa = w_br_attn[li].astype(BF16)
        wr = w_br_ret[li].astype(BF16)
        wo = w_out[li].astype(BF16)
        wpg = w_ple_gate[li].astype(BF16)
        wpp = w_ple_proj[li].astype(BF16)
        wrt = jnp.zeros((d, LANES), F32)
        wrt = wrt.at[:, :N_EXPERTS].set(w_router_expert[li])
        wrt = wrt.at[:, N_EXPERTS:N_EXPERTS + N_GROUPS].set(w_router_group[li]).astype(BF16)
        brt = jnp.zeros((1, LANES), F32)
        brt = brt.at[0, :N_EXPERTS].set(b_router_expert[li])
        brt = brt.at[0, N_EXPERTS:N_EXPERTS + N_GROUPS].set(b_router_group[li])
        da_g = da_norm_g[li].reshape(1, -1)

        q, k, v, rqk, rv, rg, gagr = _project_group(hp, norm_mix_g[li], w_in[li], pos_p, tm_p,
                                                    DA_HD ** -0.5 * math.log2(math.e),
                                                    k_seq_tiles=sp // tm_p)
        a_p = _attn_prompt(lam, q, k, v, da_g, bp, sp, out_scale, tq=256, hp=4)
        chunk = 256 if sp % 256 == 0 else 128
        r_p, st_p = _retention(rqk.reshape(bp, sp, -1), rv.reshape(bp, sp, -1),
                               rg.reshape(bp, sp, -1),
                               jnp.zeros((bp, RET_HEADS, RET_KD, RET_VD), F32),
                               ret_norm_g[li], log_g, chunk, chunk)
        kp_l.append(k.reshape(bp, DA_HEADS, 2, DA_HD, sp).transpose(0, 4, 1, 2, 3))
        vp_l.append(v.reshape(bp, sp, DA_HEADS, DA_VD))
        rp_l.append(st_p)
        h1_p, hn_p, route_p, cnt_p = _finish(
            a_p, r_p.reshape(mp, -1), gagr, hp, wa, wr, wo, norm_ffn_g[li], wrt, brt,
            jnp.zeros((1, LANES), F32), tf_p)

        q, k, v, rqk, rv, rg, gagr = _project_group(hs, norm_mix_g[li], w_in[li], pos_s, msm,
                                                    DA_HD ** -0.5)
        tpad = 8
        pad_t = lambda z: jnp.pad(z.reshape(bs, ts, -1), ((0, 0), (0, tpad - ts), (0, 0)))
        q4 = q.astype(F32).reshape(bs, ts, DA_HEADS, 2, DA_HD)
        eye = jnp.eye(2, dtype=F32)
        qm = jnp.einsum('bthmd,mn->bhmtnd', q4, eye).reshape(bs, DA_HEADS * 2 * ts, 2 * DA_HD)
        a_s = _attn_paged(page_table, lam, qm, pad_t(k), pad_t(v), da_g, cache_k2, cache_v2,
                          li * n_pool, ts, out_scale, npg=8)
        r_s, st_s = _retention(pad_t(rqk), pad_t(rv), pad_t(rg), state_ret[li],
                               ret_norm_g[li], log_g, tpad, ts)
        ks_l.append(k.reshape(bs, ts, DA_HEADS, 2, DA_HD))
        vs_l.append(v.reshape(bs, ts, DA_HEADS, DA_VD))
        rs_l.append(st_s)
        h1_s, hn_s, route_s, cnt = _finish(
            a_s.reshape(msm, -1), r_s[:, :ts].reshape(msm, -1), gagr, hs, wa, wr, wo,
            norm_ffn_g[li], wrt, brt, cnt_p, msm)

        hn_all = jnp.concatenate([hn_p, hn_s], axis=0)
        route_all = jnp.concatenate([route_p, route_s], axis=0)
        yk = _moe(hn_all, route_all, cnt[0, :N_EXPERTS], w_exp_gate[li], w_exp_up[li],
                  w_exp_down[li])

        gfin = final_norm_g
        hp = _ple(h1_p, yk, 0, route_p, p_prompt[li].reshape(mp, -1), norm_ple_g[li], wpg, wpp,
                  gfin, tf_p)
        hs = _ple(h1_s, yk, mp, route_s, p_sample[li].reshape(msm, -1), norm_ple_g[li], wpg, wpp,
                  gfin, msm)
    y_prompt = hp.reshape(bp, sp, d)
    y_sample = hs.reshape(bs, ts, d)
    return (y_prompt, y_sample, jnp.stack(kp_l), jnp.stack(vp_l), jnp.stack(rp_l),
            jnp.stack(ks_l), jnp.stack(vs_l), jnp.stack(rs_l))
```

```python
import functools
import math

import jax
import jax.numpy as jnp
import numpy as np
from jax import lax
from jax.experimental import pallas as pl
from jax.experimental.pallas import tpu as pltpu

F32 = jnp.float32
BF16 = jnp.bfloat16
I32 = jnp.int32

PAST_LEN = 16384
DA_HEADS = 8
DA_VD = 128
DA_HD = 64
DA_ROT = 16
ROPE_THETA = 500000.0
RET_HEADS = 8
RET_VD = 128
RET_KD = 64
RET_THETA = 10000.0
N_GROUPS = 4
EXPERTS_PER_GROUP = 8
N_EXPERTS = N_GROUPS * EXPERTS_PER_GROUP
EPS = 1e-6
NEG = -1e30
LANES = 128

VMEM_LIMIT = 56 * 1024 * 1024

COL_Q, COL_K, COL_V, COL_RQK, COL_RV, COL_GATES = 0, 1024, 2048, 3072, 4096, 5120

MOE_ROWS = 768
MOE_SUB = 128
MOE_FSPLIT = 4


def _params(sem, vmem=VMEM_LIMIT):
    return pltpu.CompilerParams(dimension_semantics=sem, vmem_limit_bytes=vmem)


def _rmsnorm_kernel(x_ref, g_ref, o_ref):
    x = x_ref[...]
    ms = jnp.mean(x * x, axis=-1, keepdims=True)
    o_ref[...] = (x * lax.rsqrt(ms + EPS) * g_ref[...]).astype(o_ref.dtype)


def _rmsnorm(x, g, tm):
    m, d = x.shape
    return pl.pallas_call(
        _rmsnorm_kernel,
        out_shape=jax.ShapeDtypeStruct((m, d), BF16),
        grid=(m // tm,),
        in_specs=[pl.BlockSpec((tm, d), lambda i: (i, 0)),
                  pl.BlockSpec((1, d), lambda i: (0, 0))],
        out_specs=pl.BlockSpec((tm, d), lambda i: (i, 0)),
        compiler_params=_params(("arbitrary",)),
        name="rmsnorm",
    )(x, g.reshape(1, d))


def _proj_kernel(*refs, shift, tn, seq_tiles):
    if shift:
        xn_ref, w_ref, c_ref, s1_ref, s2_ref, o_ref, wbf_ref = refs
    else:
        xn_ref, w_ref, o_ref, wbf_ref = refs

    @pl.when(pl.program_id(1) == 0)
    def _():
        wbf_ref[...] = w_ref[...].astype(BF16)

    acc = jnp.dot(xn_ref[...], wbf_ref[...], preferred_element_type=F32)
    if shift:
        groups = c_ref.shape[0]
        rep = tn // LANES // groups
        wide = lambda ref: jnp.concatenate(
            [jnp.tile(ref[g], (1, rep)) for g in range(groups)], axis=1)
        acc = (acc * wide(c_ref) + pltpu.roll(acc, tn - shift, 1) * wide(s1_ref)
               + pltpu.roll(acc, shift, 1) * wide(s2_ref))
    if seq_tiles:
        o_ref[0] = acc.T.astype(o_ref.dtype)
    else:
        o_ref[...] = acc.astype(o_ref.dtype)


def _proj(xn, w_in, col0, ncols, tm, out_dtype, rope=None, seq_tiles=0):
    m, d = xn.shape
    tn = 1024
    nj = ncols // tn
    j0 = col0 // tn
    in_specs = [pl.BlockSpec((tm, d), lambda j, i: (i, 0)),
                pl.BlockSpec((d, tn), lambda j, i: (0, j + j0))]
    args = [xn, w_in]
    shift = 0
    if rope is not None:
        assert nj == 1
        tabs, shift = rope
        groups, npos = tabs.shape[1], tabs.shape[2]
        npb = npos // tm
        for t in range(3):
            in_specs.append(pl.BlockSpec((groups, tm, LANES), lambda j, i: (0, i % npb, 0)))
            args.append(tabs[t])
    if seq_tiles:
        out_shape = jax.ShapeDtypeStruct((m // (seq_tiles * tm), ncols, seq_tiles * tm), out_dtype)
        out_spec = pl.BlockSpec((1, tn, tm), lambda j, i: (i // seq_tiles, j, i % seq_tiles))
    else:
        out_shape = jax.ShapeDtypeStruct((m, ncols), out_dtype)
        out_spec = pl.BlockSpec((tm, tn), lambda j, i: (i, j))
    return pl.pallas_call(
        functools.partial(_proj_kernel, shift=shift, tn=tn, seq_tiles=seq_tiles),
        out_shape=out_shape,
        grid=(nj, m // tm),
        in_specs=in_specs,
        out_specs=out_spec,
        scratch_shapes=[pltpu.VMEM((d, tn), BF16)],
        compiler_params=_params(("arbitrary", "arbitrary")),
        name="in_proj",
    )(*args)


def _rope_tables(pos, rot_dim, period, theta, scale):
    half = rot_dim // 2
    inv = (1.0 / np.power(np.float32(theta), np.arange(half, dtype=np.float32)
                          * np.float32(2.0 / rot_dim))).astype(np.float32)
    ang = (pos.astype(np.float32)[:, None] * inv[None, :]).astype(np.float64)
    cos, sin = np.cos(ang), np.sin(ang)
    npos = pos.shape[0]
    pad = period - rot_dim
    c = np.concatenate([cos, cos, np.ones((npos, pad))], axis=1)
    s1 = np.concatenate([-sin, np.zeros((npos, half + pad))], axis=1)
    s2 = np.concatenate([np.zeros((npos, half)), sin, np.zeros((npos, pad))], axis=1)
    tabs = np.stack([c, s1, s2]).astype(np.float32) * np.float32(scale)
    return np.tile(tabs, (1, 1, LANES // period)).astype(np.float32)


def _attn_prompt_kernel(lam_ref, q_ref, k_ref, v_ref, g_ref, o_ref,
                        kb_ref, vt_ref, m_ref, l_ref, acc_ref, *, tq, hp, out_scale):
    qi = pl.program_id(2)
    nblk = kb_ref.shape[1]

    @pl.when(qi == 0)
    def _():
        for c in range(nblk):
            for h in range(hp):
                kb_ref[h, c] = k_ref[0, h * LANES:(h + 1) * LANES,
                                     c * tq:(c + 1) * tq].T.astype(BF16)

        def stage(c, carry):
            off = pl.multiple_of(c * tq, tq)
            for h in range(hp):
                cols = slice(h * LANES, (h + 1) * LANES)
                vt_ref[h, c] = v_ref[pl.ds(off, tq), cols].T.astype(BF16)
            return carry
        lax.fori_loop(0, nblk, stage, 0)

    row = lax.broadcasted_iota(I32, (LANES, tq), 0)
    qqt = []
    for h in range(hp):
        qt = q_ref[:, h * LANES:(h + 1) * LANES].astype(F32).T
        qqt.append(jnp.concatenate([jnp.where(row < DA_HD, qt, 0.0),
                                    jnp.where(row >= DA_HD, qt, 0.0)], axis=1).astype(BF16))
    m_ref[...] = jnp.full(m_ref.shape, NEG, F32)
    l_ref[...] = jnp.zeros(l_ref.shape, F32)
    acc_ref[...] = jnp.zeros(acc_ref.shape, F32)

    def step(j, masked):
        sts = [jnp.dot(kb_ref[h, j], qqt[h], preferred_element_type=F32)
               for h in range(hp)]
        for h in range(hp):
            st = sts[h]
            if masked:
                key = lax.broadcasted_iota(I32, (tq, 2 * tq), 0)
                qry = lax.broadcasted_iota(I32, (tq, 2 * tq), 1) & (tq - 1)
                st = jnp.where(key <= qry, st, NEG)
            m_old = m_ref[h]
            m_new = jnp.maximum(m_old, jnp.max(st, axis=0, keepdims=True))
            alpha = jnp.exp2(m_old - m_new)
            p = jnp.exp2(st - m_new)
            l_ref[h] = alpha * l_ref[h] + jnp.sum(p, axis=0, keepdims=True)
            acc_ref[h] = alpha * acc_ref[h] + jnp.dot(vt_ref[h, j], p.astype(BF16),
                                                      preferred_element_type=F32)
            m_ref[h] = m_new

    def body(j, carry):
        step(j, False)
        return carry

    lax.fori_loop(0, qi, body, 0)
    step(qi, True)

    for h in range(hp):
        cols = slice(h * LANES, (h + 1) * LANES)
        o = acc_ref[h] / l_ref[h]
        out = (o[:, :tq] - lam_ref[0] * o[:, tq:]).T
        ms = jnp.mean(out * out, axis=-1, keepdims=True)
        out = out * lax.rsqrt(ms + EPS) * g_ref[:, cols] * out_scale
        o_ref[:, cols] = out.astype(o_ref.dtype)


def _attn_prompt(lam, q, k, v, g, b, s, out_scale, tq, hp):
    nq = s // tq
    w = hp * LANES
    return pl.pallas_call(
        functools.partial(_attn_prompt_kernel, tq=tq, hp=hp, out_scale=out_scale),
        out_shape=jax.ShapeDtypeStruct((b * s, DA_HEADS * DA_VD), BF16),
        grid=(b, DA_HEADS // hp, nq),
        in_specs=[pl.BlockSpec(memory_space=pltpu.SMEM),
                  pl.BlockSpec((tq, w), lambda bi, h, qi: (bi * nq + qi, h)),
                  pl.BlockSpec((1, w, s), lambda bi, h, qi: (bi, h, 0)),
                  pl.BlockSpec((s, w), lambda bi, h, qi: (bi, h)),
                  pl.BlockSpec((1, w), lambda bi, h, qi: (0, h))],
        out_specs=pl.BlockSpec((tq, w), lambda bi, h, qi: (bi * nq + qi, h)),
        scratch_shapes=[pltpu.VMEM((hp, nq, tq, LANES), BF16),
                        pltpu.VMEM((hp, nq, LANES, tq), BF16),
                        pltpu.VMEM((hp, 1, 2 * tq), F32),
                        pltpu.VMEM((hp, 1, 2 * tq), F32),
                        pltpu.VMEM((hp, LANES, 2 * tq), F32)],
        compiler_params=_params(("arbitrary", "arbitrary", "arbitrary")),
        name="attn_prompt",
    )(lam, q, k, v, g)


def _attn_paged_kernel(pt_ref, lam_ref, q_ref, kn_ref, vn_ref, g_ref, *rest,
                       npg, t, out_scale):
    k_refs = rest[:npg]
    v_refs = rest[npg:2 * npg]
    o_ref, m_ref, l_ref, acc_ref = rest[2 * npg:]
    c = pl.program_id(1)
    nc = pl.num_programs(1)
    rows = 2 * t
    page = k_refs[0].shape[2]

    @pl.when(c == 0)
    def _():
        m_ref[...] = jnp.full(m_ref.shape, NEG, F32)
        l_ref[...] = jnp.zeros(l_ref.shape, F32)
        acc_ref[...] = jnp.zeros(acc_ref.shape, F32)

    def update(scores, values):
        s_all = jnp.concatenate(scores, axis=0)
        m_old = m_ref[...]
        m_new = jnp.maximum(m_old, jnp.max(s_all, axis=-1, keepdims=True))
        alpha = jnp.exp(m_old - m_new)
        p = jnp.exp(s_all - m_new)
        l_ref[...] = alpha * l_ref[...] + jnp.sum(p, axis=-1, keepdims=True)
        m_ref[...] = m_new
        pb = p.astype(BF16)
        for h in range(DA_HEADS):
            r0 = h * rows
            pv = None
            for lo, hi, load_v in values[h]:
                d = jnp.dot(pb[r0:r0 + rows, lo:hi], load_v(), preferred_element_type=F32)
                pv = d if pv is None else pv + d
            acc_ref[r0:r0 + rows, :] = alpha[r0:r0 + rows] * acc_ref[r0:r0 + rows, :] + pv

    q = q_ref[0].astype(BF16)
    scores, values = [], []
    for h in range(DA_HEADS):
        qh = q[h * rows:(h + 1) * rows]
        sh, vh = [], []
        for i in range(npg):
            kb = k_refs[i][0, h * LANES:(h + 1) * LANES, :].astype(BF16)
            sh.append(jnp.dot(qh, kb, preferred_element_type=F32))
            vh.append((i * page, (i + 1) * page,
                       lambda i=i, h=h: v_refs[i][0, pl.ds(h, page, stride=DA_HEADS), :].astype(BF16)))
        scores.append(jnp.concatenate(sh, axis=1))
        values.append(vh)
    update(scores, values)

    @pl.when(c == nc - 1)
    def _():
        tp = kn_ref.shape[1]
        scores, values = [], []
        row = lax.broadcasted_iota(I32, (rows, tp), 0) & (t - 1)
        col = lax.broadcasted_iota(I32, (rows, tp), 1)
        for h in range(DA_HEADS):
            qh = q[h * rows:(h + 1) * rows]
            kb = kn_ref[0, :, h * LANES:(h + 1) * LANES].astype(BF16)
            s = lax.dot_general(qh, kb, (((1,), (1,)), ((), ())), preferred_element_type=F32)
            scores.append(jnp.where(col <= row, s, NEG))
            values.append([(0, tp, lambda h=h: vn_ref[0, :, h * LANES:(h + 1) * LANES].astype(BF16))])
        update(scores, values)
        o = acc_ref[...] / l_ref[...]
        lam = lam_ref[0]
        for h in range(DA_HEADS):
            r0 = h * rows
            out = o[r0:r0 + t] - lam * o[r0 + t:r0 + rows]
            ms = jnp.mean(out * out, axis=-1, keepdims=True)
            gh = g_ref[:, h * LANES:(h + 1) * LANES]
            o_ref[0, :, h * LANES:(h + 1) * LANES] = (
                out * lax.rsqrt(ms + EPS) * gh * out_scale).astype(o_ref.dtype)


def _attn_paged(page_table, lam, qm, k_new, v_new, g, cache_k, cache_v, page0, t, out_scale, npg):
    db, n_pages = page_table.shape
    prow, pcol = cache_k.shape[1], cache_k.shape[2]
    assert cache_v.shape[1:] == (prow, pcol)
    w = k_new.shape[2]
    tp = k_new.shape[1]
    nc = n_pages // npg
    pt_flat = page_table.reshape(-1) + page0

    def page_spec(i):
        return pl.BlockSpec((1, prow, pcol),
                            lambda b, c, pt: (pt[b * n_pages + c * npg + i], 0, 0))

    in_specs = [pl.BlockSpec(memory_space=pltpu.SMEM),
                pl.BlockSpec((1, qm.shape[1], LANES), lambda b, c, pt: (b, 0, 0)),
                pl.BlockSpec((1, tp, w), lambda b, c, pt: (b, 0, 0)),
                pl.BlockSpec((1, tp, w), lambda b, c, pt: (b, 0, 0)),
                pl.BlockSpec((1, w), lambda b, c, pt: (0, 0))]
    in_specs += [page_spec(i) for i in range(npg)] * 2
    rows = DA_HEADS * 2 * t
    return pl.pallas_call(
        functools.partial(_attn_paged_kernel, npg=npg, t=t, out_scale=out_scale),
        out_shape=jax.ShapeDtypeStruct((db, t, w), BF16),
        grid_spec=pltpu.PrefetchScalarGridSpec(
            num_scalar_prefetch=1,
            grid=(db, nc),
            in_specs=in_specs,
            out_specs=pl.BlockSpec((1, t, w), lambda b, c, pt: (b, 0, 0)),
            scratch_shapes=[pltpu.VMEM((rows, 1), F32),
                            pltpu.VMEM((rows, 1), F32),
                            pltpu.VMEM((rows, LANES), F32)]),
        compiler_params=_params(("arbitrary", "arbitrary")),
        name="attn_paged",
    )(pt_flat, lam, qm, k_new, v_new, g, *([cache_k] * npg), *([cache_v] * npg))


def _ret_kernel(rq_ref, rk_ref, rv_ref, rg_ref, s0_ref, dmat_ref, dq_ref, dk_ref, gc_ref, g_ref,
                o_ref, sout_ref, st_ref):
    c = pl.program_id(1)

    @pl.when(c == 0)
    def _():
        st_ref[...] = s0_ref[0]

    for h in range(RET_HEADS):
        kq = slice(h * RET_KD, (h + 1) * RET_KD)
        vs = slice(h * RET_VD, (h + 1) * RET_VD)
        qb = rq_ref[0, :, kq].astype(BF16)
        k = rk_ref[0, :, kq]
        vb = rv_ref[0, :, vs].astype(BF16)
        st = st_ref[h]
        att = lax.dot_general(qb, k.astype(BF16), (((1,), (1,)), ((), ())),
                              preferred_element_type=F32) * dmat_ref[h]
        inner = jnp.dot(att.astype(BF16), vb, preferred_element_type=F32)
        cross = jnp.dot(qb, st.astype(BF16), preferred_element_type=F32) * dq_ref[h]
        r = inner + cross
        kd = (k * dk_ref[h]).astype(BF16)
        st_ref[h] = gc_ref[h] * st + lax.dot_general(
            kd, vb, (((0,), (0,)), ((), ())), preferred_element_type=F32)
        mu = jnp.mean(r, axis=-1, keepdims=True)
        xc = r - mu
        var = jnp.mean(xc * xc, axis=-1, keepdims=True)
        y = xc * lax.rsqrt(var + EPS) * g_ref[:, vs]
        rg = rg_ref[0, :, vs]
        o_ref[0, :, vs] = (rg * jax.nn.sigmoid(rg) * y).astype(o_ref.dtype)

    @pl.when(c == pl.num_programs(1) - 1)
    def _():
        sout_ref[0] = st_ref[...]


def _ret_tables(log_g, chunk, valid):
    f32 = np.float32
    idx = np.arange(chunk, dtype=f32)
    diff = idx[:, None] - idx[None, :]
    ex = lambda e: np.exp(e.astype(f32).astype(np.float64)).astype(f32)
    dmat = np.where(diff[None] >= 0, ex(np.maximum(diff, f32(0))[None] * log_g[:, None, None]),
                    f32(0))
    dq = ex((idx + f32(1))[None, :] * log_g[:, None])
    dk = np.where(idx[None, :] < valid,
                  ex((f32(valid - 1) - idx)[None, :] * log_g[:, None]), f32(0))
    gc = ex(f32(valid) * log_g)
    h = log_g.shape[0]
    return (dmat.astype(f32),
            np.ascontiguousarray(np.broadcast_to(dq[:, :, None], (h, chunk, RET_VD))).astype(f32),
            np.ascontiguousarray(np.broadcast_to(dk[:, :, None], (h, chunk, RET_KD))).astype(f32),
            np.ascontiguousarray(np.broadcast_to(gc[:, None, None], (h, 1, RET_VD))).astype(f32))


def _retention(rqk, rv, gates, state0, g, log_g, chunk, valid):
    b, s, _ = rqk.shape
    nc = s // chunk
    dmat, dq, dk, gc = _ret_tables(log_g, chunk, valid)
    qw = RET_HEADS * RET_KD
    vw = RET_HEADS * RET_VD
    full = lambda shp: pl.BlockSpec(shp, lambda bi, c: (0,) * len(shp))
    return pl.pallas_call(
        _ret_kernel,
        out_shape=(jax.ShapeDtypeStruct((b, s, vw), BF16),
                   jax.ShapeDtypeStruct((b, RET_HEADS, RET_KD, RET_VD), F32)),
        grid=(b, nc),
        in_specs=[pl.BlockSpec((1, chunk, qw), lambda bi, c: (bi, c, 0)),
                  pl.BlockSpec((1, chunk, qw), lambda bi, c: (bi, c, 1)),
                  pl.BlockSpec((1, chunk, vw), lambda bi, c: (bi, c, 0)),
                  pl.BlockSpec((1, chunk, vw), lambda bi, c: (bi, c, 0)),
                  pl.BlockSpec((1, RET_HEADS, RET_KD, RET_VD), lambda bi, c: (bi, 0, 0, 0)),
                  full(dmat.shape), full(dq.shape), full(dk.shape), full(gc.shape),
                  full((1, vw))],
        out_specs=(pl.BlockSpec((1, chunk, vw), lambda bi, c: (bi, c, 0)),
                   pl.BlockSpec((1, RET_HEADS, RET_KD, RET_VD), lambda bi, c: (bi, 0, 0, 0))),
        scratch_shapes=[pltpu.VMEM((RET_HEADS, RET_KD, RET_VD), F32)],
        compiler_params=_params(("arbitrary", "arbitrary")),
        name="retention",
    )(rqk, rqk, rv, gates, state0, dmat, dq, dk, gc, g.reshape(1, vw))


def _finish_kernel(a_ref, r_ref, ga_ref, gr_ref, h_ref, wa_ref, wr_ref, wo_ref, gffn_ref,
                   wrt_ref, brt_ref, carry0_ref,
                   h1_ref, hn_ref, route_ref, cnt_ref, carry_ref, *, tm):
    i = pl.program_id(0)

    @pl.when(i == 0)
    def _():
        carry_ref[...] = carry0_ref[...]

    am = jnp.dot(a_ref[...], wa_ref[...], preferred_element_type=F32)
    rm = jnp.dot(r_ref[...], wr_ref[...], preferred_element_type=F32)
    mix = jax.nn.sigmoid(ga_ref[...]) * am + jax.nn.sigmoid(gr_ref[...]) * rm
    h1 = h_ref[...] + jnp.dot(mix.astype(BF16), wo_ref[...], preferred_element_type=F32)
    h1_ref[...] = h1
    ms = jnp.mean(h1 * h1, axis=-1, keepdims=True)
    hn = h1 * lax.rsqrt(ms + EPS) * gffn_ref[...]
    hn_ref[...] = hn
    logits = jnp.dot(hn.astype(BF16), wrt_ref[...], preferred_element_type=F32) + brt_ref[...]
    lane = lax.broadcasted_iota(I32, (tm, LANES), 1)
    is_g = (lane >= N_EXPERTS) & (lane < N_EXPERTS + N_GROUPS)
    glog = jnp.where(is_g, logits, -jnp.inf)
    gmax = jnp.max(glog, axis=-1, keepdims=True)
    gidx = jnp.min(jnp.where(glog == gmax, lane - N_EXPERTS, LANES), axis=-1, keepdims=True)
    gden = jnp.sum(jnp.where(is_g, jnp.exp(glog - gmax), 0.0), axis=-1, keepdims=True)
    gp = 1.0 / gden
    in_g = (lane < N_EXPERTS) & ((lane // EXPERTS_PER_GROUP) == gidx)
    e1 = jnp.where(in_g, logits, -jnp.inf)
    v1 = jnp.max(e1, axis=-1, keepdims=True)
    i1 = jnp.min(jnp.where(e1 == v1, lane, LANES), axis=-1, keepdims=True)
    e2 = jnp.where(lane == i1, -jnp.inf, e1)
    v2 = jnp.max(e2, axis=-1, keepdims=True)
    i2 = jnp.min(jnp.where(e2 == v2, lane, LANES), axis=-1, keepdims=True)
    tt = jnp.exp(v2 - v1)
    w1 = gp / (1.0 + tt)
    w2 = gp * tt / (1.0 + tt)
    oh = jnp.where((lane == i1) | (lane == i2), 1.0, 0.0)
    rr = lax.broadcasted_iota(I32, (tm, tm), 0)
    cc = lax.broadcasted_iota(I32, (tm, tm), 1)
    tri = jnp.where(cc < rr, 1.0, 0.0).astype(BF16)
    before = jnp.dot(tri, oh.astype(BF16), preferred_element_type=F32) + carry_ref[...]
    rank1 = jnp.sum(jnp.where(lane == i1, before, 0.0), axis=-1, keepdims=True)
    rank2 = jnp.sum(jnp.where(lane == i2, before, 0.0), axis=-1, keepdims=True)
    carry = carry_ref[...] + jnp.sum(oh, axis=0, keepdims=True)
    carry_ref[...] = carry
    cnt_ref[...] = carry
    cols = [i1.astype(F32), i2.astype(F32), w1, w2, rank1, rank2]
    route = jnp.zeros((tm, LANES), F32)
    for ci, val in enumerate(cols):
        route = jnp.where(lane == ci, val, route)
    route_ref[...] = route


def _finish(a, r, gagr, h, wa, wr, wo, gffn, wrt, brt, carry0, tm):
    m, d = h.shape
    aw = a.shape[1]
    assert gagr.shape[1] == 2 * d
    const = lambda shp: pl.BlockSpec(shp, lambda i: (0,) * len(shp))
    return pl.pallas_call(
        functools.partial(_finish_kernel, tm=tm),
        out_shape=(jax.ShapeDtypeStruct((m, d), F32),
                   jax.ShapeDtypeStruct((m, d), F32),
                   jax.ShapeDtypeStruct((m, LANES), F32),
                   jax.ShapeDtypeStruct((1, LANES), F32)),
        grid=(m // tm,),
        in_specs=[pl.BlockSpec((tm, aw), lambda i: (i, 0)),
                  pl.BlockSpec((tm, aw), lambda i: (i, 0)),
                  pl.BlockSpec((tm, d), lambda i: (i, 0)),
                  pl.BlockSpec((tm, d), lambda i: (i, 1)),
                  pl.BlockSpec((tm, d), lambda i: (i, 0)),
                  const(wa.shape), const(wr.shape), const(wo.shape), const((1, d)),
                  const(wrt.shape), const((1, LANES)), const((1, LANES))],
        out_specs=(pl.BlockSpec((tm, d), lambda i: (i, 0)),
                   pl.BlockSpec((tm, d), lambda i: (i, 0)),
                   pl.BlockSpec((tm, LANES), lambda i: (i, 0)),
                   pl.BlockSpec((1, LANES), lambda i: (0, 0))),
        scratch_shapes=[pltpu.VMEM((1, LANES), F32)],
        compiler_params=_params(("arbitrary",)),
        name="finish",
    )(a, r, gagr, gagr, h, wa, wr, wo, gffn.reshape(1, d), wrt, brt, carry0)


def _moe_kernel(item_e, item_start, item_n, row_src,
                hn_hbm, wg_ref, wu_ref, wd_ref, yk_hbm,
                xg, yacc, gsem, ssem, *, sub):
    i = pl.program_id(0)
    f = pl.program_id(1)
    ni = pl.num_programs(0)
    nf = pl.num_programs(1)
    n = item_n[i]
    slot = lax.rem(i, 2)

    grp = 8

    def gather_copy(s, r, tok, rows=1):
        return pltpu.make_async_copy(hn_hbm.at[pl.ds(tok, rows)], xg.at[s, pl.ds(r, rows)], gsem)

    def scatter_copy(r, k, tok, rows=1):
        return pltpu.make_async_copy(yacc.at[pl.ds(r, rows)], yk_hbm.at[k, pl.ds(tok, rows)], ssem)

    def repeat(count, fn):
        def body(t, c):
            fn(t)
            return c
        lax.fori_loop(0, count, body, 0)

    def start_gather(item, s):
        first = item_start[item]

        def issue(t):
            for u in range(grp):
                r = t * grp + u
                gather_copy(s, r, row_src[first + r] >> 1).start()
        repeat((item_n[item] + grp - 1) // grp, issue)

    def wait_gather(count):
        repeat((count + grp - 1) // grp, lambda t: gather_copy(0, 0, 0, grp).wait())

    def wait_scatter(count):
        repeat(count // grp, lambda t: scatter_copy(0, 0, 0, grp).wait())
        repeat(lax.rem(count, grp), lambda t: scatter_copy(0, 0, 0).wait())

    @pl.when(f == 0)
    def _():
        @pl.when(i == 0)
        def _():
            xg[...] = jnp.zeros(xg.shape, xg.dtype)
            start_gather(0, 0)

        @pl.when(i > 0)
        def _():
            wait_scatter(item_n[i - 1])
        wait_gather(n)

    @pl.when((f == 1) & (i + 1 < ni))
    def _():
        start_gather(i + 1, 1 - slot)

    @pl.when(n > 0)
    def _():
        def run_rows(off, m):
            x = xg[slot, pl.ds(off, m), :]
            hg = jnp.dot(x, wg_ref[0], preferred_element_type=F32)
            hu = jnp.dot(x, wu_ref[0], preferred_element_type=F32)
            hm = hg * jax.nn.sigmoid(hg) * hu
            part = jnp.dot(hm, wd_ref[0], preferred_element_type=F32)

            @pl.when(f == 0)
            def _():
                yacc[pl.ds(off, m), :] = part

            @pl.when(f > 0)
            def _():
                yacc[pl.ds(off, m), :] += part

        nsub = (n + sub - 1) // sub
        nbig = nsub // 4

        def big(s, c):
            run_rows(pl.multiple_of(s * (4 * sub), 4 * sub), 4 * sub)
            return c
        lax.fori_loop(0, nbig, big, 0)
        off2 = pl.multiple_of(nbig * (4 * sub), 2 * sub)

        @pl.when((nsub & 2) != 0)
        def _():
            run_rows(off2, 2 * sub)
        off1 = pl.multiple_of(off2 + (nsub & 2) * sub, sub)

        @pl.when((nsub & 1) != 0)
        def _():
            run_rows(off1, sub)

    @pl.when(f == nf - 1)
    def _():
        first = item_start[i]

        def issue_row(r):
            src = row_src[first + r]
            scatter_copy(r, src & 1, src >> 1).start()

        def issue_group(t):
            for u in range(grp):
                issue_row(t * grp + u)
        repeat(n // grp, issue_group)
        tail = (n // grp) * grp
        repeat(n - tail, lambda t: issue_row(tail + t))

        @pl.when(i == ni - 1)
        def _():
            wait_scatter(n)


def _moe_plan(route, counts, rows_per_item, max_items):
    ntok = route.shape[0]
    eid = route[:, 0:2].astype(I32)
    rank = route[:, 4:6].astype(I32)
    counts = counts.astype(I32)
    ends = jnp.cumsum(counts)
    starts = ends - counts
    dest = (starts[eid] + rank).reshape(-1)
    src = jnp.arange(2 * ntok, dtype=I32)
    row_src = jnp.zeros((2 * ntok + 8,), I32).at[dest].set(src)
    nit = (counts + rows_per_item - 1) // rows_per_item
    it_end = jnp.cumsum(nit)
    it_first = it_end - nit
    total = it_end[-1]
    t = jnp.arange(max_items, dtype=I32)
    e = jnp.minimum(jnp.searchsorted(it_end, t, side='right'), N_EXPERTS - 1).astype(I32)
    j = t - it_first[e]
    n = jnp.clip(counts[e] - j * rows_per_item, 0, rows_per_item)
    n = jnp.where(t < total, n, 0).astype(I32)
    start = (starts[e] + j * rows_per_item).astype(I32)
    start = jnp.where(n > 0, start, 0)
    last_e = e[jnp.maximum(total - 1, 0)]
    e = jnp.where(t < total, e, last_e)
    return e, start, n, row_src


def _moe(hn, route, counts, wg, wu, wd):
    ntok, d = hn.shape
    ne, _, ff = wg.shape
    rows, sub, nf = MOE_ROWS, MOE_SUB, MOE_FSPLIT
    assert nf > 1 and rows % sub == 0
    tf = ff // nf
    max_items = ne + (2 * ntok) // rows
    item_e, item_start, item_n, row_src = _moe_plan(route, counts, rows, max_items)

    def f_eff(i, f, item_n):
        return jnp.where(item_n[i] > 0, f, nf - 1)

    return pl.pallas_call(
        functools.partial(_moe_kernel, sub=sub),
        out_shape=jax.ShapeDtypeStruct((2, ntok, d), F32),
        grid_spec=pltpu.PrefetchScalarGridSpec(
            num_scalar_prefetch=4,
            grid=(max_items, nf),
            in_specs=[pl.BlockSpec(memory_space=pl.ANY),
                      pl.BlockSpec((1, d, tf), lambda i, f, ie, ist, inn, rs: (ie[i], 0, f_eff(i, f, inn))),
                      pl.BlockSpec((1, d, tf), lambda i, f, ie, ist, inn, rs: (ie[i], 0, f_eff(i, f, inn))),
                      pl.BlockSpec((1, tf, d), lambda i, f, ie, ist, inn, rs: (ie[i], f_eff(i, f, inn), 0))],
            out_specs=pl.BlockSpec(memory_space=pl.ANY),
            scratch_shapes=[pltpu.VMEM((2, rows, d), F32),
                            pltpu.VMEM((rows, d), F32),
                            pltpu.SemaphoreType.DMA,
                            pltpu.SemaphoreType.DMA]),
        compiler_params=_params(("arbitrary", "arbitrary")),
        name="moe_experts",
    )(item_e, item_start, item_n, row_src, hn, wg, wu, wd)


def _ple_kernel(h1_ref, y0_ref, y1_ref, route_ref, pe_ref, gple_ref, wg_ref, wp_ref, gfin_ref,
                o_ref):
    route = route_ref[...]
    y = route[:, 2:3] * y0_ref[0] + route[:, 3:4] * y1_ref[0]
    h2 = h1_ref[...] + y
    ms = jnp.mean(h2 * h2, axis=-1, keepdims=True)
    hn = (h2 * lax.rsqrt(ms + EPS) * gple_ref[...]).astype(BF16)
    gate = jax.nn.sigmoid(jnp.dot(hn, wg_ref[...], preferred_element_type=F32))
    pp = jnp.dot(pe_ref[...].astype(BF16), wp_ref[...], preferred_element_type=F32)
    h3 = h2 + gate * pp
    ms = jnp.mean(h3 * h3, axis=-1, keepdims=True)
    o_ref[...] = h3 * lax.rsqrt(ms + EPS) * gfin_ref[...]


def _ple(h1, yk, row0, route, pe, gple, wg, wp, gfin, tm):
    m, d = h1.shape
    blk0 = row0 // tm
    const = lambda shp: pl.BlockSpec(shp, lambda i: (0,) * len(shp))
    return pl.pallas_call(
        _ple_kernel,
        out_shape=jax.ShapeDtypeStruct((m, d), F32),
        grid=(m // tm,),
        in_specs=[pl.BlockSpec((tm, d), lambda i: (i, 0)),
                  pl.BlockSpec((1, tm, d), lambda i: (0, i + blk0, 0)),
                  pl.BlockSpec((1, tm, d), lambda i: (1, i + blk0, 0)),
                  pl.BlockSpec((tm, LANES), lambda i: (i, 0)),
                  pl.BlockSpec((tm, pe.shape[1]), lambda i: (i, 0)),
                  const((1, d)), const(wg.shape), const(wp.shape), const((1, d))],
        out_specs=pl.BlockSpec((tm, d), lambda i: (i, 0)),
        compiler_params=_params(("arbitrary",)),
        name="ple_final",
    )(h1, yk, yk, route, pe, gple.reshape(1, d), wg, wp, gfin.reshape(1, d))


def _project_group(x, g, w_in, pos, tm, q_scale, k_seq_tiles=0):
    xn = _rmsnorm(x, g, tm)
    da_q = _rope_tables(pos, DA_ROT, DA_HD, ROPE_THETA, q_scale)[:, None]
    da_k = _rope_tables(pos, DA_ROT, DA_HD, ROPE_THETA, 1.0)[:, None]
    rq_t = _rope_tables(pos, RET_KD, RET_KD, RET_THETA, 1.0)
    rk_t = _rope_tables(pos, RET_KD, RET_KD, RET_THETA, RET_KD ** -0.5)
    ret_t = np.stack([rq_t, rk_t], axis=1)
    q = _proj(xn, w_in, COL_Q, 1024, tm, BF16, rope=(da_q, DA_ROT // 2))
    k = _proj(xn, w_in, COL_K, 1024, tm, F32, rope=(da_k, DA_ROT // 2), seq_tiles=k_seq_tiles)
    v = _proj(xn, w_in, COL_V, 1024, tm, F32)
    rqk = _proj(xn, w_in, COL_RQK, 1024, tm, F32, rope=(ret_t, RET_KD // 2))
    rv = _proj(xn, w_in, COL_RV, 1024, tm, F32)
    rg = _proj(xn, w_in, COL_GATES, 1024, tm, F32)
    gagr = _proj(xn, w_in, COL_GATES + 1024, 4096, tm, F32)
    return q, k, v, rqk, rv, rg, gagr


def kernel(x_prompt, x_sample, cache_k, cache_v, state_ret, page_table, p_prompt, p_sample, norm_mix_g, w_in, lam_q1, lam_k1, lam_q2, lam_k2, da_norm_g, ret_norm_g, w_br_attn, w_br_ret, w_out, norm_ffn_g, w_router_group, b_router_group, w_router_expert, b_router_expert, w_exp_gate, w_exp_up, w_exp_down, norm_ple_g, w_ple_gate, w_ple_proj, final_norm_g):
    bp, sp, d = x_prompt.shape
    bs, ts, _ = x_sample.shape
    depth = w_in.shape[0]
    assert depth == 1, "the final norm is fused into the last stage of a single layer"
    n_pool, page = cache_k.shape[1], cache_k.shape[2]
    mp, msm = bp * sp, bs * ts
    log_g = np.log1p(-np.exp2(-5.0 - np.arange(RET_HEADS))).astype(np.float32)
    pos_p = np.arange(sp)
    pos_s = np.tile(PAST_LEN + np.arange(ts), bs)
    cache_k2 = cache_k.transpose(0, 1, 3, 4, 5, 2).reshape(depth * n_pool, DA_HEADS * 2 * DA_HD, page)
    cache_v2 = cache_v.reshape(depth * n_pool, page * DA_HEADS, DA_VD)
    tm_p = next(t for t in (1024, 512, 256, 128) if sp % t == 0)
    tf_p = 256

    hp = x_prompt.reshape(mp, d)
    hs = x_sample.reshape(msm, d)
    kp_l, vp_l, rp_l, ks_l, vs_l, rs_l = [], [], [], [], [], []
    for li in range(depth):
        lam_init = 0.8 - 0.6 * math.exp(-0.3 * li)
        lam = (jnp.exp(jnp.sum(lam_q1[li] * lam_k1[li]))
               - jnp.exp(jnp.sum(lam_q2[li] * lam_k2[li])) + lam_init).reshape(1).astype(F32)
        out_scale = 1.0 - lam_init
        wa = w_br_attn[li].astype(BF16)
        wr = w_br_ret[li].astype(BF16)
        wo = w_out[li].astype(BF16)
        wpg = w_ple_gate[li].astype(BF16)
        wpp = w_ple_proj[li].astype(BF16)
        wrt = jnp.zeros((d, LANES), F32)
        wrt = wrt.at[:, :N_EXPERTS].set(w_router_expert[li])
        wrt = wrt.at[:, N_EXPERTS:N_EXPERTS + N_GROUPS].set(w_router_group[li]).astype(BF16)
        brt = jnp.zeros((1, LANES), F32)
        brt = brt.at[0, :N_EXPERTS].set(b_router_expert[li])
        brt = brt.at[0, N_EXPERTS:N_EXPERTS + N_GROUPS].set(b_router_group[li])
        da_g = da_norm_g[li].reshape(1, -1)

        q, k, v, rqk, rv, rg, gagr = _project_group(hp, norm_mix_g[li], w_in[li], pos_p, tm_p,
                                                    DA_HD ** -0.5 * math.log2(math.e),
                                                    k_seq_tiles=sp // tm_p)
        a_p = _attn_prompt(lam, q, k, v, da_g, bp, sp, out_scale, tq=256, hp=4)
        chunk = 256 if sp % 256 == 0 else 128
        r_p, st_p = _retention(rqk.reshape(bp, sp, -1), rv.reshape(bp, sp, -1),
                               rg.reshape(bp, sp, -1),
                               jnp.zeros((bp, RET_HEADS, RET_KD, RET_VD), F32),
                               ret_norm_g[li], log_g, chunk, chunk)
        kp_l.append(k.reshape(bp, DA_HEADS, 2, DA_HD, sp).transpose(0, 4, 1, 2, 3))
        vp_l.append(v.reshape(bp, sp, DA_HEADS, DA_VD))
        rp_l.append(st_p)
        h1_p, hn_p, route_p, cnt_p = _finish(
            a_p, r_p.reshape(mp, -1), gagr, hp, wa, wr, wo, norm_ffn_g[li], wrt, brt,
            jnp.zeros((1, LANES), F32), tf_p)

        q, k, v, rqk, rv, rg, gagr = _project_group(hs, norm_mix_g[li], w_in[li], pos_s, msm,
                                                    DA_HD ** -0.5)
        tpad = 8
        pad_t = lambda z: jnp.pad(z.reshape(bs, ts, -1), ((0, 0), (0, tpad - ts), (0, 0)))
        q4 = q.astype(F32).reshape(bs, ts, DA_HEADS, 2, DA_HD)
        eye = jnp.eye(2, dtype=F32)
        qm = jnp.einsum('bthmd,mn->bhmtnd', q4, eye).reshape(bs, DA_HEADS * 2 * ts, 2 * DA_HD)
        a_s = _attn_paged(page_table, lam, qm, pad_t(k), pad_t(v), da_g, cache_k2, cache_v2,
                          li * n_pool, ts, out_scale, npg=8)
        r_s, st_s = _retention(pad_t(rqk), pad_t(rv), pad_t(rg), state_ret[li],
                               ret_norm_g[li], log_g, tpad, ts)
        ks_l.append(k.reshape(bs, ts, DA_HEADS, 2, DA_HD))
        vs_l.append(v.reshape(bs, ts, DA_HEADS, DA_VD))
        rs_l.append(st_s)
        h1_s, hn_s, route_s, cnt = _finish(
            a_s.reshape(msm, -1), r_s[:, :ts].reshape(msm, -1), gagr, hs, wa, wr, wo,
            norm_ffn_g[li], wrt, brt, cnt_p, msm)

        hn_all = jnp.concatenate([hn_p, hn_s], axis=0)
        route_all = jnp.concatenate([route_p, route_s], axis=0)
        yk = _moe(hn_all, route_all, cnt[0, :N_EXPERTS], w_exp_gate[li], w_exp_up[li],
                  w_exp_down[li])

        gfin = final_norm_g
        hp = _ple(h1_p, yk, 0, route_p, p_prompt[li].reshape(mp, -1), norm_ple_g[li], wpg, wpp,
                  gfin, tf_p)
        hs = _ple(h1_s, yk, mp, route_s, p_sample[li].reshape(msm, -1), norm_ple_g[li], wpg, wpp,
                  gfin, msm)
    y_prompt = hp.reshape(bp, sp, d)
    y_sample = hs.reshape(bs, ts, d)
    return (y_prompt, y_sample, jnp.stack(kp_l), jnp.stack(vp_l), jnp.stack(rp_l),
            jnp.stack(ks_l), jnp.stack(vs_l), jnp.stack(rs_l))
```

```python
import functools
import math

import jax
import jax.numpy as jnp
import numpy as np
from jax import lax
from jax.experimental import pallas as pl
from jax.experimental.pallas import tpu as pltpu

F32 = jnp.float32
BF16 = jnp.bfloat16
I32 = jnp.int32

PAST_LEN = 16384
DA_HEADS = 8
DA_VD = 128
DA_HD = 64
DA_ROT = 16
ROPE_THETA = 500000.0
RET_HEADS = 8
RET_VD = 128
RET_KD = 64
RET_THETA = 10000.0
N_GROUPS = 4
EXPERTS_PER_GROUP = 8
N_EXPERTS = N_GROUPS * EXPERTS_PER_GROUP
EPS = 1e-6
NEG = -1e30
LANES = 128

VMEM_LIMIT = 56 * 1024 * 1024

COL_Q, COL_K, COL_V, COL_RQK, COL_RV, COL_GATES = 0, 1024, 2048, 3072, 4096, 5120

MOE_ROWS = 768
MOE_SUB = 128
MOE_FSPLIT = 2


def _params(sem, vmem=VMEM_LIMIT):
    return pltpu.CompilerParams(dimension_semantics=sem, vmem_limit_bytes=vmem)


def _rmsnorm_kernel(x_ref, g_ref, o_ref):
    x = x_ref[...]
    ms = jnp.mean(x * x, axis=-1, keepdims=True)
    o_ref[...] = (x * lax.rsqrt(ms + EPS) * g_ref[...]).astype(o_ref.dtype)


def _rmsnorm(x, g, tm):
    m, d = x.shape
    return pl.pallas_call(
        _rmsnorm_kernel,
        out_shape=jax.ShapeDtypeStruct((m, d), BF16),
        grid=(m // tm,),
        in_specs=[pl.BlockSpec((tm, d), lambda i: (i, 0)),
                  pl.BlockSpec((1, d), lambda i: (0, 0))],
        out_specs=pl.BlockSpec((tm, d), lambda i: (i, 0)),
        compiler_params=_params(("arbitrary",)),
        name="rmsnorm",
    )(x, g.reshape(1, d))


def _proj_kernel(*refs, shift, tn, seq_tiles):
    if shift:
        xn_ref, w_ref, c_ref, s1_ref, s2_ref, o_ref, wbf_ref = refs
    else:
        xn_ref, w_ref, o_ref, wbf_ref = refs

    @pl.when(pl.program_id(1) == 0)
    def _():
        wbf_ref[...] = w_ref[...].astype(BF16)

    acc = jnp.dot(xn_ref[...], wbf_ref[...], preferred_element_type=F32)
    if shift:
        groups = c_ref.shape[0]
        rep = tn // LANES // groups
        wide = lambda ref: jnp.concatenate(
            [jnp.tile(ref[g], (1, rep)) for g in range(groups)], axis=1)
        acc = (acc * wide(c_ref) + pltpu.roll(acc, tn - shift, 1) * wide(s1_ref)
               + pltpu.roll(acc, shift, 1) * wide(s2_ref))
    if seq_tiles:
        o_ref[0] = acc.T.astype(o_ref.dtype)
    else:
        o_ref[...] = acc.astype(o_ref.dtype)


def _proj(xn, w_in, col0, ncols, tm, out_dtype, rope=None, seq_tiles=0):
    m, d = xn.shape
    tn = 1024
    nj = ncols // tn
    j0 = col0 // tn
    in_specs = [pl.BlockSpec((tm, d), lambda j, i: (i, 0)),
                pl.BlockSpec((d, tn), lambda j, i: (0, j + j0))]
    args = [xn, w_in]
    shift = 0
    if rope is not None:
        assert nj == 1
        tabs, shift = rope
        groups, npos = tabs.shape[1], tabs.shape[2]
        npb = npos // tm
        for t in range(3):
            in_specs.append(pl.BlockSpec((groups, tm, LANES), lambda j, i: (0, i % npb, 0)))
            args.append(tabs[t])
    if seq_tiles:
        out_shape = jax.ShapeDtypeStruct((m // (seq_tiles * tm), ncols, seq_tiles * tm), out_dtype)
        out_spec = pl.BlockSpec((1, tn, tm), lambda j, i: (i // seq_tiles, j, i % seq_tiles))
    else:
        out_shape = jax.ShapeDtypeStruct((m, ncols), out_dtype)
        out_spec = pl.BlockSpec((tm, tn), lambda j, i: (i, j))
    return pl.pallas_call(
        functools.partial(_proj_kernel, shift=shift, tn=tn, seq_tiles=seq_tiles),
        out_shape=out_shape,
        grid=(nj, m // tm),
        in_specs=in_specs,
        out_specs=out_spec,
        scratch_shapes=[pltpu.VMEM((d, tn), BF16)],
        compiler_params=_params(("arbitrary", "arbitrary")),
        name="in_proj",
    )(*args)


def _rope_tables(pos, rot_dim, period, theta, scale):
    half = rot_dim // 2
    inv = (1.0 / np.power(np.float32(theta), np.arange(half, dtype=np.float32)
                          * np.float32(2.0 / rot_dim))).astype(np.float32)
    ang = (pos.astype(np.float32)[:, None] * inv[None, :]).astype(np.float64)
    cos, sin = np.cos(ang), np.sin(ang)
    npos = pos.shape[0]
    pad = period - rot_dim
    c = np.concatenate([cos, cos, np.ones((npos, pad))], axis=1)
    s1 = np.concatenate([-sin, np.zeros((npos, half + pad))], axis=1)
    s2 = np.concatenate([np.zeros((npos, half)), sin, np.zeros((npos, pad))], axis=1)
    tabs = np.stack([c, s1, s2]).astype(np.float32) * np.float32(scale)
    return np.tile(tabs, (1, 1, LANES // period)).astype(np.float32)


def _attn_prompt_kernel(lam_ref, q_ref, k_ref, v_ref, g_ref, o_ref,
                        kb_ref, vt_ref, m_ref, acc_ref, *, tq, hp, out_scale):
    qi = pl.program_id(2)
    nblk = kb_ref.shape[1]
    ext = vt_ref.shape[2]

    @pl.when(qi == 0)
    def _():
        for c in range(nblk):
            for h in range(hp):
                kb_ref[h, c] = k_ref[0, h * LANES:(h + 1) * LANES,
                                     c * tq:(c + 1) * tq].T.astype(BF16)

        def stage(c, carry):
            off = pl.multiple_of(c * tq, tq)
            for h in range(hp):
                cols = slice(h * LANES, (h + 1) * LANES)
                vt_ref[h, c, :LANES, :] = v_ref[pl.ds(off, tq), cols].T.astype(BF16)
                vt_ref[h, c, LANES:, :] = jnp.ones((ext - LANES, tq), BF16)
            return carry
        lax.fori_loop(0, nblk, stage, 0)

    row = lax.broadcasted_iota(I32, (LANES, tq), 0)
    qqt = []
    for h in range(hp):
        qt = q_ref[:, h * LANES:(h + 1) * LANES].astype(F32).T
        qqt.append(jnp.concatenate([jnp.where(row < DA_HD, qt, 0.0),
                                    jnp.where(row >= DA_HD, qt, 0.0)], axis=1).astype(BF16))
    m_ref[...] = jnp.full(m_ref.shape, NEG, F32)
    acc_ref[...] = jnp.zeros(acc_ref.shape, F32)

    def step(j, masked):
        sts = [jnp.dot(kb_ref[h, j], qqt[h], preferred_element_type=F32)
               for h in range(hp)]
        for h in range(hp):
            st = sts[h]
            if masked:
                key = lax.broadcasted_iota(I32, (tq, 2 * tq), 0)
                qry = lax.broadcasted_iota(I32, (tq, 2 * tq), 1) & (tq - 1)
                st = jnp.where(key <= qry, st, NEG)
            m_old = m_ref[h]
            m_new = jnp.maximum(m_old, jnp.max(st, axis=0, keepdims=True))
            alpha = jnp.exp2(m_old - m_new)
            p = jnp.exp2(st - m_new)
            acc_ref[h] = alpha * acc_ref[h] + jnp.dot(vt_ref[h, j], p.astype(BF16),
                                                      preferred_element_type=F32)
            m_ref[h] = m_new

    def body(j, carry):
        step(j, False)
        return carry

    lax.fori_loop(0, qi, body, 0)
    step(qi, True)

    for h in range(hp):
        cols = slice(h * LANES, (h + 1) * LANES)
        o = acc_ref[h, :LANES, :] / acc_ref[h, LANES:LANES + 1, :]
        out = (o[:, :tq] - lam_ref[0] * o[:, tq:]).T
        ms = jnp.mean(out * out, axis=-1, keepdims=True)
        out = out * lax.rsqrt(ms + EPS) * g_ref[:, cols] * out_scale
        o_ref[:, cols] = out.astype(o_ref.dtype)


def _attn_prompt(lam, q, k, v, g, b, s, out_scale, tq, hp):
    nq = s // tq
    w = hp * LANES
    return pl.pallas_call(
        functools.partial(_attn_prompt_kernel, tq=tq, hp=hp, out_scale=out_scale),
        out_shape=jax.ShapeDtypeStruct((b * s, DA_HEADS * DA_VD), BF16),
        grid=(b, DA_HEADS // hp, nq),
        in_specs=[pl.BlockSpec(memory_space=pltpu.SMEM),
                  pl.BlockSpec((tq, w), lambda bi, h, qi: (bi * nq + qi, h)),
                  pl.BlockSpec((1, w, s), lambda bi, h, qi: (bi, h, 0)),
                  pl.BlockSpec((s, w), lambda bi, h, qi: (bi, h)),
                  pl.BlockSpec((1, w), lambda bi, h, qi: (0, h))],
        out_specs=pl.BlockSpec((tq, w), lambda bi, h, qi: (bi * nq + qi, h)),
        scratch_shapes=[pltpu.VMEM((hp, nq, tq, LANES), BF16),
                        pltpu.VMEM((hp, nq, LANES + 16, tq), BF16),
                        pltpu.VMEM((hp, 1, 2 * tq), F32),
                        pltpu.VMEM((hp, LANES + 16, 2 * tq), F32)],
        compiler_params=_params(("arbitrary", "arbitrary", "arbitrary")),
        name="attn_prompt",
    )(lam, q, k, v, g)


def _attn_paged_kernel(pt_ref, lam_ref, q_ref, kn_ref, vn_ref, g_ref, *rest,
                       npg, t, out_scale):
    k_refs = rest[:npg]
    v_refs = rest[npg:2 * npg]
    o_ref, m_ref, l_ref, acc_ref = rest[2 * npg:]
    c = pl.program_id(1)
    nc = pl.num_programs(1)
    rows = 2 * t
    page = k_refs[0].shape[2]

    @pl.when(c == 0)
    def _():
        m_ref[...] = jnp.full(m_ref.shape, NEG, F32)
        l_ref[...] = jnp.zeros(l_ref.shape, F32)
        acc_ref[...] = jnp.zeros(acc_ref.shape, F32)

    def update(scores, values):
        s_all = jnp.concatenate(scores, axis=0)
        m_old = m_ref[...]
        m_new = jnp.maximum(m_old, jnp.max(s_all, axis=-1, keepdims=True))
        alpha = jnp.exp(m_old - m_new)
        p = jnp.exp(s_all - m_new)
        l_ref[...] = alpha * l_ref[...] + jnp.sum(p, axis=-1, keepdims=True)
        m_ref[...] = m_new
        pb = p
        for h in range(DA_HEADS):
            r0 = h * rows
            pv = None
            for lo, hi, load_v in values[h]:
                d = jnp.dot(pb[r0:r0 + rows, lo:hi], load_v(), preferred_element_type=F32)
                pv = d if pv is None else pv + d
            acc_ref[r0:r0 + rows, :] = alpha[r0:r0 + rows] * acc_ref[r0:r0 + rows, :] + pv

    q = q_ref[0]
    scores, values = [], []
    for h in range(DA_HEADS):
        qh = q[h * rows:(h + 1) * rows]
        sh, vh = [], []
        for i in range(npg):
            kb = k_refs[i][0, h * LANES:(h + 1) * LANES, :]
            sh.append(jnp.dot(qh, kb, preferred_element_type=F32))
            vh.append((i * page, (i + 1) * page,
                       lambda i=i, h=h: v_refs[i][0, pl.ds(h, page, stride=DA_HEADS), :]))
        scores.append(jnp.concatenate(sh, axis=1))
        values.append(vh)
    update(scores, values)

    @pl.when(c == nc - 1)
    def _():
        tp = kn_ref.shape[1]
        scores, values = [], []
        row = lax.broadcasted_iota(I32, (rows, tp), 0) & (t - 1)
        col = lax.broadcasted_iota(I32, (rows, tp), 1)
        for h in range(DA_HEADS):
            qh = q[h * rows:(h + 1) * rows]
            kb = kn_ref[0, :, h * LANES:(h + 1) * LANES]
            s = lax.dot_general(qh, kb, (((1,), (1,)), ((), ())), preferred_element_type=F32)
            scores.append(jnp.where(col <= row, s, NEG))
            values.append([(0, tp, lambda h=h: vn_ref[0, :, h * LANES:(h + 1) * LANES])])
        update(scores, values)
        o = acc_ref[...] / l_ref[...]
        lam = lam_ref[0]
        for h in range(DA_HEADS):
            r0 = h * rows
            out = o[r0:r0 + t] - lam * o[r0 + t:r0 + rows]
            ms = jnp.mean(out * out, axis=-1, keepdims=True)
            gh = g_ref[:, h * LANES:(h + 1) * LANES]
            o_ref[0, :, h * LANES:(h + 1) * LANES] = (
                out * lax.rsqrt(ms + EPS) * gh * out_scale).astype(o_ref.dtype)


def _attn_paged(page_table, lam, qm, k_new, v_new, g, cache_k, cache_v, page0, t, out_scale, npg):
    db, n_pages = page_table.shape
    prow, pcol = cache_k.shape[1], cache_k.shape[2]
    assert cache_v.shape[1:] == (prow, pcol)
    w = k_new.shape[2]
    tp = k_new.shape[1]
    nc = n_pages // npg
    pt_flat = page_table.reshape(-1) + page0

    def page_spec(i):
        return pl.BlockSpec((1, prow, pcol),
                            lambda b, c, pt: (pt[b * n_pages + c * npg + i], 0, 0))

    in_specs = [pl.BlockSpec(memory_space=pltpu.SMEM),
                pl.BlockSpec((1, qm.shape[1], LANES), lambda b, c, pt: (b, 0, 0)),
                pl.BlockSpec((1, tp, w), lambda b, c, pt: (b, 0, 0)),
                pl.BlockSpec((1, tp, w), lambda b, c, pt: (b, 0, 0)),
                pl.BlockSpec((1, w), lambda b, c, pt: (0, 0))]
    in_specs += [page_spec(i) for i in range(npg)] * 2
    rows = DA_HEADS * 2 * t
    return pl.pallas_call(
        functools.partial(_attn_paged_kernel, npg=npg, t=t, out_scale=out_scale),
        out_shape=jax.ShapeDtypeStruct((db, t, w), BF16),
        grid_spec=pltpu.PrefetchScalarGridSpec(
            num_scalar_prefetch=1,
            grid=(db, nc),
            in_specs=in_specs,
            out_specs=pl.BlockSpec((1, t, w), lambda b, c, pt: (b, 0, 0)),
            scratch_shapes=[pltpu.VMEM((rows, 1), F32),
                            pltpu.VMEM((rows, 1), F32),
                            pltpu.VMEM((rows, LANES), F32)]),
        compiler_params=_params(("arbitrary", "arbitrary")),
        name="attn_paged",
    )(pt_flat, lam, qm, k_new, v_new, g, *([cache_k] * npg), *([cache_v] * npg))


def _ret_kernel(rq_ref, rk_ref, rv_ref, rg_ref, s0_ref, dmat_ref, dq_ref, dk_ref, gc_ref, g_ref,
                o_ref, sout_ref, st_ref):
    c = pl.program_id(1)

    @pl.when(c == 0)
    def _():
        st_ref[...] = s0_ref[0]

    for h in range(RET_HEADS):
        kq = slice(h * RET_KD, (h + 1) * RET_KD)
        vs = slice(h * RET_VD, (h + 1) * RET_VD)
        qb = rq_ref[0, :, kq].astype(BF16)
        k = rk_ref[0, :, kq]
        vb = rv_ref[0, :, vs].astype(BF16)
        st = st_ref[h]
        att = lax.dot_general(qb, k.astype(BF16), (((1,), (1,)), ((), ())),
                              preferred_element_type=F32) * dmat_ref[h]
        inner = jnp.dot(att.astype(BF16), vb, preferred_element_type=F32)
        cross = jnp.dot(qb, st.astype(BF16), preferred_element_type=F32) * dq_ref[h]
        r = inner + cross
        kd = (k * dk_ref[h]).astype(BF16)
        st_ref[h] = gc_ref[h] * st + lax.dot_general(
            kd, vb, (((0,), (0,)), ((), ())), preferred_element_type=F32)
        mu = jnp.mean(r, axis=-1, keepdims=True)
        xc = r - mu
        var = jnp.mean(xc * xc, axis=-1, keepdims=True)
        y = xc * lax.rsqrt(var + EPS) * g_ref[:, vs]
        rg = rg_ref[0, :, vs]
        o_ref[0, :, vs] = (rg * jax.nn.sigmoid(rg) * y).astype(o_ref.dtype)

    @pl.when(c == pl.num_programs(1) - 1)
    def _():
        sout_ref[0] = st_ref[...]


def _ret_tables(log_g, chunk, valid):
    f32 = np.float32
    idx = np.arange(chunk, dtype=f32)
    diff = idx[:, None] - idx[None, :]
    ex = lambda e: np.exp(e.astype(f32).astype(np.float64)).astype(f32)
    dmat = np.where(diff[None] >= 0, ex(np.maximum(diff, f32(0))[None] * log_g[:, None, None]),
                    f32(0))
    dq = ex((idx + f32(1))[None, :] * log_g[:, None])
    dk = np.where(idx[None, :] < valid,
                  ex((f32(valid - 1) - idx)[None, :] * log_g[:, None]), f32(0))
    gc = ex(f32(valid) * log_g)
    h = log_g.shape[0]
    return (dmat.astype(f32),
            np.ascontiguousarray(np.broadcast_to(dq[:, :, None], (h, chunk, RET_VD))).astype(f32),
            np.ascontiguousarray(np.broadcast_to(dk[:, :, None], (h, chunk, RET_KD))).astype(f32),
            np.ascontiguousarray(np.broadcast_to(gc[:, None, None], (h, 1, RET_VD))).astype(f32))


def _retention(rqk, rv, gates, state0, g, log_g, chunk, valid):
    b, s, _ = rqk.shape
    nc = s // chunk
    dmat, dq, dk, gc = _ret_tables(log_g, chunk, valid)
    qw = RET_HEADS * RET_KD
    vw = RET_HEADS * RET_VD
    full = lambda shp: pl.BlockSpec(shp, lambda bi, c: (0,) * len(shp))
    return pl.pallas_call(
        _ret_kernel,
        out_shape=(jax.ShapeDtypeStruct((b, s, vw), BF16),
                   jax.ShapeDtypeStruct((b, RET_HEADS, RET_KD, RET_VD), F32)),
        grid=(b, nc),
        in_specs=[pl.BlockSpec((1, chunk, qw), lambda bi, c: (bi, c, 0)),
                  pl.BlockSpec((1, chunk, qw), lambda bi, c: (bi, c, 1)),
                  pl.BlockSpec((1, chunk, vw), lambda bi, c: (bi, c, 0)),
                  pl.BlockSpec((1, chunk, vw), lambda bi, c: (bi, c, 0)),
                  pl.BlockSpec((1, RET_HEADS, RET_KD, RET_VD), lambda bi, c: (bi, 0, 0, 0)),
                  full(dmat.shape), full(dq.shape), full(dk.shape), full(gc.shape),
                  full((1, vw))],
        out_specs=(pl.BlockSpec((1, chunk, vw), lambda bi, c: (bi, c, 0)),
                   pl.BlockSpec((1, RET_HEADS, RET_KD, RET_VD), lambda bi, c: (bi, 0, 0, 0))),
        scratch_shapes=[pltpu.VMEM((RET_HEADS, RET_KD, RET_VD), F32)],
        compiler_params=_params(("arbitrary", "arbitrary")),
        name="retention",
    )(rqk, rqk, rv, gates, state0, dmat, dq, dk, gc, g.reshape(1, vw))


def _finish_kernel(a_ref, r_ref, ga_ref, gr_ref, h_ref, wa_ref, wr_ref, wo_ref, gffn_ref,
                   wrt_ref, brt_ref, carry0_ref,
                   h1_ref, hn_ref, route_ref, cnt_ref, carry_ref, *, tm):
    i = pl.program_id(0)

    @pl.when(i == 0)
    def _():
        carry_ref[...] = carry0_ref[...]

    am = jnp.dot(a_ref[...], wa_ref[...], preferred_element_type=F32)
    rm = jnp.dot(r_ref[...], wr_ref[...], preferred_element_type=F32)
    mix = jax.nn.sigmoid(ga_ref[...]) * am + jax.nn.sigmoid(gr_ref[...]) * rm
    h1 = h_ref[...] + jnp.dot(mix.astype(BF16), wo_ref[...], preferred_element_type=F32)
    h1_ref[...] = h1
    ms = jnp.mean(h1 * h1, axis=-1, keepdims=True)
    hn = h1 * lax.rsqrt(ms + EPS) * gffn_ref[...]
    hn_ref[...] = hn
    logits = jnp.dot(hn.astype(BF16), wrt_ref[...], preferred_element_type=F32) + brt_ref[...]
    lane = lax.broadcasted_iota(I32, (tm, LANES), 1)
    is_g = (lane >= N_EXPERTS) & (lane < N_EXPERTS + N_GROUPS)
    glog = jnp.where(is_g, logits, -jnp.inf)
    gmax = jnp.max(glog, axis=-1, keepdims=True)
    gidx = jnp.min(jnp.where(glog == gmax, lane - N_EXPERTS, LANES), axis=-1, keepdims=True)
    gden = jnp.sum(jnp.where(is_g, jnp.exp(glog - gmax), 0.0), axis=-1, keepdims=True)
    gp = 1.0 / gden
    in_g = (lane < N_EXPERTS) & ((lane // EXPERTS_PER_GROUP) == gidx)
    e1 = jnp.where(in_g, logits, -jnp.inf)
    v1 = jnp.max(e1, axis=-1, keepdims=True)
    i1 = jnp.min(jnp.where(e1 == v1, lane, LANES), axis=-1, keepdims=True)
    e2 = jnp.where(lane == i1, -jnp.inf, e1)
    v2 = jnp.max(e2, axis=-1, keepdims=True)
    i2 = jnp.min(jnp.where(e2 == v2, lane, LANES), axis=-1, keepdims=True)
    tt = jnp.exp(v2 - v1)
    w1 = gp / (1.0 + tt)
    w2 = gp * tt / (1.0 + tt)
    oh = jnp.where((lane == i1) | (lane == i2), 1.0, 0.0)
    rr = lax.broadcasted_iota(I32, (tm, tm), 0)
    cc = lax.broadcasted_iota(I32, (tm, tm), 1)
    tri = jnp.where(cc < rr, 1.0, 0.0).astype(BF16)
    before = jnp.dot(tri, oh.astype(BF16), preferred_element_type=F32) + carry_ref[...]
    rank1 = jnp.sum(jnp.where(lane == i1, before, 0.0), axis=-1, keepdims=True)
    rank2 = jnp.sum(jnp.where(lane == i2, before, 0.0), axis=-1, keepdims=True)
    carry = carry_ref[...] + jnp.sum(oh, axis=0, keepdims=True)
    carry_ref[...] = carry
    cnt_ref[...] = carry
    cols = [i1.astype(F32), i2.astype(F32), w1, w2, rank1, rank2]
    route = jnp.zeros((tm, LANES), F32)
    for ci, val in enumerate(cols):
        route = jnp.where(lane == ci, val, route)
    route_ref[...] = route


def _finish(a, r, gagr, h, wa, wr, wo, gffn, wrt, brt, carry0, tm):
    m, d = h.shape
    aw = a.shape[1]
    assert gagr.shape[1] == 2 * d
    const = lambda shp: pl.BlockSpec(shp, lambda i: (0,) * len(shp))
    return pl.pallas_call(
        functools.partial(_finish_kernel, tm=tm),
        out_shape=(jax.ShapeDtypeStruct((m, d), F32),
                   jax.ShapeDtypeStruct((m, d), F32),
                   jax.ShapeDtypeStruct((m, LANES), F32),
                   jax.ShapeDtypeStruct((1, LANES), F32)),
        grid=(m // tm,),
        in_specs=[pl.BlockSpec((tm, aw), lambda i: (i, 0)),
                  pl.BlockSpec((tm, aw), lambda i: (i, 0)),
                  pl.BlockSpec((tm, d), lambda i: (i, 0)),
                  pl.BlockSpec((tm, d), lambda i: (i, 1)),
                  pl.BlockSpec((tm, d), lambda i: (i, 0)),
                  const(wa.shape), const(wr.shape), const(wo.shape), const((1, d)),
                  const(wrt.shape), const((1, LANES)), const((1, LANES))],
        out_specs=(pl.BlockSpec((tm, d), lambda i: (i, 0)),
                   pl.BlockSpec((tm, d), lambda i: (i, 0)),
                   pl.BlockSpec((tm, LANES), lambda i: (i, 0)),
                   pl.BlockSpec((1, LANES), lambda i: (0, 0))),
        scratch_shapes=[pltpu.VMEM((1, LANES), F32)],
        compiler_params=_params(("arbitrary",)),
        name="finish",
    )(a, r, gagr, gagr, h, wa, wr, wo, gffn.reshape(1, d), wrt, brt, carry0)


def _moe_kernel(item_e, item_start, item_n, row_src,
                hn_hbm, wg_ref, wu_ref, wd_ref, yk_hbm,
                xg, yacc, gsem, ssem, *, sub):
    i = pl.program_id(0)
    f = pl.program_id(1)
    ni = pl.num_programs(0)
    nf = pl.num_programs(1)
    n = item_n[i]
    slot = lax.rem(i, 2)

    grp = 8

    def gather_copy(s, r, tok, rows=1):
        return pltpu.make_async_copy(hn_hbm.at[pl.ds(tok, rows)], xg.at[s, pl.ds(r, rows)], gsem)

    def scatter_copy(r, k, tok, rows=1):
        return pltpu.make_async_copy(yacc.at[pl.ds(r, rows)], yk_hbm.at[k, pl.ds(tok, rows)], ssem)

    def repeat(count, fn):
        def body(t, c):
            fn(t)
            return c
        lax.fori_loop(0, count, body, 0)

    def start_gather(item, s):
        first = item_start[item]

        def issue(t):
            for u in range(grp):
                r = t * grp + u
                gather_copy(s, r, row_src[first + r] >> 1).start()
        repeat((item_n[item] + grp - 1) // grp, issue)

    def wait_gather(count):
        repeat((count + grp - 1) // grp, lambda t: gather_copy(0, 0, 0, grp).wait())

    def wait_scatter(count):
        repeat(count // grp, lambda t: scatter_copy(0, 0, 0, grp).wait())
        repeat(lax.rem(count, grp), lambda t: scatter_copy(0, 0, 0).wait())

    @pl.when(f == 0)
    def _():
        @pl.when(i == 0)
        def _():
            xg[...] = jnp.zeros(xg.shape, xg.dtype)
            start_gather(0, 0)

        @pl.when(i > 0)
        def _():
            wait_scatter(item_n[i - 1])
        wait_gather(n)

    @pl.when((f == 1) & (i + 1 < ni))
    def _():
        start_gather(i + 1, 1 - slot)

    @pl.when(n > 0)
    def _():
        def run_rows(off, m):
            x = xg[slot, pl.ds(off, m), :]
            hg = jnp.dot(x, wg_ref[0], preferred_element_type=F32)
            hu = jnp.dot(x, wu_ref[0], preferred_element_type=F32)
            hm = hg * jax.nn.sigmoid(hg) * hu
            part = jnp.dot(hm, wd_ref[0], preferred_element_type=F32)

            @pl.when(f == 0)
            def _():
                yacc[pl.ds(off, m), :] = part

            @pl.when(f > 0)
            def _():
                yacc[pl.ds(off, m), :] += part

        nsub = (n + sub - 1) // sub
        nbig = nsub // 4

        def big(s, c):
            run_rows(pl.multiple_of(s * (4 * sub), 4 * sub), 4 * sub)
            return c
        lax.fori_loop(0, nbig, big, 0)
        off2 = pl.multiple_of(nbig * (4 * sub), 2 * sub)

        @pl.when((nsub & 2) != 0)
        def _():
            run_rows(off2, 2 * sub)
        off1 = pl.multiple_of(off2 + (nsub & 2) * sub, sub)

        @pl.when((nsub & 1) != 0)
        def _():
            run_rows(off1, sub)

    @pl.when(f == nf - 1)
    def _():
        first = item_start[i]

        def issue_row(r):
            src = row_src[first + r]
            scatter_copy(r, src & 1, src >> 1).start()

        def issue_group(t):
            for u in range(grp):
                issue_row(t * grp + u)
        repeat(n // grp, issue_group)
        tail = (n // grp) * grp
        repeat(n - tail, lambda t: issue_row(tail + t))

        @pl.when(i == ni - 1)
        def _():
            wait_scatter(n)


def _moe_plan(route, counts, rows_per_item, max_items):
    ntok = route.shape[0]
    eid = route[:, 0:2].astype(I32)
    rank = route[:, 4:6].astype(I32)
    counts = counts.astype(I32)
    ends = jnp.cumsum(counts)
    starts = ends - counts
    dest = (starts[eid] + rank).reshape(-1)
    src = jnp.arange(2 * ntok, dtype=I32)
    row_src = jnp.zeros((2 * ntok + 8,), I32).at[dest].set(src)
    nit = (counts + rows_per_item - 1) // rows_per_item
    it_end = jnp.cumsum(nit)
    it_first = it_end - nit
    total = it_end[-1]
    t = jnp.arange(max_items, dtype=I32)
    e = jnp.minimum(jnp.searchsorted(it_end, t, side='right'), N_EXPERTS - 1).astype(I32)
    j = t - it_first[e]
    n = jnp.clip(counts[e] - j * rows_per_item, 0, rows_per_item)
    n = jnp.where(t < total, n, 0).astype(I32)
    start = (starts[e] + j * rows_per_item).astype(I32)
    start = jnp.where(n > 0, start, 0)
    last_e = e[jnp.maximum(total - 1, 0)]
    e = jnp.where(t < total, e, last_e)
    return e, start, n, row_src


def _moe(hn, route, counts, wg, wu, wd):
    ntok, d = hn.shape
    ne, _, ff = wg.shape
    rows, sub, nf = MOE_ROWS, MOE_SUB, MOE_FSPLIT
    assert nf > 1 and rows % sub == 0
    tf = ff // nf
    max_items = ne + (2 * ntok) // rows
    item_e, item_start, item_n, row_src = _moe_plan(route, counts, rows, max_items)

    def f_eff(i, f, item_n):
        return jnp.where(item_n[i] > 0, f, nf - 1)

    return pl.pallas_call(
        functools.partial(_moe_kernel, sub=sub),
        out_shape=jax.ShapeDtypeStruct((2, ntok, d), F32),
        grid_spec=pltpu.PrefetchScalarGridSpec(
            num_scalar_prefetch=4,
            grid=(max_items, nf),
            in_specs=[pl.BlockSpec(memory_space=pl.ANY),
                      pl.BlockSpec((1, d, tf), lambda i, f, ie, ist, inn, rs: (ie[i], 0, f_eff(i, f, inn))),
                      pl.BlockSpec((1, d, tf), lambda i, f, ie, ist, inn, rs: (ie[i], 0, f_eff(i, f, inn))),
                      pl.BlockSpec((1, tf, d), lambda i, f, ie, ist, inn, rs: (ie[i], f_eff(i, f, inn), 0))],
            out_specs=pl.BlockSpec(memory_space=pl.ANY),
            scratch_shapes=[pltpu.VMEM((2, rows, d), F32),
                            pltpu.VMEM((rows, d), F32),
                            pltpu.SemaphoreType.DMA,
                            pltpu.SemaphoreType.DMA]),
        compiler_params=_params(("arbitrary", "arbitrary")),
        name="moe_experts",
    )(item_e, item_start, item_n, row_src, hn, wg, wu, wd)


def _ple_kernel(h1_ref, y0_ref, y1_ref, route_ref, pe_ref, gple_ref, wg_ref, wp_ref, gfin_ref,
                o_ref):
    route = route_ref[...]
    y = route[:, 2:3] * y0_ref[0] + route[:, 3:4] * y1_ref[0]
    h2 = h1_ref[...] + y
    ms = jnp.mean(h2 * h2, axis=-1, keepdims=True)
    hn = (h2 * lax.rsqrt(ms + EPS) * gple_ref[...]).astype(BF16)
    gate = jax.nn.sigmoid(jnp.dot(hn, wg_ref[...], preferred_element_type=F32))
    pp = jnp.dot(pe_ref[...].astype(BF16), wp_ref[...], preferred_element_type=F32)
    h3 = h2 + gate * pp
    ms = jnp.mean(h3 * h3, axis=-1, keepdims=True)
    o_ref[...] = h3 * lax.rsqrt(ms + EPS) * gfin_ref[...]


def _ple(h1, yk, row0, route, pe, gple, wg, wp, gfin, tm):
    m, d = h1.shape
    blk0 = row0 // tm
    const = lambda shp: pl.BlockSpec(shp, lambda i: (0,) * len(shp))
    return pl.pallas_call(
        _ple_kernel,
        out_shape=jax.ShapeDtypeStruct((m, d), F32),
        grid=(m // tm,),
        in_specs=[pl.BlockSpec((tm, d), lambda i: (i, 0)),
                  pl.BlockSpec((1, tm, d), lambda i: (0, i + blk0, 0)),
                  pl.BlockSpec((1, tm, d), lambda i: (1, i + blk0, 0)),
                  pl.BlockSpec((tm, LANES), lambda i: (i, 0)),
                  pl.BlockSpec((tm, pe.shape[1]), lambda i: (i, 0)),
                  const((1, d)), const(wg.shape), const(wp.shape), const((1, d))],
        out_specs=pl.BlockSpec((tm, d), lambda i: (i, 0)),
        compiler_params=_params(("arbitrary",)),
        name="ple_final",
    )(h1, yk, yk, route, pe, gple.reshape(1, d), wg, wp, gfin.reshape(1, d))


def _project_group(x, g, w_in, pos, tm, q_scale, k_seq_tiles=0):
    xn = _rmsnorm(x, g, tm)
    da_q = _rope_tables(pos, DA_ROT, DA_HD, ROPE_THETA, q_scale)[:, None]
    da_k = _rope_tables(pos, DA_ROT, DA_HD, ROPE_THETA, 1.0)[:, None]
    rq_t = _rope_tables(pos, RET_KD, RET_KD, RET_THETA, 1.0)
    rk_t = _rope_tables(pos, RET_KD, RET_KD, RET_THETA, RET_KD ** -0.5)
    ret_t = np.stack([rq_t, rk_t], axis=1)
    q = _proj(xn, w_in, COL_Q, 1024, tm, BF16, rope=(da_q, DA_ROT // 2))
    k = _proj(xn, w_in, COL_K, 1024, tm, F32, rope=(da_k, DA_ROT // 2), seq_tiles=k_seq_tiles)
    v = _proj(xn, w_in, COL_V, 1024, tm, F32)
    rqk = _proj(xn, w_in, COL_RQK, 1024, tm, F32, rope=(ret_t, RET_KD // 2))
    rv = _proj(xn, w_in, COL_RV, 1024, tm, F32)
    rg = _proj(xn, w_in, COL_GATES, 1024, tm, F32)
    gagr = _proj(xn, w_in, COL_GATES + 1024, 4096, tm, F32)
    return q, k, v, rqk, rv, rg, gagr


def kernel(x_prompt, x_sample, cache_k, cache_v, state_ret, page_table, p_prompt, p_sample, norm_mix_g, w_in, lam_q1, lam_k1, lam_q2, lam_k2, da_norm_g, ret_norm_g, w_br_attn, w_br_ret, w_out, norm_ffn_g, w_router_group, b_router_group, w_router_expert, b_router_expert, w_exp_gate, w_exp_up, w_exp_down, norm_ple_g, w_ple_gate, w_ple_proj, final_norm_g):
    bp, sp, d = x_prompt.shape
    bs, ts, _ = x_sample.shape
    depth = w_in.shape[0]
    assert depth == 1, "the final norm is fused into the last stage of a single layer"
    n_pool, page = cache_k.shape[1], cache_k.shape[2]
    mp, msm = bp * sp, bs * ts
    log_g = np.log1p(-np.exp2(-5.0 - np.arange(RET_HEADS))).astype(np.float32)
    pos_p = np.arange(sp)
    pos_s = np.tile(PAST_LEN + np.arange(ts), bs)
    cache_k2 = cache_k.transpose(0, 1, 3, 4, 5, 2).reshape(depth * n_pool, DA_HEADS * 2 * DA_HD, page)
    cache_v2 = cache_v.reshape(depth * n_pool, page * DA_HEADS, DA_VD)
    tm_p = next(t for t in (1024, 512, 256, 128) if sp % t == 0)
    tf_p = 256

    hp = x_prompt.reshape(mp, d)
    hs = x_sample.reshape(msm, d)
    kp_l, vp_l, rp_l, ks_l, vs_l, rs_l = [], [], [], [], [], []
    for li in range(depth):
        lam_init = 0.8 - 0.6 * math.exp(-0.3 * li)
        lam = (jnp.exp(jnp.sum(lam_q1[li] * lam_k1[li]))
               - jnp.exp(jnp.sum(lam_q2[li] * lam_k2[li])) + lam_init).reshape(1).astype(F32)
        out_scale = 1.0 - lam_init
        wa = w_br_attn[li].astype(BF16)
        wr = w_br_ret[li].astype(BF16)
        wo = w_out[li].astype(BF16)
        wpg = w_ple_gate[li].astype(BF16)
        wpp = w_ple_proj[li].astype(BF16)
        wrt = jnp.zeros((d, LANES), F32)
        wrt = wrt.at[:, :N_EXPERTS].set(w_router_expert[li])
        wrt = wrt.at[:, N_EXPERTS:N_EXPERTS + N_GROUPS].set(w_router_group[li]).astype(BF16)
        brt = jnp.zeros((1, LANES), F32)
        brt = brt.at[0, :N_EXPERTS].set(b_router_expert[li])
        brt = brt.at[0, N_EXPERTS:N_EXPERTS + N_GROUPS].set(b_router_group[li])
        da_g = da_norm_g[li].reshape(1, -1)

        q, k, v, rqk, rv, rg, gagr = _project_group(hp, norm_mix_g[li], w_in[li], pos_p, tm_p,
                                                    DA_HD ** -0.5 * math.log2(math.e),
                                                    k_seq_tiles=sp // tm_p)
        a_p = _attn_prompt(lam, q, k, v, da_g, bp, sp, out_scale, tq=256, hp=4)
        chunk = 256 if sp % 256 == 0 else 128
        r_p, st_p = _retention(rqk.reshape(bp, sp, -1), rv.reshape(bp, sp, -1),
                               rg.reshape(bp, sp, -1),
                               jnp.zeros((bp, RET_HEADS, RET_KD, RET_VD), F32),
                               ret_norm_g[li], log_g, chunk, chunk)
        kp_l.append(k.reshape(bp, DA_HEADS, 2, DA_HD, sp).transpose(0, 4, 1, 2, 3))
        vp_l.append(v.reshape(bp, sp, DA_HEADS, DA_VD))
        rp_l.append(st_p)
        h1_p, hn_p, route_p, cnt_p = _finish(
            a_p, r_p.reshape(mp, -1), gagr, hp, wa, wr, wo, norm_ffn_g[li], wrt, brt,
            jnp.zeros((1, LANES), F32), tf_p)

        q, k, v, rqk, rv, rg, gagr = _project_group(hs, norm_mix_g[li], w_in[li], pos_s, msm,
                                                    DA_HD ** -0.5)
        tpad = 8
        pad_t = lambda z: jnp.pad(z.reshape(bs, ts, -1), ((0, 0), (0, tpad - ts), (0, 0)))
        q4 = q.astype(F32).reshape(bs, ts, DA_HEADS, 2, DA_HD)
        eye = jnp.eye(2, dtype=F32)
        qm = jnp.einsum('bthmd,mn->bhmtnd', q4, eye).reshape(bs, DA_HEADS * 2 * ts, 2 * DA_HD)
        a_s = _attn_paged(page_table, lam, qm, pad_t(k), pad_t(v), da_g, cache_k2, cache_v2,
                          li * n_pool, ts, out_scale, npg=8)
        r_s, st_s = _retention(pad_t(rqk), pad_t(rv), pad_t(rg), state_ret[li],
                               ret_norm_g[li], log_g, tpad, ts)
        ks_l.append(k.reshape(bs, ts, DA_HEADS, 2, DA_HD))
        vs_l.append(v.reshape(bs, ts, DA_HEADS, DA_VD))
        rs_l.append(st_s)
        h1_s, hn_s, route_s, cnt = _finish(
            a_s.reshape(msm, -1), r_s[:, :ts].reshape(msm, -1), gagr, hs, wa, wr, wo,
            norm_ffn_g[li], wrt, brt, cnt_p, msm)

        hn_all = jnp.concatenate([hn_p, hn_s], axis=0)
        route_all = jnp.concatenate([route_p, route_s], axis=0)
        yk = _moe(hn_all, route_all, cnt[0, :N_EXPERTS], w_exp_gate[li], w_exp_up[li],
                  w_exp_down[li])

        gfin = final_norm_g
        hp = _ple(h1_p, yk, 0, route_p, p_prompt[li].reshape(mp, -1), norm_ple_g[li], wpg, wpp,
                  gfin, tf_p)
        hs = _ple(h1_s, yk, mp, route_s, p_sample[li].reshape(msm, -1), norm_ple_g[li], wpg, wpp,
                  gfin, msm)
    y_prompt = hp.reshape(bp, sp, d)
    y_sample = hs.reshape(bs, ts, d)
    return (y_prompt, y_sample, jnp.stack(kp_l), jnp.stack(vp_l), jnp.stack(rp_l),
            jnp.stack(ks_l), jnp.stack(vs_l), jnp.stack(rs_l))
```

```python
import functools
import math

import jax
import jax.numpy as jnp
import numpy as np
from jax import lax
from jax.experimental import pallas as pl
from jax.experimental.pallas import tpu as pltpu

F32 = jnp.float32
BF16 = jnp.bfloat16
I32 = jnp.int32

PAST_LEN = 16384
DA_HEADS = 8
DA_VD = 128
DA_HD = 64
DA_ROT = 16
ROPE_THETA = 500000.0
RET_HEADS = 8
RET_VD = 128
RET_KD = 64
RET_THETA = 10000.0
N_GROUPS = 4
EXPERTS_PER_GROUP = 8
N_EXPERTS = N_GROUPS * EXPERTS_PER_GROUP
EPS = 1e-6
NEG = -1e30
LANES = 128

VMEM_LIMIT = 56 * 1024 * 1024

COL_Q, COL_K, COL_V, COL_RQK, COL_RV, COL_GATES = 0, 1024, 2048, 3072, 4096, 5120

MOE_ROWS = 768
MOE_SUB = 128
MOE_FSPLIT = 2


def _params(sem, vmem=VMEM_LIMIT):
    return pltpu.CompilerParams(dimension_semantics=sem, vmem_limit_bytes=vmem)


def _rmsnorm_kernel(x_ref, g_ref, o_ref):
    x = x_ref[...]
    ms = jnp.mean(x * x, axis=-1, keepdims=True)
    o_ref[...] = (x * lax.rsqrt(ms + EPS) * g_ref[...]).astype(o_ref.dtype)


def _rmsnorm(x, g, tm):
    m, d = x.shape
    return pl.pallas_call(
        _rmsnorm_kernel,
        out_shape=jax.ShapeDtypeStruct((m, d), BF16),
        grid=(m // tm,),
        in_specs=[pl.BlockSpec((tm, d), lambda i: (i, 0)),
                  pl.BlockSpec((1, d), lambda i: (0, 0))],
        out_specs=pl.BlockSpec((tm, d), lambda i: (i, 0)),
        compiler_params=_params(("arbitrary",)),
        name="rmsnorm",
    )(x, g.reshape(1, d))


def _proj_kernel(*refs, shift, tn, seq_tiles):
    if shift:
        xn_ref, w_ref, c_ref, s1_ref, s2_ref, o_ref, wbf_ref = refs
    else:
        xn_ref, w_ref, o_ref, wbf_ref = refs

    @pl.when(pl.program_id(1) == 0)
    def _():
        wbf_ref[...] = w_ref[...].astype(BF16)

    acc = jnp.dot(xn_ref[...], wbf_ref[...], preferred_element_type=F32)
    if shift:
        groups = c_ref.shape[0]
        rep = tn // LANES // groups
        wide = lambda ref: jnp.concatenate(
            [jnp.tile(ref[g], (1, rep)) for g in range(groups)], axis=1)
        acc = (acc * wide(c_ref) + pltpu.roll(acc, tn - shift, 1) * wide(s1_ref)
               + pltpu.roll(acc, shift, 1) * wide(s2_ref))
    if seq_tiles:
        o_ref[0] = acc.T.astype(o_ref.dtype)
    else:
        o_ref[...] = acc.astype(o_ref.dtype)


def _proj(xn, w_in, col0, ncols, tm, out_dtype, rope=None, seq_tiles=0):
    m, d = xn.shape
    tn = 1024
    nj = ncols // tn
    j0 = col0 // tn
    in_specs = [pl.BlockSpec((tm, d), lambda j, i: (i, 0)),
                pl.BlockSpec((d, tn), lambda j, i: (0, j + j0))]
    args = [xn, w_in]
    shift = 0
    if rope is not None:
        assert nj == 1
        tabs, shift = rope
        groups, npos = tabs.shape[1], tabs.shape[2]
        npb = npos // tm
        for t in range(3):
            in_specs.append(pl.BlockSpec((groups, tm, LANES), lambda j, i: (0, i % npb, 0)))
            args.append(tabs[t])
    if seq_tiles:
        out_shape = jax.ShapeDtypeStruct((m // (seq_tiles * tm), ncols, seq_tiles * tm), out_dtype)
        out_spec = pl.BlockSpec((1, tn, tm), lambda j, i: (i // seq_tiles, j, i % seq_tiles))
    else:
        out_shape = jax.ShapeDtypeStruct((m, ncols), out_dtype)
        out_spec = pl.BlockSpec((tm, tn), lambda j, i: (i, j))
    return pl.pallas_call(
        functools.partial(_proj_kernel, shift=shift, tn=tn, seq_tiles=seq_tiles),
        out_shape=out_shape,
        grid=(nj, m // tm),
        in_specs=in_specs,
        out_specs=out_spec,
        scratch_shapes=[pltpu.VMEM((d, tn), BF16)],
        compiler_params=_params(("arbitrary", "arbitrary")),
        name="in_proj",
    )(*args)


def _rope_tables(pos, rot_dim, period, theta, scale):
    half = rot_dim // 2
    inv = (1.0 / np.power(np.float32(theta), np.arange(half, dtype=np.float32)
                          * np.float32(2.0 / rot_dim))).astype(np.float32)
    ang = (pos.astype(np.float32)[:, None] * inv[None, :]).astype(np.float64)
    cos, sin = np.cos(ang), np.sin(ang)
    npos = pos.shape[0]
    pad = period - rot_dim
    c = np.concatenate([cos, cos, np.ones((npos, pad))], axis=1)
    s1 = np.concatenate([-sin, np.zeros((npos, half + pad))], axis=1)
    s2 = np.concatenate([np.zeros((npos, half)), sin, np.zeros((npos, pad))], axis=1)
    tabs = np.stack([c, s1, s2]).astype(np.float32) * np.float32(scale)
    return np.tile(tabs, (1, 1, LANES // period)).astype(np.float32)


def _attn_prompt_kernel(lam_ref, q_ref, k_ref, v_ref, g_ref, o_ref,
                        kb_ref, vt_ref, m_ref, acc_ref, *, tq, hp, out_scale):
    qi = pl.program_id(2)
    nblk = kb_ref.shape[1]
    ext = vt_ref.shape[2]

    @pl.when(qi == 0)
    def _():
        for c in range(nblk):
            for h in range(hp):
                kb_ref[h, c] = k_ref[0, h * LANES:(h + 1) * LANES,
                                     c * tq:(c + 1) * tq].T.astype(BF16)

        def stage(c, carry):
            off = pl.multiple_of(c * tq, tq)
            for h in range(hp):
                cols = slice(h * LANES, (h + 1) * LANES)
                vt_ref[h, c, :LANES, :] = v_ref[pl.ds(off, tq), cols].T.astype(BF16)
                vt_ref[h, c, LANES:, :] = jnp.ones((ext - LANES, tq), BF16)
            return carry
        lax.fori_loop(0, nblk, stage, 0)

    row = lax.broadcasted_iota(I32, (LANES, tq), 0)
    qqt = []
    for h in range(hp):
        qt = q_ref[:, h * LANES:(h + 1) * LANES].astype(F32).T
        qqt.append(jnp.concatenate([jnp.where(row < DA_HD, qt, 0.0),
                                    jnp.where(row >= DA_HD, qt, 0.0)], axis=1).astype(BF16))
    m_ref[...] = jnp.full(m_ref.shape, NEG, F32)
    acc_ref[...] = jnp.zeros(acc_ref.shape, F32)

    def step(j, masked):
        sts = [jnp.dot(kb_ref[h, j], qqt[h], preferred_element_type=F32)
               for h in range(hp)]
        for h in range(hp):
            st = sts[h]
            if masked:
                key = lax.broadcasted_iota(I32, (tq, 2 * tq), 0)
                qry = lax.broadcasted_iota(I32, (tq, 2 * tq), 1) & (tq - 1)
                st = jnp.where(key <= qry, st, NEG)
            m_old = m_ref[h]
            m_new = jnp.maximum(m_old, jnp.max(st, axis=0, keepdims=True))
            alpha = jnp.exp2(m_old - m_new)
            p = jnp.exp2(st - m_new)
            acc_ref[h] = alpha * acc_ref[h] + jnp.dot(vt_ref[h, j], p.astype(BF16),
                                                      preferred_element_type=F32)
            m_ref[h] = m_new

    def body(j, carry):
        step(j, False)
        return carry

    lax.fori_loop(0, qi, body, 0)
    step(qi, True)

    for h in range(hp):
        cols = slice(h * LANES, (h + 1) * LANES)
        o = acc_ref[h, :LANES, :] / acc_ref[h, LANES:LANES + 1, :]
        out = (o[:, :tq] - lam_ref[0] * o[:, tq:]).T
        ms = jnp.mean(out * out, axis=-1, keepdims=True)
        out = out * lax.rsqrt(ms + EPS) * g_ref[:, cols] * out_scale
        o_ref[:, cols] = out.astype(o_ref.dtype)


def _attn_prompt(lam, q, k, v, g, b, s, out_scale, tq, hp):
    nq = s // tq
    w = hp * LANES
    return pl.pallas_call(
        functools.partial(_attn_prompt_kernel, tq=tq, hp=hp, out_scale=out_scale),
        out_shape=jax.ShapeDtypeStruct((b * s, DA_HEADS * DA_VD), BF16),
        grid=(b, DA_HEADS // hp, nq),
        in_specs=[pl.BlockSpec(memory_space=pltpu.SMEM),
                  pl.BlockSpec((tq, w), lambda bi, h, qi: (bi * nq + qi, h)),
                  pl.BlockSpec((1, w, s), lambda bi, h, qi: (bi, h, 0)),
                  pl.BlockSpec((s, w), lambda bi, h, qi: (bi, h)),
                  pl.BlockSpec((1, w), lambda bi, h, qi: (0, h))],
        out_specs=pl.BlockSpec((tq, w), lambda bi, h, qi: (bi * nq + qi, h)),
        scratch_shapes=[pltpu.VMEM((hp, nq, tq, LANES), BF16),
                        pltpu.VMEM((hp, nq, LANES + 16, tq), BF16),
                        pltpu.VMEM((hp, 1, 2 * tq), F32),
                        pltpu.VMEM((hp, LANES + 16, 2 * tq), F32)],
        compiler_params=_params(("arbitrary", "arbitrary", "arbitrary")),
        name="attn_prompt",
    )(lam, q, k, v, g)


def _attn_paged_kernel(pt_ref, lam_ref, q_ref, kn_ref, vn_ref, g_ref, *rest,
                       npg, t, out_scale):
    k_refs = rest[:npg]
    v_refs = rest[npg:2 * npg]
    o_ref, m_ref, l_ref, acc_ref = rest[2 * npg:]
    c = pl.program_id(1)
    nc = pl.num_programs(1)
    rows = 2 * t
    page = k_refs[0].shape[2]

    @pl.when(c == 0)
    def _():
        m_ref[...] = jnp.full(m_ref.shape, NEG, F32)
        l_ref[...] = jnp.zeros(l_ref.shape, F32)
        acc_ref[...] = jnp.zeros(acc_ref.shape, F32)

    def update(scores, values):
        s_all = jnp.concatenate(scores, axis=0)
        m_old = m_ref[...]
        m_new = jnp.maximum(m_old, jnp.max(s_all, axis=-1, keepdims=True))
        alpha = jnp.exp(m_old - m_new)
        p = jnp.exp(s_all - m_new)
        l_ref[...] = alpha * l_ref[...] + jnp.sum(p, axis=-1, keepdims=True)
        m_ref[...] = m_new
        pb = p
        for h in range(DA_HEADS):
            r0 = h * rows
            pv = None
            for lo, hi, load_v in values[h]:
                d = jnp.dot(pb[r0:r0 + rows, lo:hi], load_v(), preferred_element_type=F32)
                pv = d if pv is None else pv + d
            acc_ref[r0:r0 + rows, :] = alpha[r0:r0 + rows] * acc_ref[r0:r0 + rows, :] + pv

    q = q_ref[0]
    scores, values = [], []
    for h in range(DA_HEADS):
        qh = q[h * rows:(h + 1) * rows]
        sh, vh = [], []
        for i in range(npg):
            kb = k_refs[i][0, h * LANES:(h + 1) * LANES, :]
            sh.append(jnp.dot(qh, kb, preferred_element_type=F32))
            vh.append((i * page, (i + 1) * page,
                       lambda i=i, h=h: v_refs[i][0, pl.ds(h, page, stride=DA_HEADS), :]))
        scores.append(jnp.concatenate(sh, axis=1))
        values.append(vh)
    update(scores, values)

    @pl.when(c == nc - 1)
    def _():
        tp = kn_ref.shape[1]
        scores, values = [], []
        row = lax.broadcasted_iota(I32, (rows, tp), 0) & (t - 1)
        col = lax.broadcasted_iota(I32, (rows, tp), 1)
        for h in range(DA_HEADS):
            qh = q[h * rows:(h + 1) * rows]
            kb = kn_ref[0, :, h * LANES:(h + 1) * LANES]
            s = lax.dot_general(qh, kb, (((1,), (1,)), ((), ())), preferred_element_type=F32)
            scores.append(jnp.where(col <= row, s, NEG))
            values.append([(0, tp, lambda h=h: vn_ref[0, :, h * LANES:(h + 1) * LANES])])
        update(scores, values)
        o = acc_ref[...] / l_ref[...]
        lam = lam_ref[0]
        for h in range(DA_HEADS):
            r0 = h * rows
            out = o[r0:r0 + t] - lam * o[r0 + t:r0 + rows]
            ms = jnp.mean(out * out, axis=-1, keepdims=True)
            gh = g_ref[:, h * LANES:(h + 1) * LANES]
            o_ref[0, :, h * LANES:(h + 1) * LANES] = (
                out * lax.rsqrt(ms + EPS) * gh * out_scale).astype(o_ref.dtype)


def _attn_paged(page_table, lam, qm, k_new, v_new, g, cache_k, cache_v, page0, t, out_scale, npg):
    db, n_pages = page_table.shape
    prow, pcol = cache_k.shape[1], cache_k.shape[2]
    assert cache_v.shape[1:] == (prow, pcol)
    w = k_new.shape[2]
    tp = k_new.shape[1]
    nc = n_pages // npg
    pt_flat = page_table.reshape(-1) + page0

    def page_spec(i):
        return pl.BlockSpec((1, prow, pcol),
                            lambda b, c, pt: (pt[b * n_pages + c * npg + i], 0, 0))

    in_specs = [pl.BlockSpec(memory_space=pltpu.SMEM),
                pl.BlockSpec((1, qm.shape[1], LANES), lambda b, c, pt: (b, 0, 0)),
                pl.BlockSpec((1, tp, w), lambda b, c, pt: (b, 0, 0)),
                pl.BlockSpec((1, tp, w), lambda b, c, pt: (b, 0, 0)),
                pl.BlockSpec((1, w), lambda b, c, pt: (0, 0))]
    in_specs += [page_spec(i) for i in range(npg)] * 2
    rows = DA_HEADS * 2 * t
    return pl.pallas_call(
        functools.partial(_attn_paged_kernel, npg=npg, t=t, out_scale=out_scale),
        out_shape=jax.ShapeDtypeStruct((db, t, w), BF16),
        grid_spec=pltpu.PrefetchScalarGridSpec(
            num_scalar_prefetch=1,
            grid=(db, nc),
            in_specs=in_specs,
            out_specs=pl.BlockSpec((1, t, w), lambda b, c, pt: (b, 0, 0)),
            scratch_shapes=[pltpu.VMEM((rows, 1), F32),
                            pltpu.VMEM((rows, 1), F32),
                            pltpu.VMEM((rows, LANES), F32)]),
        compiler_params=_params(("arbitrary", "arbitrary")),
        name="attn_paged",
    )(pt_flat, lam, qm, k_new, v_new, g, *([cache_k] * npg), *([cache_v] * npg))


def _ret_kernel(rq_ref, rk_ref, rv_ref, rg_ref, s0_ref, dmat_ref, dq_ref, dk_ref, gc_ref, g_ref,
                o_ref, sout_ref, st_ref):
    c = pl.program_id(1)

    @pl.when(c == 0)
    def _():
        st_ref[...] = s0_ref[0]

    for h in range(RET_HEADS):
        kq = slice(h * RET_KD, (h + 1) * RET_KD)
        vs = slice(h * RET_VD, (h + 1) * RET_VD)
        qb = rq_ref[0, :, kq].astype(BF16)
        k = rk_ref[0, :, kq]
        vb = rv_ref[0, :, vs].astype(BF16)
        st = st_ref[h]
        att = lax.dot_general(qb, k.astype(BF16), (((1,), (1,)), ((), ())),
                              preferred_element_type=F32) * dmat_ref[h]
        inner = jnp.dot(att.astype(BF16), vb, preferred_element_type=F32)
        cross = jnp.dot(qb, st.astype(BF16), preferred_element_type=F32) * dq_ref[h]
        r = inner + cross
        kd = (k * dk_ref[h]).astype(BF16)
        st_ref[h] = gc_ref[h] * st + lax.dot_general(
            kd, vb, (((0,), (0,)), ((), ())), preferred_element_type=F32)
        mu = jnp.mean(r, axis=-1, keepdims=True)
        xc = r - mu
        var = jnp.mean(xc * xc, axis=-1, keepdims=True)
        y = xc * lax.rsqrt(var + EPS) * g_ref[:, vs]
        rg = rg_ref[0, :, vs]
        o_ref[0, :, vs] = (rg * jax.nn.sigmoid(rg) * y).astype(o_ref.dtype)

    @pl.when(c == pl.num_programs(1) - 1)
    def _():
        sout_ref[0] = st_ref[...]


def _ret_tables(log_g, chunk, valid):
    f32 = np.float32
    idx = np.arange(chunk, dtype=f32)
    diff = idx[:, None] - idx[None, :]
    ex = lambda e: np.exp(e.astype(f32).astype(np.float64)).astype(f32)
    dmat = np.where(diff[None] >= 0, ex(np.maximum(diff, f32(0))[None] * log_g[:, None, None]),
                    f32(0))
    dq = ex((idx + f32(1))[None, :] * log_g[:, None])
    dk = np.where(idx[None, :] < valid,
                  ex((f32(valid - 1) - idx)[None, :] * log_g[:, None]), f32(0))
    gc = ex(f32(valid) * log_g)
    h = log_g.shape[0]
    return (dmat.astype(f32),
            np.ascontiguousarray(np.broadcast_to(dq[:, :, None], (h, chunk, RET_VD))).astype(f32),
            np.ascontiguousarray(np.broadcast_to(dk[:, :, None], (h, chunk, RET_KD))).astype(f32),
            np.ascontiguousarray(np.broadcast_to(gc[:, None, None], (h, 1, RET_VD))).astype(f32))


def _retention(rqk, rest, state0, g, log_g, chunk, valid):
    b, s, _ = rqk.shape
    nc = s // chunk
    dmat, dq, dk, gc = _ret_tables(log_g, chunk, valid)
    qw = RET_HEADS * RET_KD
    vw = RET_HEADS * RET_VD
    full = lambda shp: pl.BlockSpec(shp, lambda bi, c: (0,) * len(shp))
    return pl.pallas_call(
        _ret_kernel,
        out_shape=(jax.ShapeDtypeStruct((b, s, vw), BF16),
                   jax.ShapeDtypeStruct((b, RET_HEADS, RET_KD, RET_VD), F32)),
        grid=(b, nc),
        in_specs=[pl.BlockSpec((1, chunk, qw), lambda bi, c: (bi, c, 0)),
                  pl.BlockSpec((1, chunk, qw), lambda bi, c: (bi, c, 1)),
                  pl.BlockSpec((1, chunk, vw), lambda bi, c: (bi, c, 0)),
                  pl.BlockSpec((1, chunk, vw), lambda bi, c: (bi, c, 1)),
                  pl.BlockSpec((1, RET_HEADS, RET_KD, RET_VD), lambda bi, c: (bi, 0, 0, 0)),
                  full(dmat.shape), full(dq.shape), full(dk.shape), full(gc.shape),
                  full((1, vw))],
        out_specs=(pl.BlockSpec((1, chunk, vw), lambda bi, c: (bi, c, 0)),
                   pl.BlockSpec((1, RET_HEADS, RET_KD, RET_VD), lambda bi, c: (bi, 0, 0, 0))),
        scratch_shapes=[pltpu.VMEM((RET_HEADS, RET_KD, RET_VD), F32)],
        compiler_params=_params(("arbitrary", "arbitrary")),
        name="retention",
    )(rqk, rqk, rest, rest, state0, dmat, dq, dk, gc, g.reshape(1, vw))


def _finish_kernel(a_ref, r_ref, ga_ref, gr_ref, h_ref, wa_ref, wr_ref, wo_ref, gffn_ref,
                   wrt_ref, brt_ref, carry0_ref,
                   h1_ref, hn_ref, route_ref, cnt_ref, carry_ref, *, tm):
    i = pl.program_id(0)

    @pl.when(i == 0)
    def _():
        carry_ref[...] = carry0_ref[...]

    am = jnp.dot(a_ref[...], wa_ref[...], preferred_element_type=F32)
    rm = jnp.dot(r_ref[...], wr_ref[...], preferred_element_type=F32)
    mix = jax.nn.sigmoid(ga_ref[...]) * am + jax.nn.sigmoid(gr_ref[...]) * rm
    h1 = h_ref[...] + jnp.dot(mix.astype(BF16), wo_ref[...], preferred_element_type=F32)
    h1_ref[...] = h1
    ms = jnp.mean(h1 * h1, axis=-1, keepdims=True)
    hn = h1 * lax.rsqrt(ms + EPS) * gffn_ref[...]
    hn_ref[...] = hn
    logits = jnp.dot(hn.astype(BF16), wrt_ref[...], preferred_element_type=F32) + brt_ref[...]
    lane = lax.broadcasted_iota(I32, (tm, LANES), 1)
    is_g = (lane >= N_EXPERTS) & (lane < N_EXPERTS + N_GROUPS)
    glog = jnp.where(is_g, logits, -jnp.inf)
    gmax = jnp.max(glog, axis=-1, keepdims=True)
    gidx = jnp.min(jnp.where(glog == gmax, lane - N_EXPERTS, LANES), axis=-1, keepdims=True)
    gden = jnp.sum(jnp.where(is_g, jnp.exp(glog - gmax), 0.0), axis=-1, keepdims=True)
    gp = 1.0 / gden
    in_g = (lane < N_EXPERTS) & ((lane // EXPERTS_PER_GROUP) == gidx)
    e1 = jnp.where(in_g, logits, -jnp.inf)
    v1 = jnp.max(e1, axis=-1, keepdims=True)
    i1 = jnp.min(jnp.where(e1 == v1, lane, LANES), axis=-1, keepdims=True)
    e2 = jnp.where(lane == i1, -jnp.inf, e1)
    v2 = jnp.max(e2, axis=-1, keepdims=True)
    i2 = jnp.min(jnp.where(e2 == v2, lane, LANES), axis=-1, keepdims=True)
    tt = jnp.exp(v2 - v1)
    w1 = gp / (1.0 + tt)
    w2 = gp * tt / (1.0 + tt)
    oh = jnp.where((lane == i1) | (lane == i2), 1.0, 0.0)
    rr = lax.broadcasted_iota(I32, (tm, tm), 0)
    cc = lax.broadcasted_iota(I32, (tm, tm), 1)
    tri = jnp.where(cc < rr, 1.0, 0.0).astype(BF16)
    before = jnp.dot(tri, oh.astype(BF16), preferred_element_type=F32) + carry_ref[...]
    rank1 = jnp.sum(jnp.where(lane == i1, before, 0.0), axis=-1, keepdims=True)
    rank2 = jnp.sum(jnp.where(lane == i2, before, 0.0), axis=-1, keepdims=True)
    carry = carry_ref[...] + jnp.sum(oh, axis=0, keepdims=True)
    carry_ref[...] = carry
    cnt_ref[...] = carry
    cols = [i1.astype(F32), i2.astype(F32), w1, w2, rank1, rank2]
    route = jnp.zeros((tm, LANES), F32)
    for ci, val in enumerate(cols):
        route = jnp.where(lane == ci, val, route)
    route_ref[...] = route


def _finish(a, r, rest, h, wa, wr, wo, gffn, wrt, brt, carry0, tm):
    m, d = h.shape
    aw = a.shape[1]
    assert rest.shape[1] == 3 * d and 2 * aw == d
    const = lambda shp: pl.BlockSpec(shp, lambda i: (0,) * len(shp))
    return pl.pallas_call(
        functools.partial(_finish_kernel, tm=tm),
        out_shape=(jax.ShapeDtypeStruct((m, d), F32),
                   jax.ShapeDtypeStruct((m, d), F32),
                   jax.ShapeDtypeStruct((m, LANES), F32),
                   jax.ShapeDtypeStruct((1, LANES), F32)),
        grid=(m // tm,),
        in_specs=[pl.BlockSpec((tm, aw), lambda i: (i, 0)),
                  pl.BlockSpec((tm, aw), lambda i: (i, 0)),
                  pl.BlockSpec((tm, d), lambda i: (i, 1)),
                  pl.BlockSpec((tm, d), lambda i: (i, 2)),
                  pl.BlockSpec((tm, d), lambda i: (i, 0)),
                  const(wa.shape), const(wr.shape), const(wo.shape), const((1, d)),
                  const(wrt.shape), const((1, LANES)), const((1, LANES))],
        out_specs=(pl.BlockSpec((tm, d), lambda i: (i, 0)),
                   pl.BlockSpec((tm, d), lambda i: (i, 0)),
                   pl.BlockSpec((tm, LANES), lambda i: (i, 0)),
                   pl.BlockSpec((1, LANES), lambda i: (0, 0))),
        scratch_shapes=[pltpu.VMEM((1, LANES), F32)],
        compiler_params=_params(("arbitrary",)),
        name="finish",
    )(a, r, rest, rest, h, wa, wr, wo, gffn.reshape(1, d), wrt, brt, carry0)


def _moe_kernel(item_e, item_start, item_n, row_src,
                hn_hbm, wg_ref, wu_ref, wd_ref, yk_hbm,
                xg, yacc, gsem, ssem, *, sub):
    i = pl.program_id(0)
    f = pl.program_id(1)
    ni = pl.num_programs(0)
    nf = pl.num_programs(1)
    n = item_n[i]
    slot = lax.rem(i, 2)

    grp = 8

    def gather_copy(s, r, tok, rows=1):
        return pltpu.make_async_copy(hn_hbm.at[pl.ds(tok, rows)], xg.at[s, pl.ds(r, rows)], gsem)

    def scatter_copy(r, k, tok, rows=1):
        return pltpu.make_async_copy(yacc.at[pl.ds(r, rows)], yk_hbm.at[k, pl.ds(tok, rows)], ssem)

    def repeat(count, fn):
        def body(t, c):
            fn(t)
            return c
        lax.fori_loop(0, count, body, 0)

    def start_gather(item, s):
        first = item_start[item]

        def issue(t):
            for u in range(grp):
                r = t * grp + u
                gather_copy(s, r, row_src[first + r] >> 1).start()
        repeat((item_n[item] + grp - 1) // grp, issue)

    def wait_gather(count):
        repeat((count + grp - 1) // grp, lambda t: gather_copy(0, 0, 0, grp).wait())

    def wait_scatter(count):
        repeat(count // grp, lambda t: scatter_copy(0, 0, 0, grp).wait())
        repeat(lax.rem(count, grp), lambda t: scatter_copy(0, 0, 0).wait())

    @pl.when(f == 0)
    def _():
        @pl.when(i == 0)
        def _():
            xg[...] = jnp.zeros(xg.shape, xg.dtype)
            start_gather(0, 0)

        @pl.when(i > 0)
        def _():
            wait_scatter(item_n[i - 1])
        wait_gather(n)

    @pl.when((f == 1) & (i + 1 < ni))
    def _():
        start_gather(i + 1, 1 - slot)

    @pl.when(n > 0)
    def _():
        def run_rows(off, m):
            x = xg[slot, pl.ds(off, m), :]
            hg = jnp.dot(x, wg_ref[0], preferred_element_type=F32)
            hu = jnp.dot(x, wu_ref[0], preferred_element_type=F32)
            hm = hg * jax.nn.sigmoid(hg) * hu
            part = jnp.dot(hm, wd_ref[0], preferred_element_type=F32)

            @pl.when(f == 0)
            def _():
                yacc[pl.ds(off, m), :] = part

            @pl.when(f > 0)
            def _():
                yacc[pl.ds(off, m), :] += part

        nsub = (n + sub - 1) // sub
        nbig = nsub // 4

        def big(s, c):
            run_rows(pl.multiple_of(s * (4 * sub), 4 * sub), 4 * sub)
            return c
        lax.fori_loop(0, nbig, big, 0)
        off2 = pl.multiple_of(nbig * (4 * sub), 2 * sub)

        @pl.when((nsub & 2) != 0)
        def _():
            run_rows(off2, 2 * sub)
        off1 = pl.multiple_of(off2 + (nsub & 2) * sub, sub)

        @pl.when((nsub & 1) != 0)
        def _():
            run_rows(off1, sub)

    @pl.when(f == nf - 1)
    def _():
        first = item_start[i]

        def issue_row(r):
            src = row_src[first + r]
            scatter_copy(r, src & 1, src >> 1).start()

        def issue_group(t):
            for u in range(grp):
                issue_row(t * grp + u)
        repeat(n // grp, issue_group)
        tail = (n // grp) * grp
        repeat(n - tail, lambda t: issue_row(tail + t))

        @pl.when(i == ni - 1)
        def _():
            wait_scatter(n)


def _moe_plan(route, counts, rows_per_item, max_items):
    ntok = route.shape[0]
    eid = route[:, 0:2].astype(I32)
    rank = route[:, 4:6].astype(I32)
    counts = counts.astype(I32)
    ends = jnp.cumsum(counts)
    starts = ends - counts
    dest = (starts[eid] + rank).reshape(-1)
    src = jnp.arange(2 * ntok, dtype=I32)
    row_src = jnp.zeros((2 * ntok + 8,), I32).at[dest].set(src)
    nit = (counts + rows_per_item - 1) // rows_per_item
    it_end = jnp.cumsum(nit)
    it_first = it_end - nit
    total = it_end[-1]
    t = jnp.arange(max_items, dtype=I32)
    e = jnp.minimum(jnp.searchsorted(it_end, t, side='right'), N_EXPERTS - 1).astype(I32)
    j = t - it_first[e]
    n = jnp.clip(counts[e] - j * rows_per_item, 0, rows_per_item)
    n = jnp.where(t < total, n, 0).astype(I32)
    start = (starts[e] + j * rows_per_item).astype(I32)
    start = jnp.where(n > 0, start, 0)
    last_e = e[jnp.maximum(total - 1, 0)]
    e = jnp.where(t < total, e, last_e)
    return e, start, n, row_src


def _moe(hn, route, counts, wg, wu, wd):
    ntok, d = hn.shape
    ne, _, ff = wg.shape
    rows, sub, nf = MOE_ROWS, MOE_SUB, MOE_FSPLIT
    assert nf > 1 and rows % sub == 0
    tf = ff // nf
    max_items = ne + (2 * ntok) // rows
    item_e, item_start, item_n, row_src = _moe_plan(route, counts, rows, max_items)

    def f_eff(i, f, item_n):
        return jnp.where(item_n[i] > 0, f, nf - 1)

    return pl.pallas_call(
        functools.partial(_moe_kernel, sub=sub),
        out_shape=jax.ShapeDtypeStruct((2, ntok, d), F32),
        grid_spec=pltpu.PrefetchScalarGridSpec(
            num_scalar_prefetch=4,
            grid=(max_items, nf),
            in_specs=[pl.BlockSpec(memory_space=pl.ANY),
                      pl.BlockSpec((1, d, tf), lambda i, f, ie, ist, inn, rs: (ie[i], 0, f_eff(i, f, inn))),
                      pl.BlockSpec((1, d, tf), lambda i, f, ie, ist, inn, rs: (ie[i], 0, f_eff(i, f, inn))),
                      pl.BlockSpec((1, tf, d), lambda i, f, ie, ist, inn, rs: (ie[i], f_eff(i, f, inn), 0))],
            out_specs=pl.BlockSpec(memory_space=pl.ANY),
            scratch_shapes=[pltpu.VMEM((2, rows, d), F32),
                            pltpu.VMEM((rows, d), F32),
                            pltpu.SemaphoreType.DMA,
                            pltpu.SemaphoreType.DMA]),
        compiler_params=_params(("arbitrary", "arbitrary")),
        name="moe_experts",
    )(item_e, item_start, item_n, row_src, hn, wg, wu, wd)


def _ple_kernel(h1_ref, y0_ref, y1_ref, route_ref, pe_ref, gple_ref, wg_ref, wp_ref, gfin_ref,
                o_ref):
    route = route_ref[...]
    y = route[:, 2:3] * y0_ref[0] + route[:, 3:4] * y1_ref[0]
    h2 = h1_ref[...] + y
    ms = jnp.mean(h2 * h2, axis=-1, keepdims=True)
    hn = (h2 * lax.rsqrt(ms + EPS) * gple_ref[...]).astype(BF16)
    gate = jax.nn.sigmoid(jnp.dot(hn, wg_ref[...], preferred_element_type=F32))
    pp = jnp.dot(pe_ref[...].astype(BF16), wp_ref[...], preferred_element_type=F32)
    h3 = h2 + gate * pp
    ms = jnp.mean(h3 * h3, axis=-1, keepdims=True)
    o_ref[...] = h3 * lax.rsqrt(ms + EPS) * gfin_ref[...]


def _ple(h1, yk, row0, route, pe, gple, wg, wp, gfin, tm):
    m, d = h1.shape
    blk0 = row0 // tm
    const = lambda shp: pl.BlockSpec(shp, lambda i: (0,) * len(shp))
    return pl.pallas_call(
        _ple_kernel,
        out_shape=jax.ShapeDtypeStruct((m, d), F32),
        grid=(m // tm,),
        in_specs=[pl.BlockSpec((tm, d), lambda i: (i, 0)),
                  pl.BlockSpec((1, tm, d), lambda i: (0, i + blk0, 0)),
                  pl.BlockSpec((1, tm, d), lambda i: (1, i + blk0, 0)),
                  pl.BlockSpec((tm, LANES), lambda i: (i, 0)),
                  pl.BlockSpec((tm, pe.shape[1]), lambda i: (i, 0)),
                  const((1, d)), const(wg.shape), const(wp.shape), const((1, d))],
        out_specs=pl.BlockSpec((tm, d), lambda i: (i, 0)),
        compiler_params=_params(("arbitrary",)),
        name="ple_final",
    )(h1, yk, yk, route, pe, gple.reshape(1, d), wg, wp, gfin.reshape(1, d))


def _project_group(x, g, w_in, pos, tm, q_scale, k_seq_tiles=0):
    xn = _rmsnorm(x, g, tm)
    da_q = _rope_tables(pos, DA_ROT, DA_HD, ROPE_THETA, q_scale)[:, None]
    da_k = _rope_tables(pos, DA_ROT, DA_HD, ROPE_THETA, 1.0)[:, None]
    rq_t = _rope_tables(pos, RET_KD, RET_KD, RET_THETA, 1.0)
    rk_t = _rope_tables(pos, RET_KD, RET_KD, RET_THETA, RET_KD ** -0.5)
    ret_t = np.stack([rq_t, rk_t], axis=1)
    q = _proj(xn, w_in, COL_Q, 1024, tm, BF16, rope=(da_q, DA_ROT // 2))
    k = _proj(xn, w_in, COL_K, 1024, tm, F32, rope=(da_k, DA_ROT // 2), seq_tiles=k_seq_tiles)
    v = _proj(xn, w_in, COL_V, 1024, tm, F32)
    rqk = _proj(xn, w_in, COL_RQK, 1024, tm, F32, rope=(ret_t, RET_KD // 2))
    rest = _proj(xn, w_in, COL_RV, 6144, tm, F32)
    return q, k, v, rqk, rest


def kernel(x_prompt, x_sample, cache_k, cache_v, state_ret, page_table, p_prompt, p_sample, norm_mix_g, w_in, lam_q1, lam_k1, lam_q2, lam_k2, da_norm_g, ret_norm_g, w_br_attn, w_br_ret, w_out, norm_ffn_g, w_router_group, b_router_group, w_router_expert, b_router_expert, w_exp_gate, w_exp_up, w_exp_down, norm_ple_g, w_ple_gate, w_ple_proj, final_norm_g):
    bp, sp, d = x_prompt.shape
    bs, ts, _ = x_sample.shape
    depth = w_in.shape[0]
    assert depth == 1, "the final norm is fused into the last stage of a single layer"
    n_pool, page = cache_k.shape[1], cache_k.shape[2]
    mp, msm = bp * sp, bs * ts
    log_g = np.log1p(-np.exp2(-5.0 - np.arange(RET_HEADS))).astype(np.float32)
    pos_p = np.arange(sp)
    pos_s = np.tile(PAST_LEN + np.arange(ts), bs)
    cache_k2 = cache_k.transpose(0, 1, 3, 4, 5, 2).reshape(depth * n_pool, DA_HEADS * 2 * DA_HD, page)
    cache_v2 = cache_v.reshape(depth * n_pool, page * DA_HEADS, DA_VD)
    tm_p = next(t for t in (1024, 512, 256, 128) if sp % t == 0)
    tf_p = 256
    tl_p = 512 if mp % 512 == 0 else 256

    hp = x_prompt.reshape(mp, d)
    hs = x_sample.reshape(msm, d)
    kp_l, vp_l, rp_l, ks_l, vs_l, rs_l = [], [], [], [], [], []
    for li in range(depth):
        lam_init = 0.8 - 0.6 * math.exp(-0.3 * li)
        lam = (jnp.exp(jnp.sum(lam_q1[li] * lam_k1[li]))
               - jnp.exp(jnp.sum(lam_q2[li] * lam_k2[li])) + lam_init).reshape(1).astype(F32)
        out_scale = 1.0 - lam_init
        wa = w_br_attn[li].astype(BF16)
        wr = w_br_ret[li].astype(BF16)
        wo = w_out[li].astype(BF16)
        wpg = w_ple_gate[li].astype(BF16)
        wpp = w_ple_proj[li].astype(BF16)
        wrt = jnp.zeros((d, LANES), F32)
        wrt = wrt.at[:, :N_EXPERTS].set(w_router_expert[li])
        wrt = wrt.at[:, N_EXPERTS:N_EXPERTS + N_GROUPS].set(w_router_group[li]).astype(BF16)
        brt = jnp.zeros((1, LANES), F32)
        brt = brt.at[0, :N_EXPERTS].set(b_router_expert[li])
        brt = brt.at[0, N_EXPERTS:N_EXPERTS + N_GROUPS].set(b_router_group[li])
        da_g = da_norm_g[li].reshape(1, -1)

        q, k, v, rqk, rest = _project_group(hp, norm_mix_g[li], w_in[li], pos_p, tm_p,
                                            DA_HD ** -0.5 * math.log2(math.e),
                                            k_seq_tiles=sp // tm_p)
        a_p = _attn_prompt(lam, q, k, v, da_g, bp, sp, out_scale, tq=256, hp=4)
        chunk = 256 if sp % 256 == 0 else 128
        r_p, st_p = _retention(rqk.reshape(bp, sp, -1), rest.reshape(bp, sp, -1),
                               jnp.zeros((bp, RET_HEADS, RET_KD, RET_VD), F32),
                               ret_norm_g[li], log_g, chunk, chunk)
        kp_l.append(k.reshape(bp, DA_HEADS, 2, DA_HD, sp).transpose(0, 4, 1, 2, 3))
        vp_l.append(v.reshape(bp, sp, DA_HEADS, DA_VD))
        rp_l.append(st_p)
        h1_p, hn_p, route_p, cnt_p = _finish(
            a_p, r_p.reshape(mp, -1), rest, hp, wa, wr, wo, norm_ffn_g[li], wrt, brt,
            jnp.zeros((1, LANES), F32), tf_p)

        q, k, v, rqk, rest = _project_group(hs, norm_mix_g[li], w_in[li], pos_s, msm,
                                            DA_HD ** -0.5)
        tpad = 8
        pad_t = lambda z: jnp.pad(z.reshape(bs, ts, -1), ((0, 0), (0, tpad - ts), (0, 0)))
        q4 = q.astype(F32).reshape(bs, ts, DA_HEADS, 2, DA_HD)
        eye = jnp.eye(2, dtype=F32)
        qm = jnp.einsum('bthmd,mn->bhmtnd', q4, eye).reshape(bs, DA_HEADS * 2 * ts, 2 * DA_HD)
        a_s = _attn_paged(page_table, lam, qm, pad_t(k), pad_t(v), da_g, cache_k2, cache_v2,
                          li * n_pool, ts, out_scale, npg=8)
        r_s, st_s = _retention(pad_t(rqk), pad_t(rest), state_ret[li],
                               ret_norm_g[li], log_g, tpad, ts)
        ks_l.append(k.reshape(bs, ts, DA_HEADS, 2, DA_HD))
        vs_l.append(v.reshape(bs, ts, DA_HEADS, DA_VD))
        rs_l.append(st_s)
        h1_s, hn_s, route_s, cnt = _finish(
            a_s.reshape(msm, -1), r_s[:, :ts].reshape(msm, -1), rest, hs, wa, wr, wo,
            norm_ffn_g[li], wrt, brt, cnt_p, msm)

        hn_all = jnp.concatenate([hn_p, hn_s], axis=0)
        route_all = jnp.concatenate([route_p, route_s], axis=0)
        yk = _moe(hn_all, route_all, cnt[0, :N_EXPERTS], w_exp_gate[li], w_exp_up[li],
                  w_exp_down[li])

        gfin = final_norm_g
        hp = _ple(h1_p, yk, 0, route_p, p_prompt[li].reshape(mp, -1), norm_ple_g[li], wpg, wpp,
                  gfin, tl_p)
        hs = _ple(h1_s, yk, mp, route_s, p_sample[li].reshape(msm, -1), norm_ple_g[li], wpg, wpp,
                  gfin, msm)
    y_prompt = hp.reshape(bp, sp, d)
    y_sample = hs.reshape(bs, ts, d)
    return (y_prompt, y_sample, jnp.stack(kp_l), jnp.stack(vp_l), jnp.stack(rp_l),
            jnp.stack(ks_l), jnp.stack(vs_l), jnp.stack(rs_l))
```

```python
import functools
import math

import jax
import jax.numpy as jnp
import numpy as np
from jax import lax
from jax.experimental import pallas as pl
from jax.experimental.pallas import tpu as pltpu

F32 = jnp.float32
BF16 = jnp.bfloat16
I32 = jnp.int32

PAST_LEN = 16384
DA_HEADS = 8
DA_VD = 128
DA_HD = 64
DA_ROT = 16
ROPE_THETA = 500000.0
RET_HEADS = 8
RET_VD = 128
RET_KD = 64
RET_THETA = 10000.0
N_GROUPS = 4
EXPERTS_PER_GROUP = 8
N_EXPERTS = N_GROUPS * EXPERTS_PER_GROUP
EPS = 1e-6
NEG = -1e30
LANES = 128

VMEM_LIMIT = 56 * 1024 * 1024

COL_Q, COL_K, COL_V, COL_RQK, COL_RV = 0, 1024, 2048, 3072, 4096

MOE_ROWS = 768
MOE_SUB = 128
MOE_FSPLIT = 2


def _params(sem, vmem=VMEM_LIMIT):
    return pltpu.CompilerParams(dimension_semantics=sem, vmem_limit_bytes=vmem)


def _rmsnorm_kernel(x_ref, g_ref, o_ref):
    x = x_ref[...]
    ms = jnp.mean(x * x, axis=-1, keepdims=True)
    o_ref[...] = (x * lax.rsqrt(ms + EPS) * g_ref[...]).astype(o_ref.dtype)


def _rmsnorm(x, g, tm):
    m, d = x.shape
    return pl.pallas_call(
        _rmsnorm_kernel,
        out_shape=jax.ShapeDtypeStruct((m, d), BF16),
        grid=(m // tm,),
        in_specs=[pl.BlockSpec((tm, d), lambda i: (i, 0)),
                  pl.BlockSpec((1, d), lambda i: (0, 0))],
        out_specs=pl.BlockSpec((tm, d), lambda i: (i, 0)),
        compiler_params=_params(("arbitrary",)),
        name="rmsnorm",
    )(x, g.reshape(1, d))


def _proj_kernel(*refs, shift, tn, seq_tiles):
    if shift:
        xn_ref, w_ref, c_ref, s1_ref, s2_ref, o_ref, wbf_ref = refs
    else:
        xn_ref, w_ref, o_ref, wbf_ref = refs

    @pl.when(pl.program_id(1) == 0)
    def _():
        wbf_ref[...] = w_ref[...].astype(BF16)

    acc = jnp.dot(xn_ref[...], wbf_ref[...], preferred_element_type=F32)
    if shift:
        groups = c_ref.shape[0]
        rep = tn // LANES // groups
        wide = lambda ref: jnp.concatenate(
            [jnp.tile(ref[g], (1, rep)) for g in range(groups)], axis=1)
        acc = (acc * wide(c_ref) + pltpu.roll(acc, tn - shift, 1) * wide(s1_ref)
               + pltpu.roll(acc, shift, 1) * wide(s2_ref))
    if seq_tiles:
        o_ref[0] = acc.T.astype(o_ref.dtype)
    else:
        o_ref[...] = acc.astype(o_ref.dtype)


def _proj(xn, w_in, col0, ncols, tm, out_dtype, rope=None, seq_tiles=0):
    m, d = xn.shape
    tn = 1024
    nj = ncols // tn
    j0 = col0 // tn
    in_specs = [pl.BlockSpec((tm, d), lambda j, i: (i, 0)),
                pl.BlockSpec((d, tn), lambda j, i: (0, j + j0))]
    args = [xn, w_in]
    shift = 0
    if rope is not None:
        assert nj == 1
        tabs, shift = rope
        groups, npos = tabs.shape[1], tabs.shape[2]
        npb = npos // tm
        for t in range(3):
            in_specs.append(pl.BlockSpec((groups, tm, LANES), lambda j, i: (0, i % npb, 0)))
            args.append(tabs[t])
    if seq_tiles:
        out_shape = jax.ShapeDtypeStruct((m // (seq_tiles * tm), ncols, seq_tiles * tm), out_dtype)
        out_spec = pl.BlockSpec((1, tn, tm), lambda j, i: (i // seq_tiles, j, i % seq_tiles))
    else:
        out_shape = jax.ShapeDtypeStruct((m, ncols), out_dtype)
        out_spec = pl.BlockSpec((tm, tn), lambda j, i: (i, j))
    return pl.pallas_call(
        functools.partial(_proj_kernel, shift=shift, tn=tn, seq_tiles=seq_tiles),
        out_shape=out_shape,
        grid=(nj, m // tm),
        in_specs=in_specs,
        out_specs=out_spec,
        scratch_shapes=[pltpu.VMEM((d, tn), BF16)],
        compiler_params=_params(("arbitrary", "arbitrary")),
        name="in_proj",
    )(*args)


def _rope_tables(pos, rot_dim, period, theta, scale):
    half = rot_dim // 2
    inv = (1.0 / np.power(np.float32(theta), np.arange(half, dtype=np.float32)
                          * np.float32(2.0 / rot_dim))).astype(np.float32)
    ang = (pos.astype(np.float32)[:, None] * inv[None, :]).astype(np.float64)
    cos, sin = np.cos(ang), np.sin(ang)
    npos = pos.shape[0]
    pad = period - rot_dim
    c = np.concatenate([cos, cos, np.ones((npos, pad))], axis=1)
    s1 = np.concatenate([-sin, np.zeros((npos, half + pad))], axis=1)
    s2 = np.concatenate([np.zeros((npos, half)), sin, np.zeros((npos, pad))], axis=1)
    tabs = np.stack([c, s1, s2]).astype(np.float32) * np.float32(scale)
    return np.tile(tabs, (1, 1, LANES // period)).astype(np.float32)


def _attn_prompt_kernel(lam_ref, q_ref, k_ref, v_ref, g_ref, o_ref,
                        kb_ref, vt_ref, m_ref, acc_ref, *, tq, hp, out_scale):
    qi = pl.program_id(2)
    nblk = kb_ref.shape[1]
    ext = vt_ref.shape[2]

    @pl.when(qi == 0)
    def _():
        for c in range(nblk):
            for h in range(hp):
                kb_ref[h, c] = k_ref[0, h * LANES:(h + 1) * LANES,
                                     c * tq:(c + 1) * tq].T.astype(BF16)

        def stage(c, carry):
            off = pl.multiple_of(c * tq, tq)
            for h in range(hp):
                cols = slice(h * LANES, (h + 1) * LANES)
                vt_ref[h, c, :LANES, :] = v_ref[pl.ds(off, tq), cols].T.astype(BF16)
                vt_ref[h, c, LANES:, :] = jnp.ones((ext - LANES, tq), BF16)
            return carry
        lax.fori_loop(0, nblk, stage, 0)

    row = lax.broadcasted_iota(I32, (LANES, tq), 0)
    qqt = []
    for h in range(hp):
        qt = q_ref[:, h * LANES:(h + 1) * LANES].astype(F32).T
        qqt.append(jnp.concatenate([jnp.where(row < DA_HD, qt, 0.0),
                                    jnp.where(row >= DA_HD, qt, 0.0)], axis=1).astype(BF16))
    m_ref[...] = jnp.full(m_ref.shape, NEG, F32)
    acc_ref[...] = jnp.zeros(acc_ref.shape, F32)

    def step(j, masked):
        sts = [jnp.dot(kb_ref[h, j], qqt[h], preferred_element_type=F32)
               for h in range(hp)]
        for h in range(hp):
            st = sts[h]
            if masked:
                key = lax.broadcasted_iota(I32, (tq, 2 * tq), 0)
                qry = lax.broadcasted_iota(I32, (tq, 2 * tq), 1) & (tq - 1)
                st = jnp.where(key <= qry, st, NEG)
            m_old = m_ref[h]
            m_new = jnp.maximum(m_old, jnp.max(st, axis=0, keepdims=True))
            alpha = jnp.exp2(m_old - m_new)
            p = jnp.exp2(st - m_new)
            acc_ref[h] = alpha * acc_ref[h] + jnp.dot(vt_ref[h, j], p.astype(BF16),
                                                      preferred_element_type=F32)
            m_ref[h] = m_new

    def body(j, carry):
        step(j, False)
        return carry

    lax.fori_loop(0, qi, body, 0)
    step(qi, True)

    for h in range(hp):
        cols = slice(h * LANES, (h + 1) * LANES)
        o = acc_ref[h, :LANES, :] / acc_ref[h, LANES:LANES + 1, :]
        out = (o[:, :tq] - lam_ref[0] * o[:, tq:]).T
        ms = jnp.mean(out * out, axis=-1, keepdims=True)
        out = out * lax.rsqrt(ms + EPS) * g_ref[:, cols] * out_scale
        o_ref[:, cols] = out.astype(o_ref.dtype)


def _attn_prompt(lam, q, k, v, g, b, s, out_scale, tq, hp):
    nq = s // tq
    w = hp * LANES
    return pl.pallas_call(
        functools.partial(_attn_prompt_kernel, tq=tq, hp=hp, out_scale=out_scale),
        out_shape=jax.ShapeDtypeStruct((b * s, DA_HEADS * DA_VD), BF16),
        grid=(b, DA_HEADS // hp, nq),
        in_specs=[pl.BlockSpec(memory_space=pltpu.SMEM),
                  pl.BlockSpec((tq, w), lambda bi, h, qi: (bi * nq + qi, h)),
                  pl.BlockSpec((1, w, s), lambda bi, h, qi: (bi, h, 0)),
                  pl.BlockSpec((s, w), lambda bi, h, qi: (bi, h)),
                  pl.BlockSpec((1, w), lambda bi, h, qi: (0, h))],
        out_specs=pl.BlockSpec((tq, w), lambda bi, h, qi: (bi * nq + qi, h)),
        scratch_shapes=[pltpu.VMEM((hp, nq, tq, LANES), BF16),
                        pltpu.VMEM((hp, nq, LANES + 16, tq), BF16),
                        pltpu.VMEM((hp, 1, 2 * tq), F32),
                        pltpu.VMEM((hp, LANES + 16, 2 * tq), F32)],
        compiler_params=_params(("arbitrary", "arbitrary", "arbitrary")),
        name="attn_prompt",
    )(lam, q, k, v, g)


def _attn_paged_kernel(pt_ref, lam_ref, q_ref, kn_ref, vn_ref, g_ref, *rest,
                       npg, t, out_scale):
    k_refs = rest[:npg]
    v_refs = rest[npg:2 * npg]
    o_ref, m_ref, l_ref, acc_ref = rest[2 * npg:]
    c = pl.program_id(1)
    nc = pl.num_programs(1)
    rows = 2 * t
    page = k_refs[0].shape[2]

    @pl.when(c == 0)
    def _():
        m_ref[...] = jnp.full(m_ref.shape, NEG, F32)
        l_ref[...] = jnp.zeros(l_ref.shape, F32)
        acc_ref[...] = jnp.zeros(acc_ref.shape, F32)

    def update(scores, values):
        s_all = jnp.concatenate(scores, axis=0)
        m_old = m_ref[...]
        m_new = jnp.maximum(m_old, jnp.max(s_all, axis=-1, keepdims=True))
        alpha = jnp.exp(m_old - m_new)
        p = jnp.exp(s_all - m_new)
        l_ref[...] = alpha * l_ref[...] + jnp.sum(p, axis=-1, keepdims=True)
        m_ref[...] = m_new
        pb = p
        for h in range(DA_HEADS):
            r0 = h * rows
            pv = None
            for lo, hi, load_v in values[h]:
                d = jnp.dot(pb[r0:r0 + rows, lo:hi], load_v(), preferred_element_type=F32)
                pv = d if pv is None else pv + d
            acc_ref[r0:r0 + rows, :] = alpha[r0:r0 + rows] * acc_ref[r0:r0 + rows, :] + pv

    q = q_ref[0]
    scores, values = [], []
    for h in range(DA_HEADS):
        qh = q[h * rows:(h + 1) * rows]
        sh, vh = [], []
        for i in range(npg):
            kb = k_refs[i][0, h * LANES:(h + 1) * LANES, :]
            sh.append(jnp.dot(qh, kb, preferred_element_type=F32))
            vh.append((i * page, (i + 1) * page,
                       lambda i=i, h=h: v_refs[i][0, pl.ds(h, page, stride=DA_HEADS), :]))
        scores.append(jnp.concatenate(sh, axis=1))
        values.append(vh)
    update(scores, values)

    @pl.when(c == nc - 1)
    def _():
        tp = kn_ref.shape[1]
        scores, values = [], []
        row = lax.broadcasted_iota(I32, (rows, tp), 0) & (t - 1)
        col = lax.broadcasted_iota(I32, (rows, tp), 1)
        for h in range(DA_HEADS):
            qh = q[h * rows:(h + 1) * rows]
            kb = kn_ref[0, :, h * LANES:(h + 1) * LANES]
            s = lax.dot_general(qh, kb, (((1,), (1,)), ((), ())), preferred_element_type=F32)
            scores.append(jnp.where(col <= row, s, NEG))
            values.append([(0, tp, lambda h=h: vn_ref[0, :, h * LANES:(h + 1) * LANES])])
        update(scores, values)
        o = acc_ref[...] / l_ref[...]
        lam = lam_ref[0]
        for h in range(DA_HEADS):
            r0 = h * rows
            out = o[r0:r0 + t] - lam * o[r0 + t:r0 + rows]
            ms = jnp.mean(out * out, axis=-1, keepdims=True)
            gh = g_ref[:, h * LANES:(h + 1) * LANES]
            o_ref[0, :, h * LANES:(h + 1) * LANES] = (
                out * lax.rsqrt(ms + EPS) * gh * out_scale).astype(o_ref.dtype)


def _attn_paged(page_table, lam, qm, k_new, v_new, g, cache_k, cache_v, page0, t, out_scale, npg):
    db, n_pages = page_table.shape
    prow, pcol = cache_k.shape[1], cache_k.shape[2]
    assert cache_v.shape[1:] == (prow, pcol)
    w = k_new.shape[2]
    tp = k_new.shape[1]
    nc = n_pages // npg
    pt_flat = page_table.reshape(-1) + page0

    def page_spec(i):
        return pl.BlockSpec((1, prow, pcol),
                            lambda b, c, pt: (pt[b * n_pages + c * npg + i], 0, 0))

    in_specs = [pl.BlockSpec(memory_space=pltpu.SMEM),
                pl.BlockSpec((1, qm.shape[1], LANES), lambda b, c, pt: (b, 0, 0)),
                pl.BlockSpec((1, tp, w), lambda b, c, pt: (b, 0, 0)),
                pl.BlockSpec((1, tp, w), lambda b, c, pt: (b, 0, 0)),
                pl.BlockSpec((1, w), lambda b, c, pt: (0, 0))]
    in_specs += [page_spec(i) for i in range(npg)] * 2
    rows = DA_HEADS * 2 * t
    return pl.pallas_call(
        functools.partial(_attn_paged_kernel, npg=npg, t=t, out_scale=out_scale),
        out_shape=jax.ShapeDtypeStruct((db, t, w), BF16),
        grid_spec=pltpu.PrefetchScalarGridSpec(
            num_scalar_prefetch=1,
            grid=(db, nc),
            in_specs=in_specs,
            out_specs=pl.BlockSpec((1, t, w), lambda b, c, pt: (b, 0, 0)),
            scratch_shapes=[pltpu.VMEM((rows, 1), F32),
                            pltpu.VMEM((rows, 1), F32),
                            pltpu.VMEM((rows, LANES), F32)]),
        compiler_params=_params(("arbitrary", "arbitrary")),
        name="attn_paged",
    )(pt_flat, lam, qm, k_new, v_new, g, *([cache_k] * npg), *([cache_v] * npg))


def _ret_kernel(rq_ref, rk_ref, rv_ref, rg_ref, s0_ref, dmat_ref, dq_ref, dk_ref, gc_ref, g_ref,
                o_ref, sout_ref, st_ref):
    c = pl.program_id(1)

    @pl.when(c == 0)
    def _():
        st_ref[...] = s0_ref[0]

    for h in range(RET_HEADS):
        kq = slice(h * RET_KD, (h + 1) * RET_KD)
        vs = slice(h * RET_VD, (h + 1) * RET_VD)
        qb = rq_ref[0, :, kq].astype(BF16)
        k = rk_ref[0, :, kq]
        vb = rv_ref[0, :, vs].astype(BF16)
        st = st_ref[h]
        att = lax.dot_general(qb, k.astype(BF16), (((1,), (1,)), ((), ())),
                              preferred_element_type=F32) * dmat_ref[h]
        inner = jnp.dot(att.astype(BF16), vb, preferred_element_type=F32)
        cross = jnp.dot(qb, st.astype(BF16), preferred_element_type=F32) * dq_ref[h]
        r = inner + cross
        kd = (k * dk_ref[h]).astype(BF16)
        st_ref[h] = gc_ref[h] * st + lax.dot_general(
            kd, vb, (((0,), (0,)), ((), ())), preferred_element_type=F32)
        mu = jnp.mean(r, axis=-1, keepdims=True)
        xc = r - mu
        var = jnp.mean(xc * xc, axis=-1, keepdims=True)
        y = xc * lax.rsqrt(var + EPS) * g_ref[:, vs]
        rg = rg_ref[0, :, vs]
        o_ref[0, :, vs] = (rg * jax.nn.sigmoid(rg) * y).astype(o_ref.dtype)

    @pl.when(c == pl.num_programs(1) - 1)
    def _():
        sout_ref[0] = st_ref[...]


def _ret_tables(log_g, chunk, valid):
    f32 = np.float32
    idx = np.arange(chunk, dtype=f32)
    diff = idx[:, None] - idx[None, :]
    ex = lambda e: np.exp(e.astype(f32).astype(np.float64)).astype(f32)
    dmat = np.where(diff[None] >= 0, ex(np.maximum(diff, f32(0))[None] * log_g[:, None, None]),
                    f32(0))
    dq = ex((idx + f32(1))[None, :] * log_g[:, None])
    dk = np.where(idx[None, :] < valid,
                  ex((f32(valid - 1) - idx)[None, :] * log_g[:, None]), f32(0))
    gc = ex(f32(valid) * log_g)
    h = log_g.shape[0]
    return (dmat.astype(f32),
            np.ascontiguousarray(np.broadcast_to(dq[:, :, None], (h, chunk, RET_VD))).astype(f32),
            np.ascontiguousarray(np.broadcast_to(dk[:, :, None], (h, chunk, RET_KD))).astype(f32),
            np.ascontiguousarray(np.broadcast_to(gc[:, None, None], (h, 1, RET_VD))).astype(f32))


def _retention(rqk, rest, state0, g, log_g, chunk, valid):
    b, s, _ = rqk.shape
    nc = s // chunk
    dmat, dq, dk, gc = _ret_tables(log_g, chunk, valid)
    qw = RET_HEADS * RET_KD
    vw = RET_HEADS * RET_VD
    full = lambda shp: pl.BlockSpec(shp, lambda bi, c: (0,) * len(shp))
    return pl.pallas_call(
        _ret_kernel,
        out_shape=(jax.ShapeDtypeStruct((b, s, vw), BF16),
                   jax.ShapeDtypeStruct((b, RET_HEADS, RET_KD, RET_VD), F32)),
        grid=(b, nc),
        in_specs=[pl.BlockSpec((1, chunk, qw), lambda bi, c: (bi, c, 0)),
                  pl.BlockSpec((1, chunk, qw), lambda bi, c: (bi, c, 1)),
                  pl.BlockSpec((1, chunk, vw), lambda bi, c: (bi, c, 0)),
                  pl.BlockSpec((1, chunk, vw), lambda bi, c: (bi, c, 1)),
                  pl.BlockSpec((1, RET_HEADS, RET_KD, RET_VD), lambda bi, c: (bi, 0, 0, 0)),
                  full(dmat.shape), full(dq.shape), full(dk.shape), full(gc.shape),
                  full((1, vw))],
        out_specs=(pl.BlockSpec((1, chunk, vw), lambda bi, c: (bi, c, 0)),
                   pl.BlockSpec((1, RET_HEADS, RET_KD, RET_VD), lambda bi, c: (bi, 0, 0, 0))),
        scratch_shapes=[pltpu.VMEM((RET_HEADS, RET_KD, RET_VD), F32)],
        compiler_params=_params(("arbitrary", "arbitrary")),
        name="retention",
    )(rqk, rqk, rest, rest, state0, dmat, dq, dk, gc, g.reshape(1, vw))


def _finish_kernel(a_ref, r_ref, ga_ref, gr_ref, h_ref, wa_ref, wr_ref, wo_ref, gffn_ref,
                   wrt_ref, brt_ref, carry0_ref,
                   h1_ref, hn_ref, route_ref, cnt_ref, carry_ref, *, tm):
    i = pl.program_id(0)

    @pl.when(i == 0)
    def _():
        carry_ref[...] = carry0_ref[...]

    am = jnp.dot(a_ref[...], wa_ref[...], preferred_element_type=F32)
    rm = jnp.dot(r_ref[...], wr_ref[...], preferred_element_type=F32)
    mix = jax.nn.sigmoid(ga_ref[...]) * am + jax.nn.sigmoid(gr_ref[...]) * rm
    h1 = h_ref[...] + jnp.dot(mix.astype(BF16), wo_ref[...], preferred_element_type=F32)
    h1_ref[...] = h1
    ms = jnp.mean(h1 * h1, axis=-1, keepdims=True)
    hn = h1 * lax.rsqrt(ms + EPS) * gffn_ref[...]
    hn_ref[...] = hn
    logits = jnp.dot(hn.astype(BF16), wrt_ref[...], preferred_element_type=F32) + brt_ref[...]
    lane = lax.broadcasted_iota(I32, (tm, LANES), 1)
    is_g = (lane >= N_EXPERTS) & (lane < N_EXPERTS + N_GROUPS)
    glog = jnp.where(is_g, logits, -jnp.inf)
    gmax = jnp.max(glog, axis=-1, keepdims=True)
    gidx = jnp.min(jnp.where(glog == gmax, lane - N_EXPERTS, LANES), axis=-1, keepdims=True)
    gden = jnp.sum(jnp.where(is_g, jnp.exp(glog - gmax), 0.0), axis=-1, keepdims=True)
    gp = 1.0 / gden
    in_g = (lane < N_EXPERTS) & ((lane // EXPERTS_PER_GROUP) == gidx)
    e1 = jnp.where(in_g, logits, -jnp.inf)
    v1 = jnp.max(e1, axis=-1, keepdims=True)
    i1 = jnp.min(jnp.where(e1 == v1, lane, LANES), axis=-1, keepdims=True)
    e2 = jnp.where(lane == i1, -jnp.inf, e1)
    v2 = jnp.max(e2, axis=-1, keepdims=True)
    i2 = jnp.min(jnp.where(e2 == v2, lane, LANES), axis=-1, keepdims=True)
    tt = jnp.exp(v2 - v1)
    w1 = gp / (1.0 + tt)
    w2 = gp * tt / (1.0 + tt)
    oh = jnp.where((lane == i1) | (lane == i2), 1.0, 0.0)
    rr = lax.broadcasted_iota(I32, (tm, tm), 0)
    cc = lax.broadcasted_iota(I32, (tm, tm), 1)
    tri = jnp.where(cc < rr, 1.0, 0.0).astype(BF16)
    before = jnp.dot(tri, oh.astype(BF16), preferred_element_type=F32) + carry_ref[...]
    rank1 = jnp.sum(jnp.where(lane == i1, before, 0.0), axis=-1, keepdims=True)
    rank2 = jnp.sum(jnp.where(lane == i2, before, 0.0), axis=-1, keepdims=True)
    carry = carry_ref[...] + jnp.sum(oh, axis=0, keepdims=True)
    carry_ref[...] = carry
    cnt_ref[...] = carry
    cols = [i1.astype(F32), i2.astype(F32), w1, w2, rank1, rank2]
    route = jnp.zeros((tm, LANES), F32)
    for ci, val in enumerate(cols):
        route = jnp.where(lane == ci, val, route)
    route_ref[...] = route


def _finish(a, r, rest, h, wa, wr, wo, gffn, wrt, brt, carry0, tm):
    m, d = h.shape
    aw = a.shape[1]
    assert rest.shape[1] == 3 * d and 2 * aw == d
    const = lambda shp: pl.BlockSpec(shp, lambda i: (0,) * len(shp))
    return pl.pallas_call(
        functools.partial(_finish_kernel, tm=tm),
        out_shape=(jax.ShapeDtypeStruct((m, d), F32),
                   jax.ShapeDtypeStruct((m, d), F32),
                   jax.ShapeDtypeStruct((m, LANES), F32),
                   jax.ShapeDtypeStruct((1, LANES), F32)),
        grid=(m // tm,),
        in_specs=[pl.BlockSpec((tm, aw), lambda i: (i, 0)),
                  pl.BlockSpec((tm, aw), lambda i: (i, 0)),
                  pl.BlockSpec((tm, d), lambda i: (i, 1)),
                  pl.BlockSpec((tm, d), lambda i: (i, 2)),
                  pl.BlockSpec((tm, d), lambda i: (i, 0)),
                  const(wa.shape), const(wr.shape), const(wo.shape), const((1, d)),
                  const(wrt.shape), const((1, LANES)), const((1, LANES))],
        out_specs=(pl.BlockSpec((tm, d), lambda i: (i, 0)),
                   pl.BlockSpec((tm, d), lambda i: (i, 0)),
                   pl.BlockSpec((tm, LANES), lambda i: (i, 0)),
                   pl.BlockSpec((1, LANES), lambda i: (0, 0))),
        scratch_shapes=[pltpu.VMEM((1, LANES), F32)],
        compiler_params=_params(("arbitrary",)),
        name="finish",
    )(a, r, rest, rest, h, wa, wr, wo, gffn.reshape(1, d), wrt, brt, carry0)


def _moe_kernel(item_e, item_start, item_n, row_src,
                hn_hbm, wg_ref, wu_ref, wd_ref, yk_hbm,
                xg, yacc, gsem, ssem, *, sub):
    i = pl.program_id(0)
    f = pl.program_id(1)
    ni = pl.num_programs(0)
    nf = pl.num_programs(1)
    n = item_n[i]
    slot = lax.rem(i, 2)

    grp = 8

    def gather_copy(s, r, tok, rows=1):
        return pltpu.make_async_copy(hn_hbm.at[pl.ds(tok, rows)], xg.at[s, pl.ds(r, rows)], gsem)

    def scatter_copy(r, k, tok, rows=1):
        return pltpu.make_async_copy(yacc.at[pl.ds(r, rows)], yk_hbm.at[k, pl.ds(tok, rows)], ssem)

    def repeat(count, fn):
        def body(t, c):
            fn(t)
            return c
        lax.fori_loop(0, count, body, 0)

    def start_gather(item, s):
        first = item_start[item]

        def issue(t):
            for u in range(grp):
                r = t * grp + u
                gather_copy(s, r, row_src[first + r] >> 1).start()
        repeat((item_n[item] + grp - 1) // grp, issue)

    def wait_gather(count):
        repeat((count + grp - 1) // grp, lambda t: gather_copy(0, 0, 0, grp).wait())

    def wait_scatter(count):
        repeat(count // grp, lambda t: scatter_copy(0, 0, 0, grp).wait())
        repeat(lax.rem(count, grp), lambda t: scatter_copy(0, 0, 0).wait())

    @pl.when(f == 0)
    def _():
        @pl.when(i == 0)
        def _():
            xg[...] = jnp.zeros(xg.shape, xg.dtype)
            start_gather(0, 0)

        @pl.when(i > 0)
        def _():
            wait_scatter(item_n[i - 1])
        wait_gather(n)

    @pl.when((f == 1) & (i + 1 < ni))
    def _():
        start_gather(i + 1, 1 - slot)

    @pl.when(n > 0)
    def _():
        def run_rows(off, m):
            x = xg[slot, pl.ds(off, m), :]
            hg = jnp.dot(x, wg_ref[0], preferred_element_type=F32)
            hu = jnp.dot(x, wu_ref[0], preferred_element_type=F32)
            hm = hg * jax.nn.sigmoid(hg) * hu
            part = jnp.dot(hm, wd_ref[0], preferred_element_type=F32)

            @pl.when(f == 0)
            def _():
                yacc[pl.ds(off, m), :] = part

            @pl.when(f > 0)
            def _():
                yacc[pl.ds(off, m), :] += part

        nsub = (n + sub - 1) // sub
        for k in range(1, xg.shape[1] // sub + 1):
            @pl.when(nsub == k)
            def _(k=k):
                run_rows(0, k * sub)

    @pl.when(f == nf - 1)
    def _():
        first = item_start[i]

        def issue_row(r):
            src = row_src[first + r]
            scatter_copy(r, src & 1, src >> 1).start()

        def issue_group(t):
            for u in range(grp):
                issue_row(t * grp + u)
        repeat(n // grp, issue_group)
        tail = (n // grp) * grp
        repeat(n - tail, lambda t: issue_row(tail + t))

        @pl.when(i == ni - 1)
        def _():
            wait_scatter(n)


def _moe_plan(route, counts, rows_per_item, max_items):
    ntok = route.shape[0]
    eid = route[:, 0:2].astype(I32)
    rank = route[:, 4:6].astype(I32)
    counts = counts.astype(I32)
    ends = jnp.cumsum(counts)
    starts = ends - counts
    dest = (starts[eid] + rank).reshape(-1)
    src = jnp.arange(2 * ntok, dtype=I32)
    row_src = jnp.zeros((2 * ntok + 8,), I32).at[dest].set(src)
    nit = (counts + rows_per_item - 1) // rows_per_item
    it_end = jnp.cumsum(nit)
    it_first = it_end - nit
    total = it_end[-1]
    t = jnp.arange(max_items, dtype=I32)
    e = jnp.minimum(jnp.searchsorted(it_end, t, side='right'), N_EXPERTS - 1).astype(I32)
    j = t - it_first[e]
    n = jnp.clip(counts[e] - j * rows_per_item, 0, rows_per_item)
    n = jnp.where(t < total, n, 0).astype(I32)
    start = (starts[e] + j * rows_per_item).astype(I32)
    start = jnp.where(n > 0, start, 0)
    last_e = e[jnp.maximum(total - 1, 0)]
    e = jnp.where(t < total, e, last_e)
    return e, start, n, row_src


def _moe(hn, route, counts, wg, wu, wd):
    ntok, d = hn.shape
    ne, _, ff = wg.shape
    rows, sub, nf = MOE_ROWS, MOE_SUB, MOE_FSPLIT
    assert nf > 1 and rows % sub == 0
    tf = ff // nf
    max_items = ne + (2 * ntok) // rows
    item_e, item_start, item_n, row_src = _moe_plan(route, counts, rows, max_items)

    def f_eff(i, f, item_n):
        return jnp.where(item_n[i] > 0, f, nf - 1)

    return pl.pallas_call(
        functools.partial(_moe_kernel, sub=sub),
        out_shape=jax.ShapeDtypeStruct((2, ntok, d), F32),
        grid_spec=pltpu.PrefetchScalarGridSpec(
            num_scalar_prefetch=4,
            grid=(max_items, nf),
            in_specs=[pl.BlockSpec(memory_space=pl.ANY),
                      pl.BlockSpec((1, d, tf), lambda i, f, ie, ist, inn, rs: (ie[i], 0, f_eff(i, f, inn))),
                      pl.BlockSpec((1, d, tf), lambda i, f, ie, ist, inn, rs: (ie[i], 0, f_eff(i, f, inn))),
                      pl.BlockSpec((1, tf, d), lambda i, f, ie, ist, inn, rs: (ie[i], f_eff(i, f, inn), 0))],
            out_specs=pl.BlockSpec(memory_space=pl.ANY),
            scratch_shapes=[pltpu.VMEM((2, rows, d), F32),
                            pltpu.VMEM((rows, d), F32),
                            pltpu.SemaphoreType.DMA,
                            pltpu.SemaphoreType.DMA]),
        compiler_params=_params(("arbitrary", "arbitrary")),
        name="moe_experts",
    )(item_e, item_start, item_n, row_src, hn, wg, wu, wd)


def _ple_kernel(h1_ref, y0_ref, y1_ref, route_ref, pe_ref, gple_ref, wg_ref, wp_ref, gfin_ref,
                o_ref):
    route = route_ref[...]
    y = route[:, 2:3] * y0_ref[0] + route[:, 3:4] * y1_ref[0]
    h2 = h1_ref[...] + y
    ms = jnp.mean(h2 * h2, axis=-1, keepdims=True)
    hn = (h2 * lax.rsqrt(ms + EPS) * gple_ref[...]).astype(BF16)
    gate = jax.nn.sigmoid(jnp.dot(hn, wg_ref[...], preferred_element_type=F32))
    pp = jnp.dot(pe_ref[...].astype(BF16), wp_ref[...], preferred_element_type=F32)
    h3 = h2 + gate * pp
    ms = jnp.mean(h3 * h3, axis=-1, keepdims=True)
    o_ref[...] = h3 * lax.rsqrt(ms + EPS) * gfin_ref[...]


def _ple(h1, yk, row0, route, pe, gple, wg, wp, gfin, tm):
    m, d = h1.shape
    blk0 = row0 // tm
    const = lambda shp: pl.BlockSpec(shp, lambda i: (0,) * len(shp))
    return pl.pallas_call(
        _ple_kernel,
        out_shape=jax.ShapeDtypeStruct((m, d), F32),
        grid=(m // tm,),
        in_specs=[pl.BlockSpec((tm, d), lambda i: (i, 0)),
                  pl.BlockSpec((1, tm, d), lambda i: (0, i + blk0, 0)),
                  pl.BlockSpec((1, tm, d), lambda i: (1, i + blk0, 0)),
                  pl.BlockSpec((tm, LANES), lambda i: (i, 0)),
                  pl.BlockSpec((tm, pe.shape[1]), lambda i: (i, 0)),
                  const((1, d)), const(wg.shape), const(wp.shape), const((1, d))],
        out_specs=pl.BlockSpec((tm, d), lambda i: (i, 0)),
        compiler_params=_params(("arbitrary",)),
        name="ple_final",
    )(h1, yk, yk, route, pe, gple.reshape(1, d), wg, wp, gfin.reshape(1, d))


def _project_group(x, g, w_in, pos, tm, q_scale, k_seq_tiles=0):
    xn = _rmsnorm(x, g, tm)
    da_q = _rope_tables(pos, DA_ROT, DA_HD, ROPE_THETA, q_scale)[:, None]
    da_k = _rope_tables(pos, DA_ROT, DA_HD, ROPE_THETA, 1.0)[:, None]
    rq_t = _rope_tables(pos, RET_KD, RET_KD, RET_THETA, 1.0)
    rk_t = _rope_tables(pos, RET_KD, RET_KD, RET_THETA, RET_KD ** -0.5)
    ret_t = np.stack([rq_t, rk_t], axis=1)
    q = _proj(xn, w_in, COL_Q, 1024, tm, BF16, rope=(da_q, DA_ROT // 2))
    k = _proj(xn, w_in, COL_K, 1024, tm, F32, rope=(da_k, DA_ROT // 2), seq_tiles=k_seq_tiles)
    v = _proj(xn, w_in, COL_V, 1024, tm, F32)
    rqk = _proj(xn, w_in, COL_RQK, 1024, tm, F32, rope=(ret_t, RET_KD // 2))
    rest = _proj(xn, w_in, COL_RV, 6144, tm, F32)
    return q, k, v, rqk, rest


def kernel(x_prompt, x_sample, cache_k, cache_v, state_ret, page_table, p_prompt, p_sample, norm_mix_g, w_in, lam_q1, lam_k1, lam_q2, lam_k2, da_norm_g, ret_norm_g, w_br_attn, w_br_ret, w_out, norm_ffn_g, w_router_group, b_router_group, w_router_expert, b_router_expert, w_exp_gate, w_exp_up, w_exp_down, norm_ple_g, w_ple_gate, w_ple_proj, final_norm_g):
    bp, sp, d = x_prompt.shape
    bs, ts, _ = x_sample.shape
    depth = w_in.shape[0]
    assert depth == 1, "the final norm is fused into the last stage of a single layer"
    n_pool, page = cache_k.shape[1], cache_k.shape[2]
    mp, msm = bp * sp, bs * ts
    log_g = np.log1p(-np.exp2(-5.0 - np.arange(RET_HEADS))).astype(np.float32)
    pos_p = np.arange(sp)
    pos_s = np.tile(PAST_LEN + np.arange(ts), bs)
    cache_k2 = cache_k.transpose(0, 1, 3, 4, 5, 2).reshape(depth * n_pool, DA_HEADS * 2 * DA_HD, page)
    cache_v2 = cache_v.reshape(depth * n_pool, page * DA_HEADS, DA_VD)
    tm_p = next(t for t in (1024, 512, 256, 128) if sp % t == 0)
    tf_p = 256
    tl_p = 512 if mp % 512 == 0 else 256

    hp = x_prompt.reshape(mp, d)
    hs = x_sample.reshape(msm, d)
    kp_l, vp_l, rp_l, ks_l, vs_l, rs_l = [], [], [], [], [], []
    for li in range(depth):
        lam_init = 0.8 - 0.6 * math.exp(-0.3 * li)
        lam = (jnp.exp(jnp.sum(lam_q1[li] * lam_k1[li]))
               - jnp.exp(jnp.sum(lam_q2[li] * lam_k2[li])) + lam_init).reshape(1).astype(F32)
        out_scale = 1.0 - lam_init
        wa = w_br_attn[li].astype(BF16)
        wr = w_br_ret[li].astype(BF16)
        wo = w_out[li].astype(BF16)
        wpg = w_ple_gate[li].astype(BF16)
        wpp = w_ple_proj[li].astype(BF16)
        wrt = jnp.zeros((d, LANES), F32)
        wrt = wrt.at[:, :N_EXPERTS].set(w_router_expert[li])
        wrt = wrt.at[:, N_EXPERTS:N_EXPERTS + N_GROUPS].set(w_router_group[li]).astype(BF16)
        brt = jnp.zeros((1, LANES), F32)
        brt = brt.at[0, :N_EXPERTS].set(b_router_expert[li])
        brt = brt.at[0, N_EXPERTS:N_EXPERTS + N_GROUPS].set(b_router_group[li])
        da_g = da_norm_g[li].reshape(1, -1)

        q, k, v, rqk, rest = _project_group(hp, norm_mix_g[li], w_in[li], pos_p, tm_p,
                                            DA_HD ** -0.5 * math.log2(math.e),
                                            k_seq_tiles=sp // tm_p)
        a_p = _attn_prompt(lam, q, k, v, da_g, bp, sp, out_scale, tq=256, hp=4)
        chunk = 256 if sp % 256 == 0 else 128
        r_p, st_p = _retention(rqk.reshape(bp, sp, -1), rest.reshape(bp, sp, -1),
                               jnp.zeros((bp, RET_HEADS, RET_KD, RET_VD), F32),
                               ret_norm_g[li], log_g, chunk, chunk)
        kp_l.append(k.reshape(bp, DA_HEADS, 2, DA_HD, sp).transpose(0, 4, 1, 2, 3))
        vp_l.append(v.reshape(bp, sp, DA_HEADS, DA_VD))
        rp_l.append(st_p)
        h1_p, hn_p, route_p, cnt_p = _finish(
            a_p, r_p.reshape(mp, -1), rest, hp, wa, wr, wo, norm_ffn_g[li], wrt, brt,
            jnp.zeros((1, LANES), F32), tf_p)

        q, k, v, rqk, rest = _project_group(hs, norm_mix_g[li], w_in[li], pos_s, msm,
                                            DA_HD ** -0.5)
        tpad = 8
        pad_t = lambda z: jnp.pad(z.reshape(bs, ts, -1), ((0, 0), (0, tpad - ts), (0, 0)))
        q4 = q.astype(F32).reshape(bs, ts, DA_HEADS, 2, DA_HD)
        eye = jnp.eye(2, dtype=F32)
        qm = jnp.einsum('bthmd,mn->bhmtnd', q4, eye).reshape(bs, DA_HEADS * 2 * ts, 2 * DA_HD)
        a_s = _attn_paged(page_table, lam, qm, pad_t(k), pad_t(v), da_g, cache_k2, cache_v2,
                          li * n_pool, ts, out_scale, npg=8)
        r_s, st_s = _retention(pad_t(rqk), pad_t(rest), state_ret[li],
                               ret_norm_g[li], log_g, tpad, ts)
        ks_l.append(k.reshape(bs, ts, DA_HEADS, 2, DA_HD))
        vs_l.append(v.reshape(bs, ts, DA_HEADS, DA_VD))
        rs_l.append(st_s)
        h1_s, hn_s, route_s, cnt = _finish(
            a_s.reshape(msm, -1), r_s[:, :ts].reshape(msm, -1), rest, hs, wa, wr, wo,
            norm_ffn_g[li], wrt, brt, cnt_p, msm)

        hn_all = jnp.concatenate([hn_p, hn_s], axis=0)
        route_all = jnp.concatenate([route_p, route_s], axis=0)
        yk = _moe(hn_all, route_all, cnt[0, :N_EXPERTS], w_exp_gate[li], w_exp_up[li],
                  w_exp_down[li])

        gfin = final_norm_g
        hp = _ple(h1_p, yk, 0, route_p, p_prompt[li].reshape(mp, -1), norm_ple_g[li], wpg, wpp,
                  gfin, tl_p)
        hs = _ple(h1_s, yk, mp, route_s, p_sample[li].reshape(msm, -1), norm_ple_g[li], wpg, wpp,
                  gfin, msm)
    y_prompt = hp.reshape(bp, sp, d)
    y_sample = hs.reshape(bs, ts, d)
    return (y_prompt, y_sample, jnp.stack(kp_l), jnp.stack(vp_l), jnp.stack(rp_l),
            jnp.stack(ks_l), jnp.stack(vs_l), jnp.stack(rs_l))
```

```python
import functools
import math

import jax
import jax.numpy as jnp
import numpy as np
from jax import lax
from jax.experimental import pallas as pl
from jax.experimental.pallas import tpu as pltpu

F32 = jnp.float32
BF16 = jnp.bfloat16
I32 = jnp.int32

PAST_LEN = 16384
DA_HEADS = 8
DA_VD = 128
DA_HD = 64
DA_ROT = 16
ROPE_THETA = 500000.0
RET_HEADS = 8
RET_VD = 128
RET_KD = 64
RET_THETA = 10000.0
N_GROUPS = 4
EXPERTS_PER_GROUP = 8
N_EXPERTS = N_GROUPS * EXPERTS_PER_GROUP
EPS = 1e-6
NEG = -1e30
LANES = 128

VMEM_LIMIT = 56 * 1024 * 1024

COL_Q, COL_K, COL_V, COL_RQK, COL_RV = 0, 1024, 2048, 3072, 4096

MOE_ROWS = 768
MOE_SUB = 128
MOE_FSPLIT = 2


def _params(sem, vmem=VMEM_LIMIT):
    return pltpu.CompilerParams(dimension_semantics=sem, vmem_limit_bytes=vmem)


def _rmsnorm_kernel(x_ref, g_ref, o_ref):
    x = x_ref[...]
    ms = jnp.mean(x * x, axis=-1, keepdims=True)
    o_ref[...] = (x * lax.rsqrt(ms + EPS) * g_ref[...]).astype(o_ref.dtype)


def _rmsnorm(x, g, tm):
    m, d = x.shape
    return pl.pallas_call(
        _rmsnorm_kernel,
        out_shape=jax.ShapeDtypeStruct((m, d), BF16),
        grid=(m // tm,),
        in_specs=[pl.BlockSpec((tm, d), lambda i: (i, 0)),
                  pl.BlockSpec((1, d), lambda i: (0, 0))],
        out_specs=pl.BlockSpec((tm, d), lambda i: (i, 0)),
        compiler_params=_params(("arbitrary",)),
        name="rmsnorm",
    )(x, g.reshape(1, d))


def _proj_kernel(*refs, shift, tn, seq_tiles):
    if shift:
        xn_ref, w_ref, c_ref, s1_ref, s2_ref, o_ref, wbf_ref = refs
    else:
        xn_ref, w_ref, o_ref, wbf_ref = refs

    @pl.when(pl.program_id(1) == 0)
    def _():
        wbf_ref[...] = w_ref[...].astype(BF16)

    acc = jnp.dot(xn_ref[...], wbf_ref[...], preferred_element_type=F32)
    if shift:
        groups = c_ref.shape[0]
        rep = tn // LANES // groups
        wide = lambda ref: jnp.concatenate(
            [jnp.tile(ref[g], (1, rep)) for g in range(groups)], axis=1)
        acc = (acc * wide(c_ref) + pltpu.roll(acc, tn - shift, 1) * wide(s1_ref)
               + pltpu.roll(acc, shift, 1) * wide(s2_ref))
    if seq_tiles:
        o_ref[0] = acc.T.astype(o_ref.dtype)
    else:
        o_ref[...] = acc.astype(o_ref.dtype)


def _proj(xn, w_in, col0, ncols, tm, out_dtype, rope=None, seq_tiles=0):
    m, d = xn.shape
    tn = 1024
    nj = ncols // tn
    j0 = col0 // tn
    in_specs = [pl.BlockSpec((tm, d), lambda j, i: (i, 0)),
                pl.BlockSpec((d, tn), lambda j, i: (0, j + j0))]
    args = [xn, w_in]
    shift = 0
    if rope is not None:
        assert nj == 1
        tabs, shift = rope
        groups, npos = tabs.shape[1], tabs.shape[2]
        npb = npos // tm
        for t in range(3):
            in_specs.append(pl.BlockSpec((groups, tm, LANES), lambda j, i: (0, i % npb, 0)))
            args.append(tabs[t])
    if seq_tiles:
        out_shape = jax.ShapeDtypeStruct((m // (seq_tiles * tm), ncols, seq_tiles * tm), out_dtype)
        out_spec = pl.BlockSpec((1, tn, tm), lambda j, i: (i // seq_tiles, j, i % seq_tiles))
    else:
        out_shape = jax.ShapeDtypeStruct((m, ncols), out_dtype)
        out_spec = pl.BlockSpec((tm, tn), lambda j, i: (i, j))
    return pl.pallas_call(
        functools.partial(_proj_kernel, shift=shift, tn=tn, seq_tiles=seq_tiles),
        out_shape=out_shape,
        grid=(nj, m // tm),
        in_specs=in_specs,
        out_specs=out_spec,
        scratch_shapes=[pltpu.VMEM((d, tn), BF16)],
        compiler_params=_params(("arbitrary", "arbitrary")),
        name="in_proj",
    )(*args)


def _rope_tables(pos, rot_dim, period, theta, scale):
    half = rot_dim // 2
    inv = (1.0 / np.power(np.float32(theta), np.arange(half, dtype=np.float32)
                          * np.float32(2.0 / rot_dim))).astype(np.float32)
    ang = (pos.astype(np.float32)[:, None] * inv[None, :]).astype(np.float64)
    cos, sin = np.cos(ang), np.sin(ang)
    npos = pos.shape[0]
    pad = period - rot_dim
    c = np.concatenate([cos, cos, np.ones((npos, pad))], axis=1)
    s1 = np.concatenate([-sin, np.zeros((npos, half + pad))], axis=1)
    s2 = np.concatenate([np.zeros((npos, half)), sin, np.zeros((npos, pad))], axis=1)
    tabs = np.stack([c, s1, s2]).astype(np.float32) * np.float32(scale)
    return np.tile(tabs, (1, 1, LANES // period)).astype(np.float32)


def _attn_prompt_kernel(lam_ref, q_ref, k_ref, v_ref, g_ref, o_ref,
                        kb_ref, vt_ref, m_ref, acc_ref, *, tq, hp, out_scale):
    qi = pl.program_id(2)
    nblk = kb_ref.shape[1]
    ext = vt_ref.shape[2]

    @pl.when(qi == 0)
    def _():
        for c in range(nblk):
            for h in range(hp):
                kb_ref[h, c] = k_ref[0, h * LANES:(h + 1) * LANES,
                                     c * tq:(c + 1) * tq].T.astype(BF16)

        def stage(c, carry):
            off = pl.multiple_of(c * tq, tq)
            for h in range(hp):
                cols = slice(h * LANES, (h + 1) * LANES)
                vt_ref[h, c, :LANES, :] = v_ref[pl.ds(off, tq), cols].T.astype(BF16)
                vt_ref[h, c, LANES:, :] = jnp.ones((ext - LANES, tq), BF16)
            return carry
        lax.fori_loop(0, nblk, stage, 0)

    row = lax.broadcasted_iota(I32, (LANES, tq), 0)
    qqt = []
    for h in range(hp):
        qt = q_ref[:, h * LANES:(h + 1) * LANES].astype(F32).T
        qqt.append(jnp.concatenate([jnp.where(row < DA_HD, qt, 0.0),
                                    jnp.where(row >= DA_HD, qt, 0.0)], axis=1).astype(BF16))
    m_ref[...] = jnp.full(m_ref.shape, NEG, F32)
    acc_ref[...] = jnp.zeros(acc_ref.shape, F32)

    def step(j, masked):
        sts = [jnp.dot(kb_ref[h, j], qqt[h], preferred_element_type=F32)
               for h in range(hp)]
        for h in range(hp):
            st = sts[h]
            if masked:
                key = lax.broadcasted_iota(I32, (tq, 2 * tq), 0)
                qry = lax.broadcasted_iota(I32, (tq, 2 * tq), 1) & (tq - 1)
                st = jnp.where(key <= qry, st, NEG)
            m_old = m_ref[h]
            m_new = jnp.maximum(m_old, jnp.max(st, axis=0, keepdims=True))
            alpha = jnp.exp2(m_old - m_new)
            p = jnp.exp2(st - m_new)
            acc_ref[h] = alpha * acc_ref[h] + jnp.dot(vt_ref[h, j], p.astype(BF16),
                                                      preferred_element_type=F32)
            m_ref[h] = m_new

    def body(j, carry):
        step(j, False)
        return carry

    lax.fori_loop(0, qi, body, 0)
    step(qi, True)

    for h in range(hp):
        cols = slice(h * LANES, (h + 1) * LANES)
        o = acc_ref[h, :LANES, :] / acc_ref[h, LANES:LANES + 1, :]
        out = (o[:, :tq] - lam_ref[0] * o[:, tq:]).T
        ms = jnp.mean(out * out, axis=-1, keepdims=True)
        out = out * lax.rsqrt(ms + EPS) * g_ref[:, cols] * out_scale
        o_ref[:, cols] = out.astype(o_ref.dtype)


def _attn_prompt(lam, q, k, v, g, b, s, out_scale, tq, hp):
    nq = s // tq
    w = hp * LANES
    return pl.pallas_call(
        functools.partial(_attn_prompt_kernel, tq=tq, hp=hp, out_scale=out_scale),
        out_shape=jax.ShapeDtypeStruct((b * s, DA_HEADS * DA_VD), BF16),
        grid=(b, DA_HEADS // hp, nq),
        in_specs=[pl.BlockSpec(memory_space=pltpu.SMEM),
                  pl.BlockSpec((tq, w), lambda bi, h, qi: (bi * nq + qi, h)),
                  pl.BlockSpec((1, w, s), lambda bi, h, qi: (bi, h, 0)),
                  pl.BlockSpec((s, w), lambda bi, h, qi: (bi, h)),
                  pl.BlockSpec((1, w), lambda bi, h, qi: (0, h))],
        out_specs=pl.BlockSpec((tq, w), lambda bi, h, qi: (bi * nq + qi, h)),
        scratch_shapes=[pltpu.VMEM((hp, nq, tq, LANES), BF16),
                        pltpu.VMEM((hp, nq, LANES + 16, tq), BF16),
                        pltpu.VMEM((hp, 1, 2 * tq), F32),
                        pltpu.VMEM((hp, LANES + 16, 2 * tq), F32)],
        compiler_params=_params(("arbitrary", "arbitrary", "arbitrary")),
        name="attn_prompt",
    )(lam, q, k, v, g)


def _attn_paged_kernel(pt_ref, lam_ref, q_ref, kn_ref, vn_ref, g_ref, *rest,
                       npg, t, out_scale):
    k_hbm, v_hbm, o_ref, m_ref, l_ref, acc_ref, kbuf, vbuf, ksem, vsem = rest
    c = pl.program_id(1)
    nc = pl.num_programs(1)
    rows = 2 * t
    page = kbuf.shape[3]

    nslot = kbuf.shape[0]
    step = pl.program_id(0) * nc + c
    nstep = pl.num_programs(0) * nc

    def page_copies(s, slot):
        out = []
        for i in range(npg):
            pid = pt_ref[s * npg + i]
            out.append(pltpu.make_async_copy(k_hbm.at[pid], kbuf.at[slot, i], ksem.at[slot]))
            out.append(pltpu.make_async_copy(v_hbm.at[pid], vbuf.at[slot, i], vsem.at[slot]))
        return out

    def start_pages(s):
        for cp in page_copies(s, lax.rem(s, nslot)):
            cp.start()

    @pl.when(step == 0)
    def _():
        for s0 in range(nslot - 1):
            @pl.when(s0 < nstep)
            def _(s0=s0):
                start_pages(jnp.int32(s0))

    @pl.when(step + nslot - 1 < nstep)
    def _():
        start_pages(step + nslot - 1)
    slot = lax.rem(step, nslot)
    for cp in page_copies(step, slot):
        cp.wait()
    k_refs = [kbuf.at[slot, i] for i in range(npg)]
    v_refs = [vbuf.at[slot, i] for i in range(npg)]

    @pl.when(c == 0)
    def _():
        m_ref[...] = jnp.full(m_ref.shape, NEG, F32)
        l_ref[...] = jnp.zeros(l_ref.shape, F32)
        acc_ref[...] = jnp.zeros(acc_ref.shape, F32)

    def update(scores, values):
        s_all = jnp.concatenate(scores, axis=0)
        m_old = m_ref[...]
        m_new = jnp.maximum(m_old, jnp.max(s_all, axis=-1, keepdims=True))
        alpha = jnp.exp(m_old - m_new)
        p = jnp.exp(s_all - m_new)
        l_ref[...] = alpha * l_ref[...] + jnp.sum(p, axis=-1, keepdims=True)
        m_ref[...] = m_new
        pb = p
        for h in range(DA_HEADS):
            r0 = h * rows
            pv = None
            for lo, hi, load_v in values[h]:
                d = jnp.dot(pb[r0:r0 + rows, lo:hi], load_v(), preferred_element_type=F32)
                pv = d if pv is None else pv + d
            acc_ref[r0:r0 + rows, :] = alpha[r0:r0 + rows] * acc_ref[r0:r0 + rows, :] + pv

    q = q_ref[0]
    scores, values = [], []
    for h in range(DA_HEADS):
        qh = q[h * rows:(h + 1) * rows]
        sh, vh = [], []
        for i in range(npg):
            kb = k_refs[i][h * LANES:(h + 1) * LANES, :]
            sh.append(jnp.dot(qh, kb, preferred_element_type=F32))
            vh.append((i * page, (i + 1) * page,
                       lambda i=i, h=h: v_refs[i][pl.ds(h, page, stride=DA_HEADS), :]))
        scores.append(jnp.concatenate(sh, axis=1))
        values.append(vh)
    update(scores, values)

    @pl.when(c == nc - 1)
    def _():
        tp = kn_ref.shape[1]
        scores, values = [], []
        row = lax.broadcasted_iota(I32, (rows, tp), 0) & (t - 1)
        col = lax.broadcasted_iota(I32, (rows, tp), 1)
        for h in range(DA_HEADS):
            qh = q[h * rows:(h + 1) * rows]
            kb = kn_ref[0, :, h * LANES:(h + 1) * LANES]
            s = lax.dot_general(qh, kb, (((1,), (1,)), ((), ())), preferred_element_type=F32)
            scores.append(jnp.where(col <= row, s, NEG))
            values.append([(0, tp, lambda h=h: vn_ref[0, :, h * LANES:(h + 1) * LANES])])
        update(scores, values)
        o = acc_ref[...] / l_ref[...]
        lam = lam_ref[0]
        for h in range(DA_HEADS):
            r0 = h * rows
            out = o[r0:r0 + t] - lam * o[r0 + t:r0 + rows]
            ms = jnp.mean(out * out, axis=-1, keepdims=True)
            gh = g_ref[:, h * LANES:(h + 1) * LANES]
            o_ref[0, :, h * LANES:(h + 1) * LANES] = (
                out * lax.rsqrt(ms + EPS) * gh * out_scale).astype(o_ref.dtype)


def _attn_paged(page_table, lam, qm, k_new, v_new, g, cache_k, cache_v, page0, t, out_scale, npg):
    db, n_pages = page_table.shape
    prow, pcol = cache_k.shape[1], cache_k.shape[2]
    assert cache_v.shape[1:] == (prow, pcol)
    w = k_new.shape[2]
    tp = k_new.shape[1]
    nc = n_pages // npg
    pt_flat = page_table.reshape(-1) + page0

    nslot = 3
    in_specs = [pl.BlockSpec(memory_space=pltpu.SMEM),
                pl.BlockSpec((1, qm.shape[1], LANES), lambda b, c, pt: (b, 0, 0)),
                pl.BlockSpec((1, tp, w), lambda b, c, pt: (b, 0, 0)),
                pl.BlockSpec((1, tp, w), lambda b, c, pt: (b, 0, 0)),
                pl.BlockSpec((1, w), lambda b, c, pt: (0, 0)),
                pl.BlockSpec(memory_space=pl.ANY),
                pl.BlockSpec(memory_space=pl.ANY)]
    rows = DA_HEADS * 2 * t
    return pl.pallas_call(
        functools.partial(_attn_paged_kernel, npg=npg, t=t, out_scale=out_scale),
        out_shape=jax.ShapeDtypeStruct((db, t, w), BF16),
        grid_spec=pltpu.PrefetchScalarGridSpec(
            num_scalar_prefetch=1,
            grid=(db, nc),
            in_specs=in_specs,
            out_specs=pl.BlockSpec((1, t, w), lambda b, c, pt: (b, 0, 0)),
            scratch_shapes=[pltpu.VMEM((rows, 1), F32),
                            pltpu.VMEM((rows, 1), F32),
                            pltpu.VMEM((rows, LANES), F32),
                            pltpu.VMEM((nslot, npg, prow, pcol), F32),
                            pltpu.VMEM((nslot, npg, prow, pcol), F32),
                            pltpu.SemaphoreType.DMA((nslot,)),
                            pltpu.SemaphoreType.DMA((nslot,))]),
        compiler_params=_params(("arbitrary", "arbitrary")),
        name="attn_paged",
    )(pt_flat, lam, qm, k_new, v_new, g, cache_k, cache_v)


def _ret_kernel(rq_ref, rk_ref, rv_ref, rg_ref, s0_ref, dmat_ref, dq_ref, dk_ref, gc_ref, g_ref,
                o_ref, sout_ref, st_ref):
    c = pl.program_id(1)

    @pl.when(c == 0)
    def _():
        st_ref[...] = s0_ref[0]

    for h in range(RET_HEADS):
        kq = slice(h * RET_KD, (h + 1) * RET_KD)
        vs = slice(h * RET_VD, (h + 1) * RET_VD)
        qb = rq_ref[0, :, kq].astype(BF16)
        k = rk_ref[0, :, kq]
        vb = rv_ref[0, :, vs].astype(BF16)
        st = st_ref[h]
        att = lax.dot_general(qb, k.astype(BF16), (((1,), (1,)), ((), ())),
                              preferred_element_type=F32) * dmat_ref[h]
        inner = jnp.dot(att.astype(BF16), vb, preferred_element_type=F32)
        cross = jnp.dot(qb, st.astype(BF16), preferred_element_type=F32) * dq_ref[h]
        r = inner + cross
        kd = (k * dk_ref[h]).astype(BF16)
        st_ref[h] = gc_ref[h] * st + lax.dot_general(
            kd, vb, (((0,), (0,)), ((), ())), preferred_element_type=F32)
        mu = jnp.mean(r, axis=-1, keepdims=True)
        xc = r - mu
        var = jnp.mean(xc * xc, axis=-1, keepdims=True)
        y = xc * lax.rsqrt(var + EPS) * g_ref[:, vs]
        rg = rg_ref[0, :, vs]
        o_ref[0, :, vs] = (rg * jax.nn.sigmoid(rg) * y).astype(o_ref.dtype)

    @pl.when(c == pl.num_programs(1) - 1)
    def _():
        sout_ref[0] = st_ref[...]


def _ret_tables(log_g, chunk, valid):
    f32 = np.float32
    idx = np.arange(chunk, dtype=f32)
    diff = idx[:, None] - idx[None, :]
    ex = lambda e: np.exp(e.astype(f32).astype(np.float64)).astype(f32)
    dmat = np.where(diff[None] >= 0, ex(np.maximum(diff, f32(0))[None] * log_g[:, None, None]),
                    f32(0))
    dq = ex((idx + f32(1))[None, :] * log_g[:, None])
    dk = np.where(idx[None, :] < valid,
                  ex((f32(valid - 1) - idx)[None, :] * log_g[:, None]), f32(0))
    gc = ex(f32(valid) * log_g)
    h = log_g.shape[0]
    return (dmat.astype(f32),
            np.ascontiguousarray(np.broadcast_to(dq[:, :, None], (h, chunk, RET_VD))).astype(f32),
            np.ascontiguousarray(np.broadcast_to(dk[:, :, None], (h, chunk, RET_KD))).astype(f32),
            np.ascontiguousarray(np.broadcast_to(gc[:, None, None], (h, 1, RET_VD))).astype(f32))


def _retention(rqk, rest, state0, g, log_g, chunk, valid):
    b, s, _ = rqk.shape
    nc = s // chunk
    dmat, dq, dk, gc = _ret_tables(log_g, chunk, valid)
    qw = RET_HEADS * RET_KD
    vw = RET_HEADS * RET_VD
    full = lambda shp: pl.BlockSpec(shp, lambda bi, c: (0,) * len(shp))
    return pl.pallas_call(
        _ret_kernel,
        out_shape=(jax.ShapeDtypeStruct((b, s, vw), BF16),
                   jax.ShapeDtypeStruct((b, RET_HEADS, RET_KD, RET_VD), F32)),
        grid=(b, nc),
        in_specs=[pl.BlockSpec((1, chunk, qw), lambda bi, c: (bi, c, 0)),
                  pl.BlockSpec((1, chunk, qw), lambda bi, c: (bi, c, 1)),
                  pl.BlockSpec((1, chunk, vw), lambda bi, c: (bi, c, 0)),
                  pl.BlockSpec((1, chunk, vw), lambda bi, c: (bi, c, 1)),
                  pl.BlockSpec((1, RET_HEADS, RET_KD, RET_VD), lambda bi, c: (bi, 0, 0, 0)),
                  full(dmat.shape), full(dq.shape), full(dk.shape), full(gc.shape),
                  full((1, vw))],
        out_specs=(pl.BlockSpec((1, chunk, vw), lambda bi, c: (bi, c, 0)),
                   pl.BlockSpec((1, RET_HEADS, RET_KD, RET_VD), lambda bi, c: (bi, 0, 0, 0))),
        scratch_shapes=[pltpu.VMEM((RET_HEADS, RET_KD, RET_VD), F32)],
        compiler_params=_params(("arbitrary", "arbitrary")),
        name="retention",
    )(rqk, rqk, rest, rest, state0, dmat, dq, dk, gc, g.reshape(1, vw))


def _finish_kernel(a_ref, r_ref, ga_ref, gr_ref, h_ref, wa_ref, wr_ref, wo_ref, gffn_ref,
                   wrt_ref, brt_ref, carry0_ref,
                   h1_ref, hn_ref, route_ref, cnt_ref, carry_ref, *, tm):
    i = pl.program_id(0)

    @pl.when(i == 0)
    def _():
        carry_ref[...] = carry0_ref[...]

    am = jnp.dot(a_ref[...], wa_ref[...], preferred_element_type=F32)
    rm = jnp.dot(r_ref[...], wr_ref[...], preferred_element_type=F32)
    mix = jax.nn.sigmoid(ga_ref[...]) * am + jax.nn.sigmoid(gr_ref[...]) * rm
    h1 = h_ref[...] + jnp.dot(mix.astype(BF16), wo_ref[...], preferred_element_type=F32)
    h1_ref[...] = h1
    ms = jnp.mean(h1 * h1, axis=-1, keepdims=True)
    hn = h1 * lax.rsqrt(ms + EPS) * gffn_ref[...]
    hn_ref[...] = hn
    logits = jnp.dot(hn.astype(BF16), wrt_ref[...], preferred_element_type=F32) + brt_ref[...]
    lane = lax.broadcasted_iota(I32, (tm, LANES), 1)
    is_g = (lane >= N_EXPERTS) & (lane < N_EXPERTS + N_GROUPS)
    glog = jnp.where(is_g, logits, -jnp.inf)
    gmax = jnp.max(glog, axis=-1, keepdims=True)
    gidx = jnp.min(jnp.where(glog == gmax, lane - N_EXPERTS, LANES), axis=-1, keepdims=True)
    gden = jnp.sum(jnp.where(is_g, jnp.exp(glog - gmax), 0.0), axis=-1, keepdims=True)
    gp = 1.0 / gden
    in_g = (lane < N_EXPERTS) & ((lane // EXPERTS_PER_GROUP) == gidx)
    e1 = jnp.where(in_g, logits, -jnp.inf)
    v1 = jnp.max(e1, axis=-1, keepdims=True)
    i1 = jnp.min(jnp.where(e1 == v1, lane, LANES), axis=-1, keepdims=True)
    e2 = jnp.where(lane == i1, -jnp.inf, e1)
    v2 = jnp.max(e2, axis=-1, keepdims=True)
    i2 = jnp.min(jnp.where(e2 == v2, lane, LANES), axis=-1, keepdims=True)
    tt = jnp.exp(v2 - v1)
    w1 = gp / (1.0 + tt)
    w2 = gp * tt / (1.0 + tt)
    oh = jnp.where((lane == i1) | (lane == i2), 1.0, 0.0)
    rr = lax.broadcasted_iota(I32, (tm, tm), 0)
    cc = lax.broadcasted_iota(I32, (tm, tm), 1)
    tri = jnp.where(cc < rr, 1.0, 0.0).astype(BF16)
    before = jnp.dot(tri, oh.astype(BF16), preferred_element_type=F32) + carry_ref[...]
    rank1 = jnp.sum(jnp.where(lane == i1, before, 0.0), axis=-1, keepdims=True)
    rank2 = jnp.sum(jnp.where(lane == i2, before, 0.0), axis=-1, keepdims=True)
    carry = carry_ref[...] + jnp.sum(oh, axis=0, keepdims=True)
    carry_ref[...] = carry
    cnt_ref[...] = carry
    cols = [i1.astype(F32), i2.astype(F32), w1, w2, rank1, rank2]
    route = jnp.zeros((tm, LANES), F32)
    for ci, val in enumerate(cols):
        route = jnp.where(lane == ci, val, route)
    route_ref[...] = route


def _finish(a, r, rest, h, wa, wr, wo, gffn, wrt, brt, carry0, tm):
    m, d = h.shape
    aw = a.shape[1]
    assert rest.shape[1] == 3 * d and 2 * aw == d
    const = lambda shp: pl.BlockSpec(shp, lambda i: (0,) * len(shp))
    return pl.pallas_call(
        functools.partial(_finish_kernel, tm=tm),
        out_shape=(jax.ShapeDtypeStruct((m, d), F32),
                   jax.ShapeDtypeStruct((m, d), F32),
                   jax.ShapeDtypeStruct((m, LANES), F32),
                   jax.ShapeDtypeStruct((1, LANES), F32)),
        grid=(m // tm,),
        in_specs=[pl.BlockSpec((tm, aw), lambda i: (i, 0)),
                  pl.BlockSpec((tm, aw), lambda i: (i, 0)),
                  pl.BlockSpec((tm, d), lambda i: (i, 1)),
                  pl.BlockSpec((tm, d), lambda i: (i, 2)),
                  pl.BlockSpec((tm, d), lambda i: (i, 0)),
                  const(wa.shape), const(wr.shape), const(wo.shape), const((1, d)),
                  const(wrt.shape), const((1, LANES)), const((1, LANES))],
        out_specs=(pl.BlockSpec((tm, d), lambda i: (i, 0)),
                   pl.BlockSpec((tm, d), lambda i: (i, 0)),
                   pl.BlockSpec((tm, LANES), lambda i: (i, 0)),
                   pl.BlockSpec((1, LANES), lambda i: (0, 0))),
        scratch_shapes=[pltpu.VMEM((1, LANES), F32)],
        compiler_params=_params(("arbitrary",)),
        name="finish",
    )(a, r, rest, rest, h, wa, wr, wo, gffn.reshape(1, d), wrt, brt, carry0)


def _moe_kernel(item_e, item_start, item_n, row_src,
                hn_hbm, wg_ref, wu_ref, wd_ref, yk_hbm,
                xg, yacc, gsem, ssem, *, sub):
    i = pl.program_id(0)
    f = pl.program_id(1)
    ni = pl.num_programs(0)
    nf = pl.num_programs(1)
    n = item_n[i]
    slot = lax.rem(i, 2)

    grp = 8

    def gather_copy(s, r, tok, rows=1):
        return pltpu.make_async_copy(hn_hbm.at[pl.ds(tok, rows)], xg.at[s, pl.ds(r, rows)], gsem)

    def scatter_copy(r, k, tok, rows=1):
        return pltpu.make_async_copy(yacc.at[pl.ds(r, rows)], yk_hbm.at[k, pl.ds(tok, rows)], ssem)

    def repeat(count, fn):
        def body(t, c):
            fn(t)
            return c
        lax.fori_loop(0, count, body, 0)

    def start_gather(item, s):
        first = item_start[item]

        def issue(t):
            for u in range(grp):
                r = t * grp + u
                gather_copy(s, r, row_src[first + r] >> 1).start()
        repeat((item_n[item] + grp - 1) // grp, issue)

    def wait_gather(count):
        repeat((count + grp - 1) // grp, lambda t: gather_copy(0, 0, 0, grp).wait())

    def wait_scatter(count):
        repeat(count // grp, lambda t: scatter_copy(0, 0, 0, grp).wait())
        repeat(lax.rem(count, grp), lambda t: scatter_copy(0, 0, 0).wait())

    @pl.when(f == 0)
    def _():
        @pl.when(i == 0)
        def _():
            xg[...] = jnp.zeros(xg.shape, xg.dtype)
            start_gather(0, 0)

        @pl.when(i > 0)
        def _():
            wait_scatter(item_n[i - 1])
        wait_gather(n)

    @pl.when((f == 1) & (i + 1 < ni))
    def _():
        start_gather(i + 1, 1 - slot)

    @pl.when(n > 0)
    def _():
        def run_rows(off, m):
            x = xg[slot, pl.ds(off, m), :]
            hg = jnp.dot(x, wg_ref[0], preferred_element_type=F32)
            hu = jnp.dot(x, wu_ref[0], preferred_element_type=F32)
            hm = hg * jax.nn.sigmoid(hg) * hu
            part = jnp.dot(hm, wd_ref[0], preferred_element_type=F32)

            @pl.when(f == 0)
            def _():
                yacc[pl.ds(off, m), :] = part

            @pl.when(f > 0)
            def _():
                yacc[pl.ds(off, m), :] += part

        nsub = (n + sub - 1) // sub
        for k in range(1, xg.shape[1] // sub + 1):
            @pl.when(nsub == k)
            def _(k=k):
                run_rows(0, k * sub)

    @pl.when(f == nf - 1)
    def _():
        first = item_start[i]

        def issue_row(r):
            src = row_src[first + r]
            scatter_copy(r, src & 1, src >> 1).start()

        def issue_group(t):
            for u in range(grp):
                issue_row(t * grp + u)
        repeat(n // grp, issue_group)
        tail = (n // grp) * grp
        repeat(n - tail, lambda t: issue_row(tail + t))

        @pl.when(i == ni - 1)
        def _():
            wait_scatter(n)


def _moe_plan(route, counts, rows_per_item, max_items):
    ntok = route.shape[0]
    eid = route[:, 0:2].astype(I32)
    rank = route[:, 4:6].astype(I32)
    counts = counts.astype(I32)
    ends = jnp.cumsum(counts)
    starts = ends - counts
    dest = (starts[eid] + rank).reshape(-1)
    src = jnp.arange(2 * ntok, dtype=I32)
    row_src = jnp.zeros((2 * ntok + 8,), I32).at[dest].set(src)
    nit = (counts + rows_per_item - 1) // rows_per_item
    it_end = jnp.cumsum(nit)
    it_first = it_end - nit
    total = it_end[-1]
    t = jnp.arange(max_items, dtype=I32)
    e = jnp.minimum(jnp.searchsorted(it_end, t, side='right'), N_EXPERTS - 1).astype(I32)
    j = t - it_first[e]
    n = jnp.clip(counts[e] - j * rows_per_item, 0, rows_per_item)
    n = jnp.where(t < total, n, 0).astype(I32)
    start = (starts[e] + j * rows_per_item).astype(I32)
    start = jnp.where(n > 0, start, 0)
    last_e = e[jnp.maximum(total - 1, 0)]
    e = jnp.where(t < total, e, last_e)
    return e, start, n, row_src


def _moe(hn, route, counts, wg, wu, wd):
    ntok, d = hn.shape
    ne, _, ff = wg.shape
    rows, sub, nf = MOE_ROWS, MOE_SUB, MOE_FSPLIT
    assert nf > 1 and rows % sub == 0
    tf = ff // nf
    max_items = ne + (2 * ntok) // rows
    item_e, item_start, item_n, row_src = _moe_plan(route, counts, rows, max_items)

    def f_eff(i, f, item_n):
        return jnp.where(item_n[i] > 0, f, nf - 1)

    return pl.pallas_call(
        functools.partial(_moe_kernel, sub=sub),
        out_shape=jax.ShapeDtypeStruct((2, ntok, d), F32),
        grid_spec=pltpu.PrefetchScalarGridSpec(
            num_scalar_prefetch=4,
            grid=(max_items, nf),
            in_specs=[pl.BlockSpec(memory_space=pl.ANY),
                      pl.BlockSpec((1, d, tf), lambda i, f, ie, ist, inn, rs: (ie[i], 0, f_eff(i, f, inn))),
                      pl.BlockSpec((1, d, tf), lambda i, f, ie, ist, inn, rs: (ie[i], 0, f_eff(i, f, inn))),
                      pl.BlockSpec((1, tf, d), lambda i, f, ie, ist, inn, rs: (ie[i], f_eff(i, f, inn), 0))],
            out_specs=pl.BlockSpec(memory_space=pl.ANY),
            scratch_shapes=[pltpu.VMEM((2, rows, d), F32),
                            pltpu.VMEM((rows, d), F32),
                            pltpu.SemaphoreType.DMA,
                            pltpu.SemaphoreType.DMA]),
        compiler_params=_params(("arbitrary", "arbitrary")),
        name="moe_experts",
    )(item_e, item_start, item_n, row_src, hn, wg, wu, wd)


def _ple_kernel(h1_ref, y0_ref, y1_ref, route_ref, pe_ref, gple_ref, wg_ref, wp_ref, gfin_ref,
                o_ref):
    route = route_ref[...]
    y = route[:, 2:3] * y0_ref[0] + route[:, 3:4] * y1_ref[0]
    h2 = h1_ref[...] + y
    ms = jnp.mean(h2 * h2, axis=-1, keepdims=True)
    hn = (h2 * lax.rsqrt(ms + EPS) * gple_ref[...]).astype(BF16)
    gate = jax.nn.sigmoid(jnp.dot(hn, wg_ref[...], preferred_element_type=F32))
    pp = jnp.dot(pe_ref[...].astype(BF16), wp_ref[...], preferred_element_type=F32)
    h3 = h2 + gate * pp
    ms = jnp.mean(h3 * h3, axis=-1, keepdims=True)
    o_ref[...] = h3 * lax.rsqrt(ms + EPS) * gfin_ref[...]


def _ple(h1, yk, row0, route, pe, gple, wg, wp, gfin, tm):
    m, d = h1.shape
    blk0 = row0 // tm
    const = lambda shp: pl.BlockSpec(shp, lambda i: (0,) * len(shp))
    return pl.pallas_call(
        _ple_kernel,
        out_shape=jax.ShapeDtypeStruct((m, d), F32),
        grid=(m // tm,),
        in_specs=[pl.BlockSpec((tm, d), lambda i: (i, 0)),
                  pl.BlockSpec((1, tm, d), lambda i: (0, i + blk0, 0)),
                  pl.BlockSpec((1, tm, d), lambda i: (1, i + blk0, 0)),
                  pl.BlockSpec((tm, LANES), lambda i: (i, 0)),
                  pl.BlockSpec((tm, pe.shape[1]), lambda i: (i, 0)),
                  const((1, d)), const(wg.shape), const(wp.shape), const((1, d))],
        out_specs=pl.BlockSpec((tm, d), lambda i: (i, 0)),
        compiler_params=_params(("arbitrary",)),
        name="ple_final",
    )(h1, yk, yk, route, pe, gple.reshape(1, d), wg, wp, gfin.reshape(1, d))


def _project_group(x, g, w_in, pos, tm, q_scale, k_seq_tiles=0):
    xn = _rmsnorm(x, g, tm)
    da_q = _rope_tables(pos, DA_ROT, DA_HD, ROPE_THETA, q_scale)[:, None]
    da_k = _rope_tables(pos, DA_ROT, DA_HD, ROPE_THETA, 1.0)[:, None]
    rq_t = _rope_tables(pos, RET_KD, RET_KD, RET_THETA, 1.0)
    rk_t = _rope_tables(pos, RET_KD, RET_KD, RET_THETA, RET_KD ** -0.5)
    ret_t = np.stack([rq_t, rk_t], axis=1)
    q = _proj(xn, w_in, COL_Q, 1024, tm, BF16, rope=(da_q, DA_ROT // 2))
    k = _proj(xn, w_in, COL_K, 1024, tm, F32, rope=(da_k, DA_ROT // 2), seq_tiles=k_seq_tiles)
    v = _proj(xn, w_in, COL_V, 1024, tm, F32)
    rqk = _proj(xn, w_in, COL_RQK, 1024, tm, F32, rope=(ret_t, RET_KD // 2))
    rest = _proj(xn, w_in, COL_RV, 6144, tm, F32)
    return q, k, v, rqk, rest


def kernel(x_prompt, x_sample, cache_k, cache_v, state_ret, page_table, p_prompt, p_sample, norm_mix_g, w_in, lam_q1, lam_k1, lam_q2, lam_k2, da_norm_g, ret_norm_g, w_br_attn, w_br_ret, w_out, norm_ffn_g, w_router_group, b_router_group, w_router_expert, b_router_expert, w_exp_gate, w_exp_up, w_exp_down, norm_ple_g, w_ple_gate, w_ple_proj, final_norm_g):
    bp, sp, d = x_prompt.shape
    bs, ts, _ = x_sample.shape
    depth = w_in.shape[0]
    assert depth == 1, "the final norm is fused into the last stage of a single layer"
    n_pool, page = cache_k.shape[1], cache_k.shape[2]
    mp, msm = bp * sp, bs * ts
    log_g = np.log1p(-np.exp2(-5.0 - np.arange(RET_HEADS))).astype(np.float32)
    pos_p = np.arange(sp)
    pos_s = np.tile(PAST_LEN + np.arange(ts), bs)
    cache_k2 = cache_k.transpose(0, 1, 3, 4, 5, 2).reshape(depth * n_pool, DA_HEADS * 2 * DA_HD, page)
    cache_v2 = cache_v.reshape(depth * n_pool, page * DA_HEADS, DA_VD)
    tm_p = next(t for t in (1024, 512, 256, 128) if sp % t == 0)
    tf_p = 256
    tl_p = 512 if mp % 512 == 0 else 256

    hp = x_prompt.reshape(mp, d)
    hs = x_sample.reshape(msm, d)
    kp_l, vp_l, rp_l, ks_l, vs_l, rs_l = [], [], [], [], [], []
    for li in range(depth):
        lam_init = 0.8 - 0.6 * math.exp(-0.3 * li)
        lam = (jnp.exp(jnp.sum(lam_q1[li] * lam_k1[li]))
               - jnp.exp(jnp.sum(lam_q2[li] * lam_k2[li])) + lam_init).reshape(1).astype(F32)
        out_scale = 1.0 - lam_init
        wa = w_br_attn[li].astype(BF16)
        wr = w_br_ret[li].astype(BF16)
        wo = w_out[li].astype(BF16)
        wpg = w_ple_gate[li].astype(BF16)
        wpp = w_ple_proj[li].astype(BF16)
        wrt = jnp.zeros((d, LANES), F32)
        wrt = wrt.at[:, :N_EXPERTS].set(w_router_expert[li])
        wrt = wrt.at[:, N_EXPERTS:N_EXPERTS + N_GROUPS].set(w_router_group[li]).astype(BF16)
        brt = jnp.zeros((1, LANES), F32)
        brt = brt.at[0, :N_EXPERTS].set(b_router_expert[li])
        brt = brt.at[0, N_EXPERTS:N_EXPERTS + N_GROUPS].set(b_router_group[li])
        da_g = da_norm_g[li].reshape(1, -1)

        q, k, v, rqk, rest = _project_group(hp, norm_mix_g[li], w_in[li], pos_p, tm_p,
                                            DA_HD ** -0.5 * math.log2(math.e),
                                            k_seq_tiles=sp // tm_p)
        a_p = _attn_prompt(lam, q, k, v, da_g, bp, sp, out_scale, tq=256, hp=4)
        chunk = 256 if sp % 256 == 0 else 128
        r_p, st_p = _retention(rqk.reshape(bp, sp, -1), rest.reshape(bp, sp, -1),
                               jnp.zeros((bp, RET_HEADS, RET_KD, RET_VD), F32),
                               ret_norm_g[li], log_g, chunk, chunk)
        kp_l.append(k.reshape(bp, DA_HEADS, 2, DA_HD, sp).transpose(0, 4, 1, 2, 3))
        vp_l.append(v.reshape(bp, sp, DA_HEADS, DA_VD))
        rp_l.append(st_p)
        h1_p, hn_p, route_p, cnt_p = _finish(
            a_p, r_p.reshape(mp, -1), rest, hp, wa, wr, wo, norm_ffn_g[li], wrt, brt,
            jnp.zeros((1, LANES), F32), tf_p)

        q, k, v, rqk, rest = _project_group(hs, norm_mix_g[li], w_in[li], pos_s, msm,
                                            DA_HD ** -0.5)
        tpad = 8
        pad_t = lambda z: jnp.pad(z.reshape(bs, ts, -1), ((0, 0), (0, tpad - ts), (0, 0)))
        q4 = q.astype(F32).reshape(bs, ts, DA_HEADS, 2, DA_HD)
        eye = jnp.eye(2, dtype=F32)
        qm = jnp.einsum('bthmd,mn->bhmtnd', q4, eye).reshape(bs, DA_HEADS * 2 * ts, 2 * DA_HD)
        a_s = _attn_paged(page_table, lam, qm, pad_t(k), pad_t(v), da_g, cache_k2, cache_v2,
                          li * n_pool, ts, out_scale, npg=8)
        r_s, st_s = _retention(pad_t(rqk), pad_t(rest), state_ret[li],
                               ret_norm_g[li], log_g, tpad, ts)
        ks_l.append(k.reshape(bs, ts, DA_HEADS, 2, DA_HD))
        vs_l.append(v.reshape(bs, ts, DA_HEADS, DA_VD))
        rs_l.append(st_s)
        h1_s, hn_s, route_s, cnt = _finish(
            a_s.reshape(msm, -1), r_s[:, :ts].reshape(msm, -1), rest, hs, wa, wr, wo,
            norm_ffn_g[li], wrt, brt, cnt_p, msm)

        hn_all = jnp.concatenate([hn_p, hn_s], axis=0)
        route_all = jnp.concatenate([route_p, route_s], axis=0)
        yk = _moe(hn_all, route_all, cnt[0, :N_EXPERTS], w_exp_gate[li], w_exp_up[li],
                  w_exp_down[li])

        gfin = final_norm_g
        hp = _ple(h1_p, yk, 0, route_p, p_prompt[li].reshape(mp, -1), norm_ple_g[li], wpg, wpp,
                  gfin, tl_p)
        hs = _ple(h1_s, yk, mp, route_s, p_sample[li].reshape(msm, -1), norm_ple_g[li], wpg, wpp,
                  gfin, msm)
    y_prompt = hp.reshape(bp, sp, d)
    y_sample = hs.reshape(bs, ts, d)
    return (y_prompt, y_sample, jnp.stack(kp_l), jnp.stack(vp_l), jnp.stack(rp_l),
            jnp.stack(ks_l), jnp.stack(vs_l), jnp.stack(rs_l))
```
